```python
import jax, jax.numpy as jnp
from jax import lax
import numpy as np

D_MODEL = 2048
BATCH = 4
SEQ = 2048
DEPTH = 1
DEC_BATCH = 16
DEC_SEQ = 16
PAST_LEN = 2048

CHUNK = 64
N_HEADS = 8
HEAD_DIM = 128
D_ATTN = N_HEADS * HEAD_DIM
D_CONV = 1024
CONV_W = 3
D_FF = 5632
Q_BLOCK = 128
EPS = 1e-6
FFN_SCALE = 0.5
SPLIT_POINTS = (D_CONV, 2 * D_CONV, 3 * D_CONV,
                3 * D_CONV + D_ATTN, 3 * D_CONV + 2 * D_ATTN, 3 * D_CONV + 3 * D_ATTN,
                3 * D_CONV + 3 * D_ATTN + D_MODEL)
D_IN = 3 * D_CONV + 3 * D_ATTN + 2 * D_MODEL

kernel_name = "hybrid_shortconv_stickbreaking_streaming_step"


def _rmsnorm(x, g):
    xf = x.astype(jnp.float32)
    y = xf * lax.rsqrt(jnp.mean(xf * xf, axis=-1, keepdims=True) + EPS)
    return (y * g.astype(jnp.float32)).astype(x.dtype)


def _swiglu(x, w_gate_up, w_down):
    gu = jnp.einsum("btd,df->btf", x, w_gate_up)
    g, u = jnp.split(gu, 2, axis=-1)
    return jnp.einsum("btf,fd->btd", jax.nn.silu(g) * u, w_down)


def _causal_conv(u, hist, w):
    T = u.shape[1]
    full = jnp.concatenate([hist.astype(u.dtype), u], axis=1)
    y = full[:, 0:T] * w[0]
    for i in range(1, CONV_W):
        y = y + full[:, i:i + T] * w[i]
    return y, full[:, -(CONV_W - 1):]


def _stick_breaking_block(q, k, v, q_start):
    Tq, Tk = q.shape[1], k.shape[1]
    z = jnp.einsum("bqhd,bkhd->bhqk", q, k,
                   preferred_element_type=jnp.float32) * (HEAD_DIM ** -0.5)
    t_pos = q_start + jnp.arange(Tq)[:, None]
    s_pos = jnp.arange(Tk)[None, :]
    mask = s_pos < t_pos
    sp = jnp.where(mask, jax.nn.softplus(z), 0.0)
    tail = lax.cumsum(sp, axis=3, reverse=True) - sp
    log_a = jnp.where(mask, jax.nn.log_sigmoid(z) - tail, -jnp.inf)
    a = jnp.exp(log_a)
    return jnp.einsum("bhqk,bkhd->bqhd", a.astype(v.dtype), v)


def _stick_breaking(q, k_all, v_all, q_start):
    T = q.shape[1]
    n_blocks = (T + Q_BLOCK - 1) // Q_BLOCK
    outs = []
    for i in range(n_blocks):
        lo, hi = i * Q_BLOCK, min((i + 1) * Q_BLOCK, T)
        end = q_start + hi
        outs.append(_stick_breaking_block(q[:, lo:hi], k_all[:, :end], v_all[:, :end], q_start + lo))
    return jnp.concatenate(outs, axis=1)


def _layer(x, conv_hist, past_k, past_v, q_start,
           g_ffn1, w_gu1, w_dn1, g_mix, w_in, w_conv, w_conv_out, w_attn_out, w_o,
           g_ffn2, w_gu2, w_dn2):
    B, T, _ = x.shape
    x = x + FFN_SCALE * _swiglu(_rmsnorm(x, g_ffn1), w_gu1, w_dn1)
    h = _rmsnorm(x, g_mix)
    proj = jnp.einsum("btd,de->bte", h, w_in)
    cb, cc, cx, q, k, v, ga, gb = jnp.split(proj, SPLIT_POINTS, axis=-1)
    conv_out, new_hist = _causal_conv(cc * cx, conv_hist, w_conv)
    y_a = jnp.einsum("btc,cd->btd", cb * conv_out, w_conv_out)
    q = q.reshape(B, T, N_HEADS, HEAD_DIM)
    k = k.reshape(B, T, N_HEADS, HEAD_DIM)
    v = v.reshape(B, T, N_HEADS, HEAD_DIM)
    if past_k is None:
        k_all, v_all = k, v
    else:
        k_all = jnp.concatenate([past_k.astype(k.dtype), k], axis=1)
        v_all = jnp.concatenate([past_v.astype(v.dtype), v], axis=1)
    o = _stick_breaking(q, k_all, v_all, q_start).reshape(B, T, D_ATTN)
    y_b = jnp.einsum("bte,ed->btd", o, w_attn_out)
    mixed = jax.nn.sigmoid(ga) * y_a + jax.nn.sigmoid(gb) * y_b
    x = x + jnp.einsum("btd,de->bte", mixed, w_o)
    x = x + FFN_SCALE * _swiglu(_rmsnorm(x, g_ffn2), w_gu2, w_dn2)
    return x, new_hist, k, v


def setup_inputs(seed: int = 0) -> dict:
    key = jax.random.key(seed)
    ks = jax.random.split(key, 20)
    f32 = jnp.float32

    def nrm(k, shape, fan_in):
        return jax.random.normal(k, shape, f32) * (fan_in ** -0.5)

    def gain(k, shape):
        return 1.0 + 0.02 * jax.random.normal(k, shape, f32)

    return {
        "x_prompt": jax.random.normal(ks[0], (BATCH, SEQ, D_MODEL), f32),
        "x_sample": jax.random.normal(ks[1], (DEC_BATCH, DEC_SEQ, D_MODEL), f32),
        "cache_k": jax.random.normal(ks[2], (DEPTH, DEC_BATCH, PAST_LEN, N_HEADS, HEAD_DIM), f32),
        "cache_v": jax.random.normal(ks[3], (DEPTH, DEC_BATCH, PAST_LEN, N_HEADS, HEAD_DIM), f32),
        "state_conv": jax.random.normal(ks[4], (DEPTH, DEC_BATCH, CONV_W - 1, D_CONV), f32),
        "norm_ffn1": gain(ks[5], (DEPTH, D_MODEL)),
        "ffn1_w_gate_up": nrm(ks[6], (DEPTH, D_MODEL, 2 * D_FF), D_MODEL),
        "ffn1_w_down": nrm(ks[7], (DEPTH, D_FF, D_MODEL), D_FF),
        "norm_mix": gain(ks[8], (DEPTH, D_MODEL)),
        "w_in": nrm(ks[9], (DEPTH, D_MODEL, D_IN), D_MODEL),
        "conv_w": nrm(ks[10], (DEPTH, CONV_W, D_CONV), CONV_W),
        "w_conv_out": nrm(ks[11], (DEPTH, D_CONV, D_MODEL), D_CONV),
        "w_attn_out": nrm(ks[12], (DEPTH, D_ATTN, D_MODEL), D_ATTN),
        "w_o": nrm(ks[13], (DEPTH, D_MODEL, D_MODEL), D_MODEL),
        "norm_ffn2": gain(ks[14], (DEPTH, D_MODEL)),
        "ffn2_w_gate_up": nrm(ks[15], (DEPTH, D_MODEL, 2 * D_FF), D_MODEL),
        "ffn2_w_down": nrm(ks[16], (DEPTH, D_FF, D_MODEL), D_FF),
        "norm_final": gain(ks[17], (D_MODEL,)),
    }


def reference(x_prompt, x_sample, cache_k, cache_v, state_conv,
              norm_ffn1, ffn1_w_gate_up, ffn1_w_down, norm_mix, w_in, conv_w,
              w_conv_out, w_attn_out, w_o, norm_ffn2, ffn2_w_gate_up, ffn2_w_down,
              norm_final):
    xp, xs = x_prompt, x_sample
    kp_l, vp_l, cp_l, ks_l, vs_l, cs_l = [], [], [], [], [], []
    for l in range(DEPTH):
        w = (norm_ffn1[l], ffn1_w_gate_up[l], ffn1_w_down[l], norm_mix[l], w_in[l], conv_w[l],
             w_conv_out[l], w_attn_out[l], w_o[l], norm_ffn2[l], ffn2_w_gate_up[l], ffn2_w_down[l])
        hist0 = jnp.zeros((xp.shape[0], CONV_W - 1, D_CONV), xp.dtype)
        xp, c_p, k_p, v_p = _layer(xp, hist0, None, None, 0, *w)
        xs, c_s, k_s, v_s = _layer(xs, state_conv[l], cache_k[l], cache_v[l], PAST_LEN, *w)
        kp_l.append(k_p); vp_l.append(v_p); cp_l.append(c_p)
        ks_l.append(k_s); vs_l.append(v_s); cs_l.append(c_s)
    y_prompt = _rmsnorm(xp, norm_final)
    y_sample = _rmsnorm(xs, norm_final)
    k_prompt = jnp.stack(kp_l, axis=0)
    v_prompt = jnp.stack(vp_l, axis=0)
    conv_prompt = jnp.stack(cp_l, axis=0)
    k_sample = jnp.stack(ks_l, axis=0)
    v_sample = jnp.stack(vs_l, axis=0)
    conv_sample = jnp.stack(cs_l, axis=0)
    return (y_prompt, y_sample, k_prompt, v_prompt, conv_prompt, k_sample, v_sample, conv_sample)
```

```python
import functools

import jax
import jax.numpy as jnp
from jax import lax
from jax.experimental import pallas as pl
from jax.experimental.pallas import tpu as pltpu

N_HEADS = 8
HEAD_DIM = 128
CONV_W = 3
EPS = 1e-6
FFN_SCALE = 0.5

F32 = jnp.float32
BF16 = jnp.bfloat16

V7X_VMEM_LIMIT_CAP = 56 * 1024 * 1024

FFN_TM = 512
FFN_TF = 512
MM_TM = 512
MM_TN = 1024
MIX_TM = 256
ATT_TQ = 256
ATT_BK = 256


def _vmem_limit(*block_bytes, scratch=0):
    est = 2 * sum(block_bytes) + scratch
    return int(min(V7X_VMEM_LIMIT_CAP, max(2 * est, 16 * 1024 * 1024)))


def _nbytes(shape, dtype):
    n = 1
    for s in shape:
        n *= s
    return n * jnp.dtype(dtype).itemsize


def _rmsnorm(x, g):
    ms = jnp.mean(x * x, axis=-1, keepdims=True)
    return x * lax.rsqrt(ms + EPS) * g


def _dot(a, b):
    return jnp.dot(a, b, preferred_element_type=F32)


def _dot_nt(a, b):
    return lax.dot_general(a, b, (((1,), (1,)), ((), ())), preferred_element_type=F32)


def _ffn_body(x_ref, gin_ref, wg_ref, wu_ref, wd_ref, gout_ref, *rest, emit_x):
    if emit_x:
        xo_ref, ho_ref, h_scr, acc_scr = rest
    else:
        ho_ref, h_scr, acc_scr = rest
    j = pl.program_id(1)

    @pl.when(j == 0)
    def _():
        h_scr[...] = _rmsnorm(x_ref[...], gin_ref[...]).astype(BF16)
        acc_scr[...] = jnp.zeros_like(acc_scr)

    h = h_scr[...]
    g = _dot(h, wg_ref[...])
    u = _dot(h, wu_ref[...])
    act = (g * (1.0 / (1.0 + jnp.exp(-g))) * u).astype(BF16)
    acc_scr[...] += _dot(act, wd_ref[...])

    @pl.when(j == pl.num_programs(1) - 1)
    def _():
        xn = x_ref[...] + FFN_SCALE * acc_scr[...]
        if emit_x:
            xo_ref[...] = xn
        ho_ref[...] = _rmsnorm(xn, gout_ref[...]).astype(ho_ref.dtype)


def _ffn(x, g_in, w_gu, w_dn, g_out, *, emit_x, h_dtype):
    T, D = x.shape
    d_ff = w_dn.shape[0]
    tm = min(FFN_TM, T)
    tf = FFN_TF
    assert T % tm == 0 and d_ff % tf == 0
    n_f = d_ff // tf
    grid = (T // tm, n_f)
    in_specs = [
        pl.BlockSpec((tm, D), lambda i, j: (i, 0)),
        pl.BlockSpec((1, D), lambda i, j: (0, 0)),
        pl.BlockSpec((D, tf), lambda i, j: (0, j)),
        pl.BlockSpec((D, tf), lambda i, j: (0, j + n_f)),
        pl.BlockSpec((tf, D), lambda i, j: (j, 0)),
        pl.BlockSpec((1, D), lambda i, j: (0, 0)),
    ]
    out_shape = [jax.ShapeDtypeStruct((T, D), h_dtype)]
    out_specs = [pl.BlockSpec((tm, D), lambda i, j: (i, 0))]
    if emit_x:
        out_shape = [jax.ShapeDtypeStruct((T, D), F32)] + out_shape
        out_specs = [pl.BlockSpec((tm, D), lambda i, j: (i, 0))] + out_specs
    limit = _vmem_limit(
        _nbytes((tm, D), F32), 2 * _nbytes((D, tf), BF16), _nbytes((tf, D), BF16),
        _nbytes((tm, D), F32) * (2 if emit_x else 1),
        scratch=_nbytes((tm, D), BF16) + _nbytes((tm, D), F32))
    outs = pl.pallas_call(
        functools.partial(_ffn_body, emit_x=emit_x),
        grid=grid, in_specs=in_specs, out_specs=out_specs, out_shape=out_shape,
        scratch_shapes=[pltpu.VMEM((tm, D), BF16), pltpu.VMEM((tm, D), F32)],
        compiler_params=pltpu.CompilerParams(
            dimension_semantics=("parallel", "arbitrary"), vmem_limit_bytes=limit),
        name="ffn_mid" if emit_x else "ffn_final",
    )(x, g_in.reshape(1, D), w_gu, w_gu, w_dn, g_out.reshape(1, D))
    return outs if emit_x else outs[0]


def _mm_body(x_ref, w_ref, o_ref):
    o_ref[...] = _dot(x_ref[...], w_ref[...]).astype(o_ref.dtype)


def _mm_res_body(x_ref, w_ref, r_ref, o_ref):
    o_ref[...] = r_ref[...] + _dot(x_ref[...], w_ref[...])


def _matmul(x, w, col0, n, out_dtype, residual=None):
    T, K = x.shape
    tm = min(MM_TM, T)
    tn = min(MM_TN, n)
    assert T % tm == 0 and n % tn == 0 and col0 % tn == 0
    c0 = col0 // tn
    grid = (T // tm, n // tn)
    in_specs = [
        pl.BlockSpec((tm, K), lambda i, j: (i, 0)),
        pl.BlockSpec((K, tn), lambda i, j: (0, j + c0)),
    ]
    args = [x, w]
    body = _mm_body
    blocks = [_nbytes((tm, K), x.dtype), _nbytes((K, tn), w.dtype), _nbytes((tm, tn), out_dtype)]
    if residual is not None:
        in_specs.append(pl.BlockSpec((tm, tn), lambda i, j: (i, j)))
        args.append(residual)
        body = _mm_res_body
        blocks.append(_nbytes((tm, tn), F32))
    return pl.pallas_call(
        body, grid=grid, in_specs=in_specs,
        out_specs=pl.BlockSpec((tm, tn), lambda i, j: (i, j)),
        out_shape=jax.ShapeDtypeStruct((T, n), out_dtype),
        compiler_params=pltpu.CompilerParams(
            dimension_semantics=("parallel", "arbitrary"),
            vmem_limit_bytes=_vmem_limit(*blocks)),
        name="proj",
    )(*args)


def _tri_ones(bk, n_ones):
    j = lax.broadcasted_iota(jnp.int32, (bk, bk), 0)
    s = lax.broadcasted_iota(jnp.int32, (bk, bk), 1)
    tri = (j >= s).astype(BF16)
    return jnp.concatenate([tri, jnp.ones((bk, n_ones), BF16)], axis=1)


def _softplus(z):
    return jnp.maximum(z, 0.0) + jnp.log(1.0 + jnp.exp(-jnp.abs(z)))


def _cumsum_mxu(sp, tri_ones):
    hi = sp.astype(BF16)
    lo = (sp - hi.astype(F32)).astype(BF16)
    return _dot(hi, tri_ones) + _dot(lo, tri_ones)


def _sb_block(q, k_blk, v_blk, tri_ones, run, mask):
    bk = k_blk.shape[0]
    z = _dot_nt(q, k_blk) * (HEAD_DIM ** -0.5)
    sp = _softplus(z)
    if mask is not None:
        sp = jnp.where(mask, sp, 0.0)
    cs = _cumsum_mxu(sp, tri_ones)
    a = jnp.exp(z - (cs[:, :bk] + run))
    if mask is not None:
        a = jnp.where(mask, a, 0.0)
    return _dot(a.astype(BF16), v_blk), run + cs[:, bk:]


def _attn_prompt_body(q_ref, k_ref, v_ref, t1_ref, o_ref, run_scr, acc_scr):
    qi = pl.program_id(2)
    tq, bk = ATT_TQ, ATT_BK
    q = q_ref[...]
    tri_ones = t1_ref[...]

    def kv(kb):
        start = pl.multiple_of(kb * bk, bk)
        return (k_ref[pl.ds(start, bk), :].astype(BF16),
                v_ref[pl.ds(start, bk), :].astype(BF16))

    t = lax.broadcasted_iota(jnp.int32, (tq, bk), 0)
    s = lax.broadcasted_iota(jnp.int32, (tq, bk), 1)
    k_blk, v_blk = kv(qi)
    pv, run = _sb_block(q, k_blk, v_blk, tri_ones, jnp.zeros((tq, bk), F32), s < t)
    acc_scr[...] = pv
    run_scr[...] = run

    def step(it, carry):
        k_blk, v_blk = kv(qi - 1 - it)
        pv, run = _sb_block(q, k_blk, v_blk, tri_ones, run_scr[...], None)
        acc_scr[...] += pv
        run_scr[...] = run
        return carry

    lax.fori_loop(0, qi, step, 0)
    o_ref[...] = acc_scr[...].astype(o_ref.dtype)


def _attn_prompt(q, k, v, batch, seq):
    tq, bk = ATT_TQ, ATT_BK
    assert tq == bk and seq % tq == 0
    nq = seq // tq
    hd = HEAD_DIM
    grid = (batch, N_HEADS, nq)
    return pl.pallas_call(
        _attn_prompt_body, grid=grid,
        in_specs=[
            pl.BlockSpec((tq, hd), lambda b, h, i: (b * nq + i, h)),
            pl.BlockSpec((seq, hd), lambda b, h, i: (b, h)),
            pl.BlockSpec((seq, hd), lambda b, h, i: (b, h)),
            pl.BlockSpec((bk, 2 * bk), lambda b, h, i: (0, 0)),
        ],
        out_specs=pl.BlockSpec((tq, hd), lambda b, h, i: (b * nq + i, h)),
        out_shape=jax.ShapeDtypeStruct(q.shape, BF16),
        scratch_shapes=[pltpu.VMEM((tq, bk), F32), pltpu.VMEM((tq, hd), F32)],
        compiler_params=pltpu.CompilerParams(
            dimension_semantics=("parallel", "parallel", "arbitrary"),
            vmem_limit_bytes=_vmem_limit(2 * _nbytes((seq, hd), F32), _nbytes((bk, 2 * bk), BF16))),
        name="attn_prompt",
    )(q, k, v, _tri_ones(bk, bk))


def _attn_sample_body(q_ref, kn_ref, vn_ref, kc_ref, vc_ref, t1_ref, tn_ref, on_ref, o_ref,
                      qbd_scr, run_scr, acc_scr, *, n_new):
    step = pl.program_id(1)
    rows = N_HEADS * n_new
    d_all = N_HEADS * HEAD_DIM
    bk = ATT_BK

    @pl.when(step == 0)
    def _():
        q_rep = jnp.concatenate([q_ref[...]] * N_HEADS, axis=0)
        r = lax.broadcasted_iota(jnp.int32, (rows, d_all), 0)
        c = lax.broadcasted_iota(jnp.int32, (rows, d_all), 1)
        qbd = jnp.where(r // n_new == c // HEAD_DIM, q_rep, jnp.zeros_like(q_rep))
        qbd_scr[...] = qbd
        k_new = kn_ref[...].astype(BF16)
        v_new = vn_ref[...].astype(BF16)
        z = _dot_nt(qbd, k_new) * (HEAD_DIM ** -0.5)
        t = lax.broadcasted_iota(jnp.int32, (rows, n_new), 0) % n_new
        s = lax.broadcasted_iota(jnp.int32, (rows, n_new), 1)
        mask = s < t
        sp = jnp.where(mask, _softplus(z), 0.0)
        rc = _cumsum_mxu(sp, tn_ref[...])
        a = jnp.where(mask, jnp.exp(z - rc), 0.0)
        acc_scr[...] = _dot(a.astype(BF16), v_new)
        run_scr[...] = _cumsum_mxu(sp, on_ref[...])

    @pl.when(step > 0)
    def _():
        pv, run = _sb_block(qbd_scr[...], kc_ref[...].astype(BF16), vc_ref[...].astype(BF16),
                            t1_ref[...], run_scr[...], None)
        acc_scr[...] += pv
        run_scr[...] = run

    @pl.when(step == pl.num_programs(1) - 1)
    def _():
        for h in range(N_HEADS):
            o_ref[:, h * HEAD_DIM:(h + 1) * HEAD_DIM] = acc_scr[
                h * n_new:(h + 1) * n_new, h * HEAD_DIM:(h + 1) * HEAD_DIM].astype(o_ref.dtype)


def _attn_sample(q, k_new, v_new, cache_k, cache_v, n_streams, n_new, past_len):
    bk = ATT_BK
    assert past_len % bk == 0
    nkb = past_len // bk
    d_all = N_HEADS * HEAD_DIM
    rows = N_HEADS * n_new

    def cache_map(b, s):
        return (b * nkb + nkb - jnp.maximum(s, 1), 0)

    tri_new = _tri_ones(n_new, 0)
    ones_new = jnp.ones((n_new, bk), BF16)
    return pl.pallas_call(
        functools.partial(_attn_sample_body, n_new=n_new),
        grid=(n_streams, nkb + 1),
        in_specs=[
            pl.BlockSpec((n_new, d_all), lambda b, s: (b, 0)),
            pl.BlockSpec((n_new, d_all), lambda b, s: (b, 0)),
            pl.BlockSpec((n_new, d_all), lambda b, s: (b, 0)),
            pl.BlockSpec((bk, d_all), cache_map),
            pl.BlockSpec((bk, d_all), cache_map),
            pl.BlockSpec((bk, 2 * bk), lambda b, s: (0, 0)),
            pl.BlockSpec((n_new, n_new), lambda b, s: (0, 0)),
            pl.BlockSpec((n_new, bk), lambda b, s: (0, 0)),
        ],
        out_specs=pl.BlockSpec((n_new, d_all), lambda b, s: (b, 0)),
        out_shape=jax.ShapeDtypeStruct(q.shape, BF16),
        scratch_shapes=[pltpu.VMEM((rows, d_all), BF16), pltpu.VMEM((rows, bk), F32),
                        pltpu.VMEM((rows, d_all), F32)],
        compiler_params=pltpu.CompilerParams(
            dimension_semantics=("parallel", "arbitrary"),
            vmem_limit_bytes=_vmem_limit(2 * _nbytes((bk, d_all), F32),
                                         scratch=_nbytes((rows, d_all), F32))),
        name="attn_sample",
    )(q, k_new, v_new, cache_k, cache_v, _tri_ones(bk, bk), tri_new, ones_new)


def _sigmoid(x):
    return 1.0 / (1.0 + jnp.exp(-x))


def _merge(cb, conv, o, ga, gb, wco_ref, wao_ref, m_ref):
    y_a = _dot((cb * conv).astype(BF16), wco_ref[...])
    y_b = _dot(o, wao_ref[...])
    m_ref[...] = (_sigmoid(ga) * y_a + _sigmoid(gb) * y_b).astype(m_ref.dtype)


def _mix_prompt_body(cb_ref, cc_ref, cx_ref, o_ref, ga_ref, gb_ref, cw_ref, wco_ref, wao_ref,
                     m_ref, hist_ref, carry_scr, *, tiles_per_seq):
    i = pl.program_id(0)
    tm = cc_ref.shape[0]

    @pl.when(i % tiles_per_seq == 0)
    def _():
        carry_scr[...] = jnp.zeros_like(carry_scr)

    u = cc_ref[...] * cx_ref[...]
    row = lax.broadcasted_iota(jnp.int32, u.shape, 0)
    p1 = carry_scr[7:8, :]
    p2 = carry_scr[6:7, :]
    u1 = jnp.where(row == 0, p1, pltpu.roll(u, 1, axis=0))
    u2 = jnp.where(row == 0, p2, jnp.where(row == 1, p1, pltpu.roll(u, 2, axis=0)))
    conv = u2 * cw_ref[0:1, :] + u1 * cw_ref[1:2, :] + u * cw_ref[2:3, :]
    carry_scr[...] = u[tm - 8:, :]
    hist_ref[...] = u[tm - (CONV_W - 1):, :]
    _merge(cb_ref[...], conv, o_ref[...], ga_ref[...], gb_ref[...], wco_ref, wao_ref, m_ref)


def _mix_sample_body(cb_ref, cc_ref, cx_ref, o_ref, ga_ref, gb_ref, cw_ref, wco_ref, wao_ref,
                     p1_ref, p2_ref, m_ref, u_ref, *, n_new):
    u = cc_ref[...] * cx_ref[...]
    t = lax.broadcasted_iota(jnp.int32, u.shape, 0) % n_new
    u1 = jnp.where(t == 0, p1_ref[...], pltpu.roll(u, 1, axis=0))
    u2 = jnp.where(t < 2, p2_ref[...], pltpu.roll(u, 2, axis=0))
    conv = u2 * cw_ref[0:1, :] + u1 * cw_ref[1:2, :] + u * cw_ref[2:3, :]
    u_ref[...] = u
    _merge(cb_ref[...], conv, o_ref[...], ga_ref[...], gb_ref[...], wco_ref, wao_ref, m_ref)


def _mix_prompt(pconv, o, gates, conv_w, w_co, w_ao, batch, seq):
    T = pconv.shape[0]
    C = pconv.shape[1] // 3
    D = gates.shape[1] // 2
    Da = o.shape[1]
    tm = MIX_TM
    assert seq % tm == 0
    tps = seq // tm
    row = lambda c: (lambda i: (i, c))
    const = lambda i: (0, 0)
    m, hist = pl.pallas_call(
        functools.partial(_mix_prompt_body, tiles_per_seq=tps),
        grid=(T // tm,),
        in_specs=[
            pl.BlockSpec((tm, C), row(0)), pl.BlockSpec((tm, C), row(1)), pl.BlockSpec((tm, C), row(2)),
            pl.BlockSpec((tm, Da), row(0)),
            pl.BlockSpec((tm, D), row(0)), pl.BlockSpec((tm, D), row(1)),
            pl.BlockSpec((CONV_W, C), const),
            pl.BlockSpec((C, D), const), pl.BlockSpec((Da, D), const),
        ],
        out_specs=[pl.BlockSpec((tm, D), row(0)),
                   pl.BlockSpec((None, CONV_W - 1, C), lambda i: (i // tps, 0, 0))],
        out_shape=[jax.ShapeDtypeStruct((T, D), BF16),
                   jax.ShapeDtypeStruct((batch, CONV_W - 1, C), F32)],
        scratch_shapes=[pltpu.VMEM((8, C), F32)],
        compiler_params=pltpu.CompilerParams(
            dimension_semantics=("arbitrary",),
            vmem_limit_bytes=_vmem_limit(3 * _nbytes((tm, C), F32), _nbytes((tm, Da), BF16),
                                         2 * _nbytes((tm, D), F32), _nbytes((C, D), BF16),
                                         _nbytes((Da, D), BF16), _nbytes((tm, D), BF16))),
        name="mix_prompt",
    )(pconv, pconv, pconv, o, gates, gates, conv_w, w_co, w_ao)
    return m, hist


def _mix_sample(pconv, o, gates, conv_w, w_co, w_ao, state, n_streams, n_new):
    T = pconv.shape[0]
    C = pconv.shape[1] // 3
    D = gates.shape[1] // 2
    Da = o.shape[1]
    assert T == n_streams * n_new and n_new >= CONV_W - 1
    zeros = jnp.zeros((n_streams, n_new, C), F32)
    p1 = zeros.at[:, 0].set(state[:, 1]).reshape(T, C)
    p2 = zeros.at[:, 0].set(state[:, 0]).at[:, 1].set(state[:, 1]).reshape(T, C)
    col = lambda c: (lambda i: (0, c))
    const = lambda i: (0, 0)
    m, u = pl.pallas_call(
        functools.partial(_mix_sample_body, n_new=n_new),
        grid=(1,),
        in_specs=[
            pl.BlockSpec((T, C), col(0)), pl.BlockSpec((T, C), col(1)), pl.BlockSpec((T, C), col(2)),
            pl.BlockSpec((T, Da), const),
            pl.BlockSpec((T, D), col(0)), pl.BlockSpec((T, D), col(1)),
            pl.BlockSpec((CONV_W, C), const),
            pl.BlockSpec((C, D), const), pl.BlockSpec((Da, D), const),
            pl.BlockSpec((T, C), const), pl.BlockSpec((T, C), const),
        ],
        out_specs=[pl.BlockSpec((T, D), const), pl.BlockSpec((T, C), const)],
        out_shape=[jax.ShapeDtypeStruct((T, D), BF16), jax.ShapeDtypeStruct((T, C), F32)],
        compiler_params=pltpu.CompilerParams(
            dimension_semantics=("arbitrary",),
            vmem_limit_bytes=_vmem_limit(6 * _nbytes((T, C), F32), _nbytes((T, Da), BF16),
                                         2 * _nbytes((T, D), F32), _nbytes((C, D), BF16),
                                         _nbytes((Da, D), BF16), _nbytes((T, D), BF16))),
        name="mix_sample",
    )(pconv, pconv, pconv, o, gates, gates, conv_w, w_co, w_ao, p1, p2)
    new_hist = u.reshape(n_streams, n_new, C)[:, n_new - (CONV_W - 1):]
    return m, new_hist


def _layer(x, w, *, batch, seq, sample_state=None):
    C = w["conv_w"].shape[1]
    Da = N_HEADS * HEAD_DIM
    D = x.shape[1]
    x1, h = _ffn(x, w["norm_ffn1"], w["w_gu1"], w["w_dn1"], w["norm_mix"], emit_x=True, h_dtype=BF16)
    pconv = _matmul(h, w["w_in"], 0, 3 * C, F32)
    q = _matmul(h, w["w_in"], 3 * C, Da, BF16)
    k = _matmul(h, w["w_in"], 3 * C + Da, Da, F32)
    v = _matmul(h, w["w_in"], 3 * C + 2 * Da, Da, F32)
    gates = _matmul(h, w["w_in"], 3 * C + 3 * Da, 2 * D, F32)
    if sample_state is None:
        o = _attn_prompt(q, k, v, batch, seq)
        m, hist = _mix_prompt(pconv, o, gates, w["conv_w"], w["w_co"], w["w_ao"], batch, seq)
    else:
        conv_state, cache_k, cache_v, past_len = sample_state
        o = _attn_sample(q, k, v, cache_k.reshape(batch * past_len, Da),
                         cache_v.reshape(batch * past_len, Da), batch, seq, past_len)
        m, hist = _mix_sample(pconv, o, gates, w["conv_w"], w["w_co"], w["w_ao"], conv_state, batch, seq)
    x2 = _matmul(m, w["w_o"], 0, D, F32, residual=x1)
    return x2, hist, k, v


def kernel(x_prompt, x_sample, cache_k, cache_v, state_conv, norm_ffn1, ffn1_w_gate_up, ffn1_w_down,
           norm_mix, w_in, conv_w, w_conv_out, w_attn_out, w_o, norm_ffn2, ffn2_w_gate_up,
           ffn2_w_down, norm_final):
    depth = w_in.shape[0]
    B, T, D = x_prompt.shape
    S, n_new, _ = x_sample.shape
    past_len = cache_k.shape[2]
    xp = x_prompt.reshape(B * T, D)
    xs = x_sample.reshape(S * n_new, D)
    outs = [[] for _ in range(6)]
    for l in range(depth):
        w = {
            "norm_ffn1": norm_ffn1[l], "w_gu1": ffn1_w_gate_up[l].astype(BF16),
            "w_dn1": ffn1_w_down[l].astype(BF16), "norm_mix": norm_mix[l],
            "w_in": w_in[l].astype(BF16), "conv_w": conv_w[l],
            "w_co": w_conv_out[l].astype(BF16), "w_ao": w_attn_out[l].astype(BF16),
            "w_o": w_o[l].astype(BF16),
        }
        w2 = (norm_ffn2[l], ffn2_w_gate_up[l].astype(BF16), ffn2_w_down[l].astype(BF16), norm_final)
        xp2, c_p, k_p, v_p = _layer(xp, w, batch=B, seq=T)
        xs2, c_s, k_s, v_s = _layer(xs, w, batch=S, seq=n_new,
                                    sample_state=(state_conv[l], cache_k[l], cache_v[l], past_len))
        if l == depth - 1:
            xp = _ffn(xp2, *w2, emit_x=False, h_dtype=F32)
            xs = _ffn(xs2, *w2, emit_x=False, h_dtype=F32)
        else:
            xp = _ffn(xp2, *w2, emit_x=True, h_dtype=BF16)[0]
            xs = _ffn(xs2, *w2, emit_x=True, h_dtype=BF16)[0]
        for lst, val in zip(outs, (k_p.reshape(B, T, N_HEADS, HEAD_DIM), v_p.reshape(B, T, N_HEADS, HEAD_DIM),
                                   c_p, k_s.reshape(S, n_new, N_HEADS, HEAD_DIM),
                                   v_s.reshape(S, n_new, N_HEADS, HEAD_DIM), c_s)):
            lst.append(val)
    y_prompt = xp.reshape(B, T, D)
    y_sample = xs.reshape(S, n_new, D)
    return (y_prompt, y_sample) + tuple(jnp.stack(o, axis=0) for o in outs)
```

```python
import functools

import jax
import jax.numpy as jnp
from jax import lax
from jax.experimental import pallas as pl
from jax.experimental.pallas import tpu as pltpu

N_HEADS = 8
HEAD_DIM = 128
CONV_W = 3
EPS = 1e-6
FFN_SCALE = 0.5

F32 = jnp.float32
BF16 = jnp.bfloat16

V7X_VMEM_LIMIT_CAP = 56 * 1024 * 1024

FFN_TM = 512
FFN_TF = 512
MM_TM = 1024
MM_TN = 1024
MIX_TM = 256
ATT_BLK = 256


def _vmem_limit(*block_bytes, scratch=0):
    est = 2 * sum(block_bytes) + scratch
    return int(min(V7X_VMEM_LIMIT_CAP, max(2 * est, 16 * 1024 * 1024)))


def _nbytes(shape, dtype):
    n = 1
    for s in shape:
        n *= s
    return n * jnp.dtype(dtype).itemsize


def _rmsnorm(x, g):
    ms = jnp.mean(x * x, axis=-1, keepdims=True)
    return x * lax.rsqrt(ms + EPS) * g


def _dot(a, b):
    return jnp.dot(a, b, preferred_element_type=F32)


def _dot_nt(a, b):
    return lax.dot_general(a, b, (((1,), (1,)), ((), ())), preferred_element_type=F32)


def _ffn_body(x_ref, gin_ref, wg_ref, wu_ref, wd_ref, gout_ref, *rest, emit_x):
    if emit_x:
        xo_ref, ho_ref, h_scr, acc_scr = rest
    else:
        ho_ref, h_scr, acc_scr = rest
    j = pl.program_id(1)

    @pl.when(j == 0)
    def _():
        h_scr[...] = _rmsnorm(x_ref[...], gin_ref[...]).astype(BF16)
        acc_scr[...] = jnp.zeros_like(acc_scr)

    h = h_scr[...]
    g = _dot(h, wg_ref[...])
    u = _dot(h, wu_ref[...])
    act = (g * (1.0 / (1.0 + jnp.exp(-g))) * u).astype(BF16)
    acc_scr[...] += _dot(act, wd_ref[...])

    @pl.when(j == pl.num_programs(1) - 1)
    def _():
        xn = x_ref[...] + FFN_SCALE * acc_scr[...]
        if emit_x:
            xo_ref[...] = xn
        ho_ref[...] = _rmsnorm(xn, gout_ref[...]).astype(ho_ref.dtype)


def _ffn(x, g_in, w_gu, w_dn, g_out, *, emit_x, h_dtype):
    T, D = x.shape
    d_ff = w_dn.shape[0]
    tm = min(FFN_TM, T)
    tf = FFN_TF
    assert T % tm == 0 and d_ff % tf == 0
    n_f = d_ff // tf
    grid = (T // tm, n_f)
    in_specs = [
        pl.BlockSpec((tm, D), lambda i, j: (i, 0)),
        pl.BlockSpec((1, D), lambda i, j: (0, 0)),
        pl.BlockSpec((D, tf), lambda i, j: (0, j)),
        pl.BlockSpec((D, tf), lambda i, j: (0, j + n_f)),
        pl.BlockSpec((tf, D), lambda i, j: (j, 0)),
        pl.BlockSpec((1, D), lambda i, j: (0, 0)),
    ]
    out_shape = [jax.ShapeDtypeStruct((T, D), h_dtype)]
    out_specs = [pl.BlockSpec((tm, D), lambda i, j: (i, 0))]
    if emit_x:
        out_shape = [jax.ShapeDtypeStruct((T, D), F32)] + out_shape
        out_specs = [pl.BlockSpec((tm, D), lambda i, j: (i, 0))] + out_specs
    limit = _vmem_limit(
        _nbytes((tm, D), F32), 2 * _nbytes((D, tf), BF16), _nbytes((tf, D), BF16),
        _nbytes((tm, D), F32) * (2 if emit_x else 1),
        scratch=_nbytes((tm, D), BF16) + _nbytes((tm, D), F32))
    outs = pl.pallas_call(
        functools.partial(_ffn_body, emit_x=emit_x),
        grid=grid, in_specs=in_specs, out_specs=out_specs, out_shape=out_shape,
        scratch_shapes=[pltpu.VMEM((tm, D), BF16), pltpu.VMEM((tm, D), F32)],
        compiler_params=pltpu.CompilerParams(
            dimension_semantics=("parallel", "arbitrary"), vmem_limit_bytes=limit),
        name="ffn_mid" if emit_x else "ffn_final",
    )(x, g_in.reshape(1, D), w_gu, w_gu, w_dn, g_out.reshape(1, D))
    return outs if emit_x else outs[0]


def _mm_body(x_ref, w_ref, o_ref, *, out_scale):
    y = _dot(x_ref[...], w_ref[...])
    if out_scale is not None:
        y = y * out_scale
    o_ref[...] = y.astype(o_ref.dtype)


def _mm_res_body(x_ref, w_ref, r_ref, o_ref):
    o_ref[...] = r_ref[...] + _dot(x_ref[...], w_ref[...])


def _matmul(x, w, col0, n, out_dtype, residual=None, out_scale=None):
    T, K = x.shape
    tm = min(MM_TM, T)
    tn = min(MM_TN, n)
    assert T % tm == 0 and n % tn == 0 and col0 % tn == 0
    c0 = col0 // tn
    grid = (T // tm, n // tn)
    in_specs = [
        pl.BlockSpec((tm, K), lambda i, j: (i, 0)),
        pl.BlockSpec((K, tn), lambda i, j: (0, j + c0)),
    ]
    args = [x, w]
    body = functools.partial(_mm_body, out_scale=out_scale)
    blocks = [_nbytes((tm, K), x.dtype), _nbytes((K, tn), w.dtype), _nbytes((tm, tn), out_dtype)]
    if residual is not None:
        in_specs.append(pl.BlockSpec((tm, tn), lambda i, j: (i, j)))
        args.append(residual)
        body = _mm_res_body
        blocks.append(_nbytes((tm, tn), F32))
    return pl.pallas_call(
        body, grid=grid, in_specs=in_specs,
        out_specs=pl.BlockSpec((tm, tn), lambda i, j: (i, j)),
        out_shape=jax.ShapeDtypeStruct((T, n), out_dtype),
        compiler_params=pltpu.CompilerParams(
            dimension_semantics=("parallel", "arbitrary"),
            vmem_limit_bytes=_vmem_limit(*blocks)),
        name="proj",
    )(*args)


def _tri(bk):
    j = lax.broadcasted_iota(jnp.int32, (bk, bk), 0)
    s = lax.broadcasted_iota(jnp.int32, (bk, bk), 1)
    return (j >= s).astype(BF16)


LOG2E = 1.4426950408889634
Q_SCALE = HEAD_DIM ** -0.5 * LOG2E


def _softplus2(z2):
    sign_bit = jnp.uint32(0x80000000)
    neg_abs = lax.bitcast_convert_type(lax.bitcast_convert_type(z2, jnp.uint32) | sign_bit, F32)
    return jnp.maximum(z2, 0.0) + jnp.log(1.0 + jnp.exp2(neg_abs)) * LOG2E


def _cumsum_mxu(sp, tri):
    hi = sp.astype(BF16)
    lo = (sp - hi.astype(F32)).astype(BF16)
    if sp.shape[1] % 128 == 0:
        return _dot(jnp.concatenate([hi, lo], axis=1), jnp.concatenate([tri, tri], axis=0))
    return _dot(hi, tri) + _dot(lo, tri)


def _sb_weights(z2, tri, run, masked):
    sp = _softplus2(z2)
    if masked is not None:
        sp = masked(sp)
    tot = _cumsum_mxu(sp, tri)
    if run is not None:
        tot = tot + jnp.concatenate([run] * (z2.shape[1] // run.shape[1]), axis=1)
    a = jnp.exp2(z2 - tot)
    if masked is not None:
        a = masked(a)
    return a.astype(BF16), jnp.sum(sp, axis=1, keepdims=True)


def _attn_prompt_body(q_ref, k_ref, v_ref, tri_ref, o_ref, run_scr, acc_scr):
    blk = ATT_BLK
    nb = q_ref.shape[0] // blk
    tri = tri_ref[...]
    t = lax.broadcasted_iota(jnp.int32, (blk, blk), 0)
    s = lax.broadcasted_iota(jnp.int32, (blk, blk), 1)
    causal = s < t

    def diag(x):
        head = jnp.where(causal, x[:blk], 0.0)
        return head if x.shape[0] == blk else jnp.concatenate([head, x[blk:]], axis=0)

    run = None
    acc = None
    for c in range(nb - 1, -1, -1):
        keys = slice(c * blk, (c + 1) * blk)
        k_blk = k_ref[keys, :].astype(BF16)
        v_blk = v_ref[keys, :].astype(BF16)
        z2 = _dot_nt(q_ref[c * blk:, :], k_blk)
        if run is not None:
            run = jnp.concatenate([jnp.zeros((blk, HEAD_DIM), F32), run], axis=0)
        a, tot = _sb_weights(z2, tri, run, diag)
        pv = _dot(a, v_blk)
        if run is None:
            run, acc = jnp.broadcast_to(tot, (blk, HEAD_DIM)), pv
        else:
            run = run + tot
            acc = jnp.concatenate([pv[:blk], acc + pv[blk:]], axis=0)
    o_ref[...] = acc.astype(o_ref.dtype)


def _attn_prompt(q, k, v, batch, seq):
    blk = ATT_BLK
    assert seq % blk == 0
    hd = HEAD_DIM
    spec = pl.BlockSpec((seq, hd), lambda b, h: (b, h))
    return pl.pallas_call(
        _attn_prompt_body, grid=(batch, N_HEADS),
        in_specs=[spec, spec, spec, pl.BlockSpec((blk, blk), lambda b, h: (0, 0))],
        out_specs=spec,
        out_shape=jax.ShapeDtypeStruct(q.shape, BF16),
        scratch_shapes=[pltpu.VMEM((seq, hd), F32), pltpu.VMEM((seq, hd), F32)],
        compiler_params=pltpu.CompilerParams(
            dimension_semantics=("parallel", "parallel"),
            vmem_limit_bytes=_vmem_limit(2 * _nbytes((seq, hd), F32), 2 * _nbytes((seq, hd), BF16),
                                         scratch=2 * _nbytes((seq, hd), F32))),
        name="attn_prompt",
    )(q, k, v, _tri(blk))


def _attn_sample_body(q_ref, kn_ref, vn_ref, kc_ref, vc_ref, tri_ref, trin_ref, o_ref,
                      run_scr, acc_scr, *, n_new):
    step = pl.program_id(1)
    rows = N_HEADS * n_new
    bk = ATT_BLK
    head_cols = [slice(h * HEAD_DIM, (h + 1) * HEAD_DIM) for h in range(N_HEADS)]
    head_rows = [slice(h * n_new, (h + 1) * n_new) for h in range(N_HEADS)]

    def scores(load_k):
        parts = [_dot_nt(q_ref[:, head_cols[h]], load_k(h)) for h in range(N_HEADS)]
        return jnp.concatenate(parts, axis=0)

    def weighted_values(a, load_v):
        return jnp.concatenate([_dot(a[head_rows[h], :], load_v(h)) for h in range(N_HEADS)], axis=0)

    @pl.when(step == 0)
    def _():
        z = scores(lambda h: kn_ref[:, head_cols[h]].astype(BF16))
        t = lax.broadcasted_iota(jnp.int32, (rows, n_new), 0) % n_new
        s = lax.broadcasted_iota(jnp.int32, (rows, n_new), 1)
        a, tot = _sb_weights(z, trin_ref[...], None, lambda x: jnp.where(s < t, x, 0.0))
        acc_scr[...] = weighted_values(a, lambda h: vn_ref[:, head_cols[h]].astype(BF16))
        run_scr[...] = jnp.broadcast_to(tot, run_scr.shape)

    @pl.when(step > 0)
    def _():
        z = scores(lambda h: kc_ref[pl.ds(h, bk, stride=N_HEADS), :].astype(BF16))
        run = run_scr[...]
        a, tot = _sb_weights(z, tri_ref[...], run, None)
        acc_scr[...] += weighted_values(a, lambda h: vc_ref[pl.ds(h, bk, stride=N_HEADS), :].astype(BF16))
        run_scr[...] = run + tot

    @pl.when(step == pl.num_programs(1) - 1)
    def _():
        for h in range(N_HEADS):
            o_ref[:, head_cols[h]] = acc_scr[head_rows[h], :].astype(o_ref.dtype)


def _attn_sample(q, k_new, v_new, cache_k, cache_v, n_streams, n_new, past_len):
    bk = ATT_BLK
    assert past_len % bk == 0
    nkb = past_len // bk
    d_all = N_HEADS * HEAD_DIM
    rows = N_HEADS * n_new

    def cache_map(b, s):
        return (b * nkb + nkb - jnp.maximum(s, 1), 0)

    new_spec = pl.BlockSpec((n_new, d_all), lambda b, s: (b, 0))
    cache_spec = pl.BlockSpec((bk * N_HEADS, HEAD_DIM), cache_map)
    return pl.pallas_call(
        functools.partial(_attn_sample_body, n_new=n_new),
        grid=(n_streams, nkb + 1),
        in_specs=[new_spec, new_spec, new_spec, cache_spec, cache_spec,
                  pl.BlockSpec((bk, bk), lambda b, s: (0, 0)),
                  pl.BlockSpec((n_new, n_new), lambda b, s: (0, 0))],
        out_specs=new_spec,
        out_shape=jax.ShapeDtypeStruct(q.shape, BF16),
        scratch_shapes=[pltpu.VMEM((rows, HEAD_DIM), F32), pltpu.VMEM((rows, HEAD_DIM), F32)],
        compiler_params=pltpu.CompilerParams(
            dimension_semantics=("parallel", "arbitrary"),
            vmem_limit_bytes=_vmem_limit(2 * _nbytes((bk * N_HEADS, HEAD_DIM), F32))),
        name="attn_sample",
    )(q, k_new, v_new, cache_k, cache_v, _tri(bk), _tri(n_new))


def _sigmoid(x):
    return 1.0 / (1.0 + jnp.exp(-x))


def _merge(cb, conv, o, ga, gb, wco_ref, wao_ref, m_ref):
    y_a = _dot((cb * conv).astype(BF16), wco_ref[...])
    y_b = _dot(o, wao_ref[...])
    m_ref[...] = (_sigmoid(ga) * y_a + _sigmoid(gb) * y_b).astype(m_ref.dtype)


def _mix_prompt_body(cb_ref, cc_ref, cx_ref, o_ref, ga_ref, gb_ref, cw_ref, wco_ref, wao_ref,
                     m_ref, hist_ref, carry_scr, *, tiles_per_seq):
    i = pl.program_id(0)
    tm = cc_ref.shape[0]

    @pl.when(i % tiles_per_seq == 0)
    def _():
        carry_scr[...] = jnp.zeros_like(carry_scr)

    u = cc_ref[...] * cx_ref[...]
    row = lax.broadcasted_iota(jnp.int32, u.shape, 0)
    p1 = carry_scr[7:8, :]
    p2 = carry_scr[6:7, :]
    u1 = jnp.where(row == 0, p1, pltpu.roll(u, 1, axis=0))
    u2 = jnp.where(row == 0, p2, jnp.where(row == 1, p1, pltpu.roll(u, 2, axis=0)))
    conv = u2 * cw_ref[0:1, :] + u1 * cw_ref[1:2, :] + u * cw_ref[2:3, :]
    carry_scr[...] = u[tm - 8:, :]
    hist_ref[...] = u[tm - (CONV_W - 1):, :]
    _merge(cb_ref[...], conv, o_ref[...], ga_ref[...], gb_ref[...], wco_ref, wao_ref, m_ref)


def _mix_sample_body(cb_ref, cc_ref, cx_ref, o_ref, ga_ref, gb_ref, cw_ref, wco_ref, wao_ref,
                     p1_ref, p2_ref, m_ref, u_ref, *, n_new):
    u = cc_ref[...] * cx_ref[...]
    t = lax.broadcasted_iota(jnp.int32, u.shape, 0) % n_new
    u1 = jnp.where(t == 0, p1_ref[...], pltpu.roll(u, 1, axis=0))
    u2 = jnp.where(t < 2, p2_ref[...], pltpu.roll(u, 2, axis=0))
    conv = u2 * cw_ref[0:1, :] + u1 * cw_ref[1:2, :] + u * cw_ref[2:3, :]
    u_ref[...] = u
    _merge(cb_ref[...], conv, o_ref[...], ga_ref[...], gb_ref[...], wco_ref, wao_ref, m_ref)


def _mix_prompt(pconv, o, gates, conv_w, w_co, w_ao, batch, seq):
    T = pconv.shape[0]
    C = pconv.shape[1] // 3
    D = gates.shape[1] // 2
    Da = o.shape[1]
    tm = MIX_TM
    assert seq % tm == 0
    tps = seq // tm
    row = lambda c: (lambda i: (i, c))
    const = lambda i: (0, 0)
    m, hist = pl.pallas_call(
        functools.partial(_mix_prompt_body, tiles_per_seq=tps),
        grid=(T // tm,),
        in_specs=[
            pl.BlockSpec((tm, C), row(0)), pl.BlockSpec((tm, C), row(1)), pl.BlockSpec((tm, C), row(2)),
            pl.BlockSpec((tm, Da), row(0)),
            pl.BlockSpec((tm, D), row(0)), pl.BlockSpec((tm, D), row(1)),
            pl.BlockSpec((CONV_W, C), const),
            pl.BlockSpec((C, D), const), pl.BlockSpec((Da, D), const),
        ],
        out_specs=[pl.BlockSpec((tm, D), row(0)),
                   pl.BlockSpec((None, CONV_W - 1, C), lambda i: (i // tps, 0, 0))],
        out_shape=[jax.ShapeDtypeStruct((T, D), BF16),
                   jax.ShapeDtypeStruct((batch, CONV_W - 1, C), F32)],
        scratch_shapes=[pltpu.VMEM((8, C), F32)],
        compiler_params=pltpu.CompilerParams(
            dimension_semantics=("arbitrary",),
            vmem_limit_bytes=_vmem_limit(3 * _nbytes((tm, C), F32), _nbytes((tm, Da), BF16),
                                         2 * _nbytes((tm, D), F32), _nbytes((C, D), BF16),
                                         _nbytes((Da, D), BF16), _nbytes((tm, D), BF16))),
        name="mix_prompt",
    )(pconv, pconv, pconv, o, gates, gates, conv_w, w_co, w_ao)
    return m, hist


def _mix_sample(pconv, o, gates, conv_w, w_co, w_ao, state, n_streams, n_new):
    T = pconv.shape[0]
    C = pconv.shape[1] // 3
    D = gates.shape[1] // 2
    Da = o.shape[1]
    assert T == n_streams * n_new and n_new >= CONV_W - 1
    zeros = jnp.zeros((n_streams, n_new, C), F32)
    p1 = zeros.at[:, 0].set(state[:, 1]).reshape(T, C)
    p2 = zeros.at[:, 0].set(state[:, 0]).at[:, 1].set(state[:, 1]).reshape(T, C)
    col = lambda c: (lambda i: (0, c))
    const = lambda i: (0, 0)
    m, u = pl.pallas_call(
        functools.partial(_mix_sample_body, n_new=n_new),
        grid=(1,),
        in_specs=[
            pl.BlockSpec((T, C), col(0)), pl.BlockSpec((T, C), col(1)), pl.BlockSpec((T, C), col(2)),
            pl.BlockSpec((T, Da), const),
            pl.BlockSpec((T, D), col(0)), pl.BlockSpec((T, D), col(1)),
            pl.BlockSpec((CONV_W, C), const),
            pl.BlockSpec((C, D), const), pl.BlockSpec((Da, D), const),
            pl.BlockSpec((T, C), const), pl.BlockSpec((T, C), const),
        ],
        out_specs=[pl.BlockSpec((T, D), const), pl.BlockSpec((T, C), const)],
        out_shape=[jax.ShapeDtypeStruct((T, D), BF16), jax.ShapeDtypeStruct((T, C), F32)],
        compiler_params=pltpu.CompilerParams(
            dimension_semantics=("arbitrary",),
            vmem_limit_bytes=_vmem_limit(6 * _nbytes((T, C), F32), _nbytes((T, Da), BF16),
                                         2 * _nbytes((T, D), F32), _nbytes((C, D), BF16),
                                         _nbytes((Da, D), BF16), _nbytes((T, D), BF16))),
        name="mix_sample",
    )(pconv, pconv, pconv, o, gates, gates, conv_w, w_co, w_ao, p1, p2)
    new_hist = u.reshape(n_streams, n_new, C)[:, n_new - (CONV_W - 1):]
    return m, new_hist


def _layer(x, w, *, batch, seq, sample_state=None):
    C = w["conv_w"].shape[1]
    Da = N_HEADS * HEAD_DIM
    D = x.shape[1]
    x1, h = _ffn(x, w["norm_ffn1"], w["w_gu1"], w["w_dn1"], w["norm_mix"], emit_x=True, h_dtype=BF16)
    pconv = _matmul(h, w["w_in"], 0, 3 * C, F32)
    q = _matmul(h, w["w_in"], 3 * C, Da, BF16, out_scale=Q_SCALE)
    k = _matmul(h, w["w_in"], 3 * C + Da, Da, F32)
    v = _matmul(h, w["w_in"], 3 * C + 2 * Da, Da, F32)
    gates = _matmul(h, w["w_in"], 3 * C + 3 * Da, 2 * D, F32)
    if sample_state is None:
        o = _attn_prompt(q, k, v, batch, seq)
        m, hist = _mix_prompt(pconv, o, gates, w["conv_w"], w["w_co"], w["w_ao"], batch, seq)
    else:
        conv_state, cache_k, cache_v, past_len = sample_state
        o = _attn_sample(q, k, v, cache_k.reshape(batch * past_len * N_HEADS, HEAD_DIM),
                         cache_v.reshape(batch * past_len * N_HEADS, HEAD_DIM), batch, seq, past_len)
        m, hist = _mix_sample(pconv, o, gates, w["conv_w"], w["w_co"], w["w_ao"], conv_state, batch, seq)
    x2 = _matmul(m, w["w_o"], 0, D, F32, residual=x1)
    return x2, hist, k, v


def kernel(x_prompt, x_sample, cache_k, cache_v, state_conv, norm_ffn1, ffn1_w_gate_up, ffn1_w_down,
           norm_mix, w_in, conv_w, w_conv_out, w_attn_out, w_o, norm_ffn2, ffn2_w_gate_up,
           ffn2_w_down, norm_final):
    depth = w_in.shape[0]
    B, T, D = x_prompt.shape
    S, n_new, _ = x_sample.shape
    past_len = cache_k.shape[2]
    xp = x_prompt.reshape(B * T, D)
    xs = x_sample.reshape(S * n_new, D)
    outs = [[] for _ in range(6)]
    for l in range(depth):
        w = {
            "norm_ffn1": norm_ffn1[l], "w_gu1": ffn1_w_gate_up[l].astype(BF16),
            "w_dn1": ffn1_w_down[l].astype(BF16), "norm_mix": norm_mix[l],
            "w_in": w_in[l].astype(BF16), "conv_w": conv_w[l],
            "w_co": w_conv_out[l].astype(BF16), "w_ao": w_attn_out[l].astype(BF16),
            "w_o": w_o[l].astype(BF16),
        }
        w2 = (norm_ffn2[l], ffn2_w_gate_up[l].astype(BF16), ffn2_w_down[l].astype(BF16), norm_final)
        xp2, c_p, k_p, v_p = _layer(xp, w, batch=B, seq=T)
        xs2, c_s, k_s, v_s = _layer(xs, w, batch=S, seq=n_new,
                                    sample_state=(state_conv[l], cache_k[l], cache_v[l], past_len))
        if l == depth - 1:
            xp = _ffn(xp2, *w2, emit_x=False, h_dtype=F32)
            xs = _ffn(xs2, *w2, emit_x=False, h_dtype=F32)
        else:
            xp = _ffn(xp2, *w2, emit_x=True, h_dtype=BF16)[0]
            xs = _ffn(xs2, *w2, emit_x=True, h_dtype=BF16)[0]
        for lst, val in zip(outs, (k_p.reshape(B, T, N_HEADS, HEAD_DIM), v_p.reshape(B, T, N_HEADS, HEAD_DIM),
                                   c_p, k_s.reshape(S, n_new, N_HEADS, HEAD_DIM),
                                   v_s.reshape(S, n_new, N_HEADS, HEAD_DIM), c_s)):
            lst.append(val)
    y_prompt = xp.reshape(B, T, D)
    y_sample = xs.reshape(S, n_new, D)
    return (y_prompt, y_sample) + tuple(jnp.stack(o, axis=0) for o in outs)
```

```python
import functools

import jax
import jax.numpy as jnp
from jax import lax
from jax.experimental import pallas as pl
from jax.experimental.pallas import tpu as pltpu

N_HEADS = 8
HEAD_DIM = 128
CONV_W = 3
EPS = 1e-6
FFN_SCALE = 0.5

F32 = jnp.float32
BF16 = jnp.bfloat16

V7X_VMEM_LIMIT_CAP = 56 * 1024 * 1024

FFN_TM = 512
FFN_TF = 512
MM_TM = 1024
MM_TN = 1024
MIX_TM = 256
ATT_BLK = 256
SAMPLE_KEYS_PER_STEP = 1024


def _vmem_limit(*block_bytes, scratch=0):
    est = 2 * sum(block_bytes) + scratch
    return int(min(V7X_VMEM_LIMIT_CAP, max(2 * est, 16 * 1024 * 1024)))


def _nbytes(shape, dtype):
    n = 1
    for s in shape:
        n *= s
    return n * jnp.dtype(dtype).itemsize


def _rmsnorm(x, g):
    ms = jnp.mean(x * x, axis=-1, keepdims=True)
    return x * lax.rsqrt(ms + EPS) * g


def _dot(a, b):
    return jnp.dot(a, b, preferred_element_type=F32)


def _dot_nt(a, b):
    return lax.dot_general(a, b, (((1,), (1,)), ((), ())), preferred_element_type=F32)


def _ffn_body(x_ref, gin_ref, wg_ref, wu_ref, wd_ref, gout_ref, *rest, emit_x):
    if emit_x:
        xo_ref, ho_ref, h_scr, acc_scr = rest
    else:
        ho_ref, h_scr, acc_scr = rest
    j = pl.program_id(1)

    @pl.when(j == 0)
    def _():
        h_scr[...] = _rmsnorm(x_ref[...], gin_ref[...]).astype(BF16)
        acc_scr[...] = jnp.zeros_like(acc_scr)

    h = h_scr[...]
    g = _dot(h, wg_ref[...])
    u = _dot(h, wu_ref[...])
    act = (g * (1.0 / (1.0 + jnp.exp(-g))) * u).astype(BF16)
    acc_scr[...] += _dot(act, wd_ref[...])

    @pl.when(j == pl.num_programs(1) - 1)
    def _():
        xn = x_ref[...] + FFN_SCALE * acc_scr[...]
        if emit_x:
            xo_ref[...] = xn
        ho_ref[...] = _rmsnorm(xn, gout_ref[...]).astype(ho_ref.dtype)


def _ffn(x, g_in, w_gu, w_dn, g_out, *, emit_x, h_dtype):
    T, D = x.shape
    d_ff = w_dn.shape[0]
    tm = min(FFN_TM, T)
    tf = FFN_TF
    assert T % tm == 0 and d_ff % tf == 0
    n_f = d_ff // tf
    grid = (T // tm, n_f)
    in_specs = [
        pl.BlockSpec((tm, D), lambda i, j: (i, 0)),
        pl.BlockSpec((1, D), lambda i, j: (0, 0)),
        pl.BlockSpec((D, tf), lambda i, j: (0, j)),
        pl.BlockSpec((D, tf), lambda i, j: (0, j + n_f)),
        pl.BlockSpec((tf, D), lambda i, j: (j, 0)),
        pl.BlockSpec((1, D), lambda i, j: (0, 0)),
    ]
    out_shape = [jax.ShapeDtypeStruct((T, D), h_dtype)]
    out_specs = [pl.BlockSpec((tm, D), lambda i, j: (i, 0))]
    if emit_x:
        out_shape = [jax.ShapeDtypeStruct((T, D), F32)] + out_shape
        out_specs = [pl.BlockSpec((tm, D), lambda i, j: (i, 0))] + out_specs
    limit = _vmem_limit(
        _nbytes((tm, D), F32), 2 * _nbytes((D, tf), BF16), _nbytes((tf, D), BF16),
        _nbytes((tm, D), F32) * (2 if emit_x else 1),
        scratch=_nbytes((tm, D), BF16) + _nbytes((tm, D), F32))
    outs = pl.pallas_call(
        functools.partial(_ffn_body, emit_x=emit_x),
        grid=grid, in_specs=in_specs, out_specs=out_specs, out_shape=out_shape,
        scratch_shapes=[pltpu.VMEM((tm, D), BF16), pltpu.VMEM((tm, D), F32)],
        compiler_params=pltpu.CompilerParams(
            dimension_semantics=("parallel", "arbitrary"), vmem_limit_bytes=limit),
        name="ffn_mid" if emit_x else "ffn_final",
    )(x, g_in.reshape(1, D), w_gu, w_gu, w_dn, g_out.reshape(1, D))
    return outs if emit_x else outs[0]


def _mm_body(x_ref, w_ref, *rest, out_scale, has_residual):
    if has_residual:
        r_ref, o_ref, wb_scr = rest
    else:
        o_ref, wb_scr = rest

    @pl.when(pl.program_id(1) == 0)
    def _():
        wb_scr[...] = w_ref[...].astype(BF16)

    y = _dot(x_ref[...], wb_scr[...])
    if out_scale is not None:
        y = y * out_scale
    if has_residual:
        y = r_ref[...] + y
    o_ref[...] = y.astype(o_ref.dtype)


def _matmul(x, w, col0, n, out_dtype, residual=None, out_scale=None):
    T, K = x.shape
    tm = min(MM_TM, T)
    tn = min(MM_TN, n)
    assert T % tm == 0 and n % tn == 0 and col0 % tn == 0
    c0 = col0 // tn
    grid = (n // tn, T // tm)
    in_specs = [
        pl.BlockSpec((tm, K), lambda j, i: (i, 0)),
        pl.BlockSpec((K, tn), lambda j, i: (0, j + c0)),
    ]
    args = [x, w]
    blocks = [_nbytes((tm, K), x.dtype), _nbytes((K, tn), w.dtype), _nbytes((tm, tn), out_dtype)]
    if residual is not None:
        in_specs.append(pl.BlockSpec((tm, tn), lambda j, i: (i, j)))
        args.append(residual)
        blocks.append(_nbytes((tm, tn), F32))
    return pl.pallas_call(
        functools.partial(_mm_body, out_scale=out_scale, has_residual=residual is not None),
        grid=grid, in_specs=in_specs,
        out_specs=pl.BlockSpec((tm, tn), lambda j, i: (i, j)),
        out_shape=jax.ShapeDtypeStruct((T, n), out_dtype),
        scratch_shapes=[pltpu.VMEM((K, tn), BF16)],
        compiler_params=pltpu.CompilerParams(
            dimension_semantics=("arbitrary", "arbitrary"),
            vmem_limit_bytes=_vmem_limit(*blocks, scratch=_nbytes((K, tn), BF16))),
        name="proj",
    )(*args)


def _tri(bk):
    j = lax.broadcasted_iota(jnp.int32, (bk, bk), 0)
    s = lax.broadcasted_iota(jnp.int32, (bk, bk), 1)
    return (j >= s).astype(BF16)


LOG2E = 1.4426950408889634
Q_SCALE = HEAD_DIM ** -0.5 * LOG2E


def _softplus2(z2):
    return jnp.maximum(z2, 0.0) + jnp.log(1.0 + jnp.exp2(-jnp.abs(z2))) * LOG2E


def _cumsum_mxu(sp, tri):
    hi = sp.astype(BF16)
    lo = (sp - hi.astype(F32)).astype(BF16)
    if sp.shape[1] % 128 == 0:
        return _dot(jnp.concatenate([hi, lo], axis=1), jnp.concatenate([tri, tri], axis=0))
    return _dot(hi, tri) + _dot(lo, tri)


def _sb_weights(z2, tri, run, masked):
    sp = _softplus2(z2)
    if masked is not None:
        sp = masked(sp)
    tot = _cumsum_mxu(sp, tri)
    if run is not None:
        tot = tot + jnp.concatenate([run] * (z2.shape[1] // run.shape[1]), axis=1)
    a = jnp.exp2(z2 - tot)
    if masked is not None:
        a = masked(a)
    return a.astype(BF16), jnp.sum(sp, axis=1, keepdims=True)


def _attn_prompt_body(q_ref, k_ref, v_ref, tri_ref, o_ref, run_scr, acc_scr):
    blk = ATT_BLK
    nb = q_ref.shape[0] // blk
    tri = tri_ref[...]
    t = lax.broadcasted_iota(jnp.int32, (blk, blk), 0)
    s = lax.broadcasted_iota(jnp.int32, (blk, blk), 1)
    causal = s < t

    def diag(x):
        head = jnp.where(causal, x[:blk], 0.0)
        return head if x.shape[0] == blk else jnp.concatenate([head, x[blk:]], axis=0)

    run = None
    acc = None
    for c in range(nb - 1, -1, -1):
        keys = slice(c * blk, (c + 1) * blk)
        k_blk = k_ref[keys, :].astype(BF16)
        v_blk = v_ref[keys, :].astype(BF16)
        z2 = _dot_nt(q_ref[c * blk:, :], k_blk)
        if run is not None:
            run = jnp.concatenate([jnp.zeros((blk, HEAD_DIM), F32), run], axis=0)
        a, tot = _sb_weights(z2, tri, run, diag)
        pv = _dot(a, v_blk)
        if run is None:
            run, acc = jnp.broadcast_to(tot, (blk, HEAD_DIM)), pv
        else:
            run = run + tot
            acc = jnp.concatenate([pv[:blk], acc + pv[blk:]], axis=0)
    o_ref[...] = acc.astype(o_ref.dtype)


def _attn_prompt(q, k, v, batch, seq):
    blk = ATT_BLK
    assert seq % blk == 0
    hd = HEAD_DIM
    spec = pl.BlockSpec((seq, hd), lambda b, h: (b, h))
    return pl.pallas_call(
        _attn_prompt_body, grid=(batch, N_HEADS),
        in_specs=[spec, spec, spec, pl.BlockSpec((blk, blk), lambda b, h: (0, 0))],
        out_specs=spec,
        out_shape=jax.ShapeDtypeStruct(q.shape, BF16),
        scratch_shapes=[pltpu.VMEM((seq, hd), F32), pltpu.VMEM((seq, hd), F32)],
        compiler_params=pltpu.CompilerParams(
            dimension_semantics=("parallel", "parallel"),
            vmem_limit_bytes=_vmem_limit(2 * _nbytes((seq, hd), F32), 2 * _nbytes((seq, hd), BF16),
                                         scratch=2 * _nbytes((seq, hd), F32))),
        name="attn_prompt",
    )(q, k, v, _tri(blk))


def _attn_sample_body(q_ref, kn_ref, vn_ref, kc_ref, vc_ref, tri_ref, trin_ref, o_ref,
                      run_scr, acc_scr, *, n_new):
    step = pl.program_id(1)
    rows = N_HEADS * n_new
    bk = ATT_BLK
    head_cols = [slice(h * HEAD_DIM, (h + 1) * HEAD_DIM) for h in range(N_HEADS)]
    head_rows = [slice(h * n_new, (h + 1) * n_new) for h in range(N_HEADS)]

    def scores(load_k):
        parts = [_dot_nt(q_ref[:, head_cols[h]], load_k(h)) for h in range(N_HEADS)]
        return jnp.concatenate(parts, axis=0)

    def weighted_values(a, load_v):
        return jnp.concatenate([_dot(a[head_rows[h], :], load_v(h)) for h in range(N_HEADS)], axis=0)

    @pl.when(step == 0)
    def _():
        z = scores(lambda h: kn_ref[:, head_cols[h]].astype(BF16))
        t = lax.broadcasted_iota(jnp.int32, (rows, n_new), 0) % n_new
        s = lax.broadcasted_iota(jnp.int32, (rows, n_new), 1)
        a, tot = _sb_weights(z, trin_ref[...], None, lambda x: jnp.where(s < t, x, 0.0))
        acc_scr[...] = weighted_values(a, lambda h: vn_ref[:, head_cols[h]].astype(BF16))
        run_scr[...] = jnp.broadcast_to(tot, run_scr.shape)

    @pl.when(step > 0)
    def _():
        run = run_scr[...]
        acc = acc_scr[...]
        for sb in range(kc_ref.shape[0] // (bk * N_HEADS) - 1, -1, -1):
            def head_rows_of(ref, h):
                return ref[pl.ds(sb * bk * N_HEADS + h, bk, stride=N_HEADS), :].astype(BF16)
            z = scores(lambda h: head_rows_of(kc_ref, h))
            a, tot = _sb_weights(z, tri_ref[...], run, None)
            acc = acc + weighted_values(a, lambda h: head_rows_of(vc_ref, h))
            run = run + tot
        acc_scr[...] = acc
        run_scr[...] = run

    @pl.when(step == pl.num_programs(1) - 1)
    def _():
        for h in range(N_HEADS):
            o_ref[:, head_cols[h]] = acc_scr[head_rows[h], :].astype(o_ref.dtype)


def _attn_sample(q, k_new, v_new, cache_k, cache_v, n_streams, n_new, past_len):
    bk = ATT_BLK
    keys_per_step = min(SAMPLE_KEYS_PER_STEP, past_len)
    assert past_len % keys_per_step == 0 and keys_per_step % bk == 0
    nkb = past_len // keys_per_step
    d_all = N_HEADS * HEAD_DIM
    rows = N_HEADS * n_new

    def cache_map(b, s):
        return (b * nkb + nkb - jnp.maximum(s, 1), 0)

    new_spec = pl.BlockSpec((n_new, d_all), lambda b, s: (b, 0))
    cache_spec = pl.BlockSpec((keys_per_step * N_HEADS, HEAD_DIM), cache_map)
    return pl.pallas_call(
        functools.partial(_attn_sample_body, n_new=n_new),
        grid=(n_streams, nkb + 1),
        in_specs=[new_spec, new_spec, new_spec, cache_spec, cache_spec,
                  pl.BlockSpec((bk, bk), lambda b, s: (0, 0)),
                  pl.BlockSpec((n_new, n_new), lambda b, s: (0, 0))],
        out_specs=new_spec,
        out_shape=jax.ShapeDtypeStruct(q.shape, BF16),
        scratch_shapes=[pltpu.VMEM((rows, HEAD_DIM), F32), pltpu.VMEM((rows, HEAD_DIM), F32)],
        compiler_params=pltpu.CompilerParams(
            dimension_semantics=("parallel", "arbitrary"),
            vmem_limit_bytes=_vmem_limit(2 * _nbytes((keys_per_step * N_HEADS, HEAD_DIM), F32))),
        name="attn_sample",
    )(q, k_new, v_new, cache_k, cache_v, _tri(bk), _tri(n_new))


def _sigmoid(x):
    return 1.0 / (1.0 + jnp.exp(-x))


def _merge(cb, conv, o, ga, gb, wco_ref, wao_ref, m_ref):
    y_a = _dot((cb * conv).astype(BF16), wco_ref[...])
    y_b = _dot(o, wao_ref[...])
    m_ref[...] = (_sigmoid(ga) * y_a + _sigmoid(gb) * y_b).astype(m_ref.dtype)


def _mix_prompt_body(cb_ref, cc_ref, cx_ref, o_ref, ga_ref, gb_ref, cw_ref, wco_ref, wao_ref,
                     m_ref, hist_ref, carry_scr, *, tiles_per_seq):
    i = pl.program_id(0)
    tm = cc_ref.shape[0]

    @pl.when(i % tiles_per_seq == 0)
    def _():
        carry_scr[...] = jnp.zeros_like(carry_scr)

    u = cc_ref[...] * cx_ref[...]
    row = lax.broadcasted_iota(jnp.int32, u.shape, 0)
    p1 = carry_scr[7:8, :]
    p2 = carry_scr[6:7, :]
    u1 = jnp.where(row == 0, p1, pltpu.roll(u, 1, axis=0))
    u2 = jnp.where(row == 0, p2, jnp.where(row == 1, p1, pltpu.roll(u, 2, axis=0)))
    conv = u2 * cw_ref[0:1, :] + u1 * cw_ref[1:2, :] + u * cw_ref[2:3, :]
    carry_scr[...] = u[tm - 8:, :]
    hist_ref[...] = u[tm - (CONV_W - 1):, :]
    _merge(cb_ref[...], conv, o_ref[...], ga_ref[...], gb_ref[...], wco_ref, wao_ref, m_ref)


def _mix_sample_body(cb_ref, cc_ref, cx_ref, o_ref, ga_ref, gb_ref, cw_ref, wco_ref, wao_ref,
                     p1_ref, p2_ref, m_ref, u_ref, *, n_new):
    u = cc_ref[...] * cx_ref[...]
    t = lax.broadcasted_iota(jnp.int32, u.shape, 0) % n_new
    u1 = jnp.where(t == 0, p1_ref[...], pltpu.roll(u, 1, axis=0))
    u2 = jnp.where(t < 2, p2_ref[...], pltpu.roll(u, 2, axis=0))
    conv = u2 * cw_ref[0:1, :] + u1 * cw_ref[1:2, :] + u * cw_ref[2:3, :]
    u_ref[...] = u
    _merge(cb_ref[...], conv, o_ref[...], ga_ref[...], gb_ref[...], wco_ref, wao_ref, m_ref)


def _mix_prompt(pconv, o, gates, conv_w, w_co, w_ao, batch, seq):
    T = pconv.shape[0]
    C = pconv.shape[1] // 3
    D = gates.shape[1] // 2
    Da = o.shape[1]
    tm = MIX_TM
    assert seq % tm == 0
    tps = seq // tm
    row = lambda c: (lambda i: (i, c))
    const = lambda i: (0, 0)
    m, hist = pl.pallas_call(
        functools.partial(_mix_prompt_body, tiles_per_seq=tps),
        grid=(T // tm,),
        in_specs=[
            pl.BlockSpec((tm, C), row(0)), pl.BlockSpec((tm, C), row(1)), pl.BlockSpec((tm, C), row(2)),
            pl.BlockSpec((tm, Da), row(0)),
            pl.BlockSpec((tm, D), row(0)), pl.BlockSpec((tm, D), row(1)),
            pl.BlockSpec((CONV_W, C), const),
            pl.BlockSpec((C, D), const), pl.BlockSpec((Da, D), const),
        ],
        out_specs=[pl.BlockSpec((tm, D), row(0)),
                   pl.BlockSpec((None, CONV_W - 1, C), lambda i: (i // tps, 0, 0))],
        out_shape=[jax.ShapeDtypeStruct((T, D), BF16),
                   jax.ShapeDtypeStruct((batch, CONV_W - 1, C), F32)],
        scratch_shapes=[pltpu.VMEM((8, C), F32)],
        compiler_params=pltpu.CompilerParams(
            dimension_semantics=("arbitrary",),
            vmem_limit_bytes=_vmem_limit(3 * _nbytes((tm, C), F32), _nbytes((tm, Da), BF16),
                                         2 * _nbytes((tm, D), F32), _nbytes((C, D), BF16),
                                         _nbytes((Da, D), BF16), _nbytes((tm, D), BF16))),
        name="mix_prompt",
    )(pconv, pconv, pconv, o, gates, gates, conv_w, w_co, w_ao)
    return m, hist


def _mix_sample(pconv, o, gates, conv_w, w_co, w_ao, state, n_streams, n_new):
    T = pconv.shape[0]
    C = pconv.shape[1] // 3
    D = gates.shape[1] // 2
    Da = o.shape[1]
    assert T == n_streams * n_new and n_new >= CONV_W - 1
    zeros = jnp.zeros((n_streams, n_new, C), F32)
    p1 = zeros.at[:, 0].set(state[:, 1]).reshape(T, C)
    p2 = zeros.at[:, 0].set(state[:, 0]).at[:, 1].set(state[:, 1]).reshape(T, C)
    col = lambda c: (lambda i: (0, c))
    const = lambda i: (0, 0)
    m, u = pl.pallas_call(
        functools.partial(_mix_sample_body, n_new=n_new),
        grid=(1,),
        in_specs=[
            pl.BlockSpec((T, C), col(0)), pl.BlockSpec((T, C), col(1)), pl.BlockSpec((T, C), col(2)),
            pl.BlockSpec((T, Da), const),
            pl.BlockSpec((T, D), col(0)), pl.BlockSpec((T, D), col(1)),
            pl.BlockSpec((CONV_W, C), const),
            pl.BlockSpec((C, D), const), pl.BlockSpec((Da, D), const),
            pl.BlockSpec((T, C), const), pl.BlockSpec((T, C), const),
        ],
        out_specs=[pl.BlockSpec((T, D), const), pl.BlockSpec((T, C), const)],
        out_shape=[jax.ShapeDtypeStruct((T, D), BF16), jax.ShapeDtypeStruct((T, C), F32)],
        compiler_params=pltpu.CompilerParams(
            dimension_semantics=("arbitrary",),
            vmem_limit_bytes=_vmem_limit(6 * _nbytes((T, C), F32), _nbytes((T, Da), BF16),
                                         2 * _nbytes((T, D), F32), _nbytes((C, D), BF16),
                                         _nbytes((Da, D), BF16), _nbytes((T, D), BF16))),
        name="mix_sample",
    )(pconv, pconv, pconv, o, gates, gates, conv_w, w_co, w_ao, p1, p2)
    new_hist = u.reshape(n_streams, n_new, C)[:, n_new - (CONV_W - 1):]
    return m, new_hist


def _layer(x, w, *, batch, seq, sample_state=None):
    C = w["conv_w"].shape[1]
    Da = N_HEADS * HEAD_DIM
    D = x.shape[1]
    x1, h = _ffn(x, w["norm_ffn1"], w["w_gu1"], w["w_dn1"], w["norm_mix"], emit_x=True, h_dtype=BF16)
    pconv = _matmul(h, w["w_in"], 0, 3 * C, F32)
    q = _matmul(h, w["w_in"], 3 * C, Da, BF16, out_scale=Q_SCALE)
    k = _matmul(h, w["w_in"], 3 * C + Da, Da, F32)
    v = _matmul(h, w["w_in"], 3 * C + 2 * Da, Da, F32)
    gates = _matmul(h, w["w_in"], 3 * C + 3 * Da, 2 * D, F32)
    if sample_state is None:
        o = _attn_prompt(q, k, v, batch, seq)
        m, hist = _mix_prompt(pconv, o, gates, w["conv_w"], w["w_co"], w["w_ao"], batch, seq)
    else:
        conv_state, cache_k, cache_v, past_len = sample_state
        o = _attn_sample(q, k, v, cache_k.reshape(batch * past_len * N_HEADS, HEAD_DIM),
                         cache_v.reshape(batch * past_len * N_HEADS, HEAD_DIM), batch, seq, past_len)
        m, hist = _mix_sample(pconv, o, gates, w["conv_w"], w["w_co"], w["w_ao"], conv_state, batch, seq)
    x2 = _matmul(m, w["w_o"], 0, D, F32, residual=x1)
    return x2, hist, k, v


def kernel(x_prompt, x_sample, cache_k, cache_v, state_conv, norm_ffn1, ffn1_w_gate_up, ffn1_w_down,
           norm_mix, w_in, conv_w, w_conv_out, w_attn_out, w_o, norm_ffn2, ffn2_w_gate_up,
           ffn2_w_down, norm_final):
    depth = w_in.shape[0]
    B, T, D = x_prompt.shape
    S, n_new, _ = x_sample.shape
    past_len = cache_k.shape[2]
    xp = x_prompt.reshape(B * T, D)
    xs = x_sample.reshape(S * n_new, D)
    outs = [[] for _ in range(6)]
    for l in range(depth):
        w = {
            "norm_ffn1": norm_ffn1[l], "w_gu1": ffn1_w_gate_up[l].astype(BF16),
            "w_dn1": ffn1_w_down[l].astype(BF16), "norm_mix": norm_mix[l],
            "w_in": w_in[l], "conv_w": conv_w[l],
            "w_co": w_conv_out[l].astype(BF16), "w_ao": w_attn_out[l].astype(BF16),
            "w_o": w_o[l],
        }
        w2 = (norm_ffn2[l], ffn2_w_gate_up[l].astype(BF16), ffn2_w_down[l].astype(BF16), norm_final)
        xp2, c_p, k_p, v_p = _layer(xp, w, batch=B, seq=T)
        xs2, c_s, k_s, v_s = _layer(xs, w, batch=S, seq=n_new,
                                    sample_state=(state_conv[l], cache_k[l], cache_v[l], past_len))
        if l == depth - 1:
            xp = _ffn(xp2, *w2, emit_x=False, h_dtype=F32)
            xs = _ffn(xs2, *w2, emit_x=False, h_dtype=F32)
        else:
            xp = _ffn(xp2, *w2, emit_x=True, h_dtype=BF16)[0]
            xs = _ffn(xs2, *w2, emit_x=True, h_dtype=BF16)[0]
        for lst, val in zip(outs, (k_p.reshape(B, T, N_HEADS, HEAD_DIM), v_p.reshape(B, T, N_HEADS, HEAD_DIM),
                                   c_p, k_s.reshape(S, n_new, N_HEADS, HEAD_DIM),
                                   v_s.reshape(S, n_new, N_HEADS, HEAD_DIM), c_s)):
            lst.append(val)
    y_prompt = xp.reshape(B, T, D)
    y_sample = xs.reshape(S, n_new, D)
    return (y_prompt, y_sample) + tuple(jnp.stack(o, axis=0) for o in outs)
```

```python
import functools

import jax
import jax.numpy as jnp
from jax import lax
from jax.experimental import pallas as pl
from jax.experimental.pallas import tpu as pltpu

N_HEADS = 8
HEAD_DIM = 128
CONV_W = 3
EPS = 1e-6
FFN_SCALE = 0.5

F32 = jnp.float32
BF16 = jnp.bfloat16

V7X_VMEM_LIMIT_CAP = 56 * 1024 * 1024

FFN_TM = 512
FFN_TF = 512
MM_TM = 1024
MM_TN = 1024
MIX_TM = 256
ATT_BLK = 256
ATT_HEADS_PER_STEP = 2
SAMPLE_KEYS_PER_STEP = 1024


def _vmem_limit(*block_bytes, scratch=0):
    est = 2 * sum(block_bytes) + scratch
    return int(min(V7X_VMEM_LIMIT_CAP, max(2 * est, 16 * 1024 * 1024)))


def _nbytes(shape, dtype):
    n = 1
    for s in shape:
        n *= s
    return n * jnp.dtype(dtype).itemsize


def _rmsnorm(x, g):
    ms = jnp.mean(x * x, axis=-1, keepdims=True)
    return x * lax.rsqrt(ms + EPS) * g


def _dot(a, b):
    return jnp.dot(a, b, preferred_element_type=F32)


def _dot_nt(a, b):
    return lax.dot_general(a, b, (((1,), (1,)), ((), ())), preferred_element_type=F32)


def _ffn_body(x_ref, gin_ref, wg_ref, wu_ref, wd_ref, gout_ref, *rest, emit_x):
    if emit_x:
        xo_ref, ho_ref, h_scr, acc_scr = rest
    else:
        ho_ref, h_scr, acc_scr = rest
    j = pl.program_id(1)

    @pl.when(j == 0)
    def _():
        h_scr[...] = _rmsnorm(x_ref[...], gin_ref[...]).astype(BF16)
        acc_scr[...] = jnp.zeros_like(acc_scr)

    h = h_scr[...]
    g = _dot(h, wg_ref[...])
    u = _dot(h, wu_ref[...])
    act = (g * (1.0 / (1.0 + jnp.exp(-g))) * u).astype(BF16)
    acc_scr[...] += _dot(act, wd_ref[...])

    @pl.when(j == pl.num_programs(1) - 1)
    def _():
        xn = x_ref[...] + FFN_SCALE * acc_scr[...]
        if emit_x:
            xo_ref[...] = xn
        ho_ref[...] = _rmsnorm(xn, gout_ref[...]).astype(ho_ref.dtype)


def _ffn_weights(w_gate_up, w_down):
    D, two_ff = w_gate_up.shape
    assert two_ff % (2 * FFN_TF) == 0
    w_gu = w_gate_up.reshape(D, two_ff // FFN_TF, FFN_TF).transpose(1, 0, 2).astype(BF16)
    return w_gu, w_down.astype(BF16)


def _ffn(x, g_in, w_gu, w_dn, g_out, *, emit_x, h_dtype):
    T, D = x.shape
    d_ff = w_dn.shape[0]
    tm = min(FFN_TM, T)
    tf = FFN_TF
    assert T % tm == 0 and w_gu.shape == (2 * d_ff // tf, D, tf)
    n_f = d_ff // tf
    grid = (T // tm, n_f)
    in_specs = [
        pl.BlockSpec((tm, D), lambda i, j: (i, 0)),
        pl.BlockSpec((1, D), lambda i, j: (0, 0)),
        pl.BlockSpec((None, D, tf), lambda i, j: (j, 0, 0)),
        pl.BlockSpec((None, D, tf), lambda i, j: (j + n_f, 0, 0)),
        pl.BlockSpec((tf, D), lambda i, j: (j, 0)),
        pl.BlockSpec((1, D), lambda i, j: (0, 0)),
    ]
    out_shape = [jax.ShapeDtypeStruct((T, D), h_dtype)]
    out_specs = [pl.BlockSpec((tm, D), lambda i, j: (i, 0))]
    if emit_x:
        out_shape = [jax.ShapeDtypeStruct((T, D), F32)] + out_shape
        out_specs = [pl.BlockSpec((tm, D), lambda i, j: (i, 0))] + out_specs
    limit = _vmem_limit(
        _nbytes((tm, D), F32), 2 * _nbytes((D, tf), BF16), _nbytes((tf, D), BF16),
        _nbytes((tm, D), F32) * (2 if emit_x else 1),
        scratch=_nbytes((tm, D), BF16) + _nbytes((tm, D), F32))
    outs = pl.pallas_call(
        functools.partial(_ffn_body, emit_x=emit_x),
        grid=grid, in_specs=in_specs, out_specs=out_specs, out_shape=out_shape,
        scratch_shapes=[pltpu.VMEM((tm, D), BF16), pltpu.VMEM((tm, D), F32)],
        compiler_params=pltpu.CompilerParams(
            dimension_semantics=("parallel", "arbitrary"), vmem_limit_bytes=limit),
        name="ffn_mid" if emit_x else "ffn_final",
    )(x, g_in.reshape(1, D), w_gu, w_gu, w_dn, g_out.reshape(1, D))
    return outs if emit_x else outs[0]


def _mm_body(x_ref, w_ref, *rest, out_scale, has_residual):
    if has_residual:
        r_ref, o_ref, wb_scr = rest
    else:
        o_ref, wb_scr = rest

    @pl.when(pl.program_id(1) == 0)
    def _():
        wb_scr[...] = w_ref[...].astype(BF16)

    y = _dot(x_ref[...], wb_scr[...])
    if out_scale is not None:
        y = y * out_scale
    if has_residual:
        y = r_ref[...] + y
    o_ref[...] = y.astype(o_ref.dtype)


def _matmul(x, w, col0, n, out_dtype, residual=None, out_scale=None):
    T, K = x.shape
    tm = min(MM_TM, T)
    tn = min(MM_TN, n)
    assert T % tm == 0 and n % tn == 0 and col0 % tn == 0
    c0 = col0 // tn
    grid = (n // tn, T // tm)
    in_specs = [
        pl.BlockSpec((tm, K), lambda j, i: (i, 0)),
        pl.BlockSpec((K, tn), lambda j, i: (0, j + c0)),
    ]
    args = [x, w]
    blocks = [_nbytes((tm, K), x.dtype), _nbytes((K, tn), w.dtype), _nbytes((tm, tn), out_dtype)]
    if residual is not None:
        in_specs.append(pl.BlockSpec((tm, tn), lambda j, i: (i, j)))
        args.append(residual)
        blocks.append(_nbytes((tm, tn), F32))
    return pl.pallas_call(
        functools.partial(_mm_body, out_scale=out_scale, has_residual=residual is not None),
        grid=grid, in_specs=in_specs,
        out_specs=pl.BlockSpec((tm, tn), lambda j, i: (i, j)),
        out_shape=jax.ShapeDtypeStruct((T, n), out_dtype),
        scratch_shapes=[pltpu.VMEM((K, tn), BF16)],
        compiler_params=pltpu.CompilerParams(
            dimension_semantics=("arbitrary", "arbitrary"),
            vmem_limit_bytes=_vmem_limit(*blocks, scratch=_nbytes((K, tn), BF16))),
        name="proj",
    )(*args)


def _tri(bk):
    j = lax.broadcasted_iota(jnp.int32, (bk, bk), 0)
    s = lax.broadcasted_iota(jnp.int32, (bk, bk), 1)
    return (j >= s).astype(BF16)


LOG2E = 1.4426950408889634
Q_SCALE = HEAD_DIM ** -0.5 * LOG2E


def _softplus2(z2):
    return jnp.maximum(z2, 0.0) + jnp.log(1.0 + jnp.exp2(-jnp.abs(z2))) * LOG2E


def _cumsum_mxu(sp, tri):
    hi = sp.astype(BF16)
    lo = (sp - hi.astype(F32)).astype(BF16)
    if sp.shape[1] % 128 == 0:
        return _dot(jnp.concatenate([hi, lo], axis=1), jnp.concatenate([tri, tri], axis=0))
    return _dot(hi, tri) + _dot(lo, tri)


def _sb_weights(z2, tri, run, masked):
    sp = _softplus2(z2)
    if masked is not None:
        sp = masked(sp)
    tot = _cumsum_mxu(sp, tri)
    if run is not None:
        tot = tot + jnp.concatenate([run] * (z2.shape[1] // run.shape[1]), axis=1)
    a = jnp.exp2(z2 - tot)
    if masked is not None:
        a = masked(a)
    return a.astype(BF16), jnp.sum(sp, axis=1, keepdims=True)


def _attn_prompt_body(q_ref, k_ref, v_ref, tri_ref, o_ref):
    blk = ATT_BLK
    nb = q_ref.shape[0] // blk
    tri = tri_ref[...]
    t = lax.broadcasted_iota(jnp.int32, (blk, blk), 0)
    s = lax.broadcasted_iota(jnp.int32, (blk, blk), 1)
    causal = s < t

    def diag(x):
        head = jnp.where(causal, x[:blk], 0.0)
        return head if x.shape[0] == blk else jnp.concatenate([head, x[blk:]], axis=0)

    heads = [slice(h * HEAD_DIM, (h + 1) * HEAD_DIM) for h in range(ATT_HEADS_PER_STEP)]
    run = [None] * len(heads)
    acc = [None] * len(heads)
    for c in range(nb - 1, -1, -1):
        keys = slice(c * blk, (c + 1) * blk)
        for i, cols in enumerate(heads):
            k_blk = k_ref[keys, cols].astype(BF16)
            v_blk = v_ref[keys, cols].astype(BF16)
            z2 = _dot_nt(q_ref[c * blk:, cols], k_blk)
            if run[i] is not None:
                run[i] = jnp.concatenate([jnp.zeros((blk, HEAD_DIM), F32), run[i]], axis=0)
            a, tot = _sb_weights(z2, tri, run[i], diag)
            pv = _dot(a, v_blk)
            if run[i] is None:
                run[i], acc[i] = jnp.broadcast_to(tot, (blk, HEAD_DIM)), pv
            else:
                run[i] = run[i] + tot
                acc[i] = jnp.concatenate([pv[:blk], acc[i] + pv[blk:]], axis=0)
    for i, cols in enumerate(heads):
        o_ref[:, cols] = acc[i].astype(o_ref.dtype)


def _attn_prompt(q, k, v, batch, seq):
    blk = ATT_BLK
    assert seq % blk == 0 and N_HEADS % ATT_HEADS_PER_STEP == 0
    width = ATT_HEADS_PER_STEP * HEAD_DIM
    spec = pl.BlockSpec((seq, width), lambda b, h: (b, h))
    return pl.pallas_call(
        _attn_prompt_body, grid=(batch, N_HEADS // ATT_HEADS_PER_STEP),
        in_specs=[spec, spec, spec, pl.BlockSpec((blk, blk), lambda b, h: (0, 0))],
        out_specs=spec,
        out_shape=jax.ShapeDtypeStruct(q.shape, BF16),
        compiler_params=pltpu.CompilerParams(
            dimension_semantics=("parallel", "parallel"),
            vmem_limit_bytes=_vmem_limit(2 * _nbytes((seq, width), F32), 2 * _nbytes((seq, width), BF16),
                                         scratch=8 * ATT_HEADS_PER_STEP * _nbytes((seq, blk), F32))),
        name="attn_prompt",
    )(q, k, v, _tri(blk))


def _attn_sample_body(q_ref, kn_ref, vn_ref, kc_ref, vc_ref, tri_ref, trin_ref, o_ref,
                      run_scr, acc_scr, *, n_new):
    step = pl.program_id(1)
    rows = N_HEADS * n_new
    bk = ATT_BLK
    head_cols = [slice(h * HEAD_DIM, (h + 1) * HEAD_DIM) for h in range(N_HEADS)]
    head_rows = [slice(h * n_new, (h + 1) * n_new) for h in range(N_HEADS)]

    def scores(load_k):
        parts = [_dot_nt(q_ref[:, head_cols[h]], load_k(h)) for h in range(N_HEADS)]
        return jnp.concatenate(parts, axis=0)

    def weighted_values(a, load_v):
        return jnp.concatenate([_dot(a[head_rows[h], :], load_v(h)) for h in range(N_HEADS)], axis=0)

    @pl.when(step == 0)
    def _():
        z = scores(lambda h: kn_ref[:, head_cols[h]].astype(BF16))
        t = lax.broadcasted_iota(jnp.int32, (rows, n_new), 0) % n_new
        s = lax.broadcasted_iota(jnp.int32, (rows, n_new), 1)
        a, tot = _sb_weights(z, trin_ref[...], None, lambda x: jnp.where(s < t, x, 0.0))
        acc_scr[...] = weighted_values(a, lambda h: vn_ref[:, head_cols[h]].astype(BF16))
        run_scr[...] = jnp.broadcast_to(tot, run_scr.shape)

    @pl.when(step > 0)
    def _():
        run = run_scr[...]
        acc = acc_scr[...]
        for sb in range(kc_ref.shape[0] // (bk * N_HEADS) - 1, -1, -1):
            def head_rows_of(ref, h):
                return ref[pl.ds(sb * bk * N_HEADS + h, bk, stride=N_HEADS), :].astype(BF16)
            z = scores(lambda h: head_rows_of(kc_ref, h))
            a, tot = _sb_weights(z, tri_ref[...], run, None)
            acc = acc + weighted_values(a, lambda h: head_rows_of(vc_ref, h))
            run = run + tot
        acc_scr[...] = acc
        run_scr[...] = run

    @pl.when(step == pl.num_programs(1) - 1)
    def _():
        for h in range(N_HEADS):
            o_ref[:, head_cols[h]] = acc_scr[head_rows[h], :].astype(o_ref.dtype)


def _attn_sample(q, k_new, v_new, cache_k, cache_v, n_streams, n_new, past_len):
    bk = ATT_BLK
    keys_per_step = min(SAMPLE_KEYS_PER_STEP, past_len)
    assert past_len % keys_per_step == 0 and keys_per_step % bk == 0
    nkb = past_len // keys_per_step
    d_all = N_HEADS * HEAD_DIM
    rows = N_HEADS * n_new

    def cache_map(b, s):
        return (b * nkb + nkb - jnp.maximum(s, 1), 0)

    new_spec = pl.BlockSpec((n_new, d_all), lambda b, s: (b, 0))
    cache_spec = pl.BlockSpec((keys_per_step * N_HEADS, HEAD_DIM), cache_map)
    return pl.pallas_call(
        functools.partial(_attn_sample_body, n_new=n_new),
        grid=(n_streams, nkb + 1),
        in_specs=[new_spec, new_spec, new_spec, cache_spec, cache_spec,
                  pl.BlockSpec((bk, bk), lambda b, s: (0, 0)),
                  pl.BlockSpec((n_new, n_new), lambda b, s: (0, 0))],
        out_specs=new_spec,
        out_shape=jax.ShapeDtypeStruct(q.shape, BF16),
        scratch_shapes=[pltpu.VMEM((rows, HEAD_DIM), F32), pltpu.VMEM((rows, HEAD_DIM), F32)],
        compiler_params=pltpu.CompilerParams(
            dimension_semantics=("parallel", "arbitrary"),
            vmem_limit_bytes=_vmem_limit(2 * _nbytes((keys_per_step * N_HEADS, HEAD_DIM), F32))),
        name="attn_sample",
    )(q, k_new, v_new, cache_k, cache_v, _tri(bk), _tri(n_new))


def _sigmoid(x):
    return 1.0 / (1.0 + jnp.exp(-x))


def _merge(cb, conv, o, ga, gb, wco_ref, wao_ref, m_ref):
    y_a = _dot((cb * conv).astype(BF16), wco_ref[...])
    y_b = _dot(o, wao_ref[...])
    m_ref[...] = (_sigmoid(ga) * y_a + _sigmoid(gb) * y_b).astype(m_ref.dtype)


def _mix_prompt_body(cb_ref, cc_ref, cx_ref, o_ref, ga_ref, gb_ref, cw_ref, wco_ref, wao_ref,
                     m_ref, hist_ref, carry_scr, *, tiles_per_seq):
    i = pl.program_id(0)
    tm = cc_ref.shape[0]

    @pl.when(i % tiles_per_seq == 0)
    def _():
        carry_scr[...] = jnp.zeros_like(carry_scr)

    u = cc_ref[...] * cx_ref[...]
    row = lax.broadcasted_iota(jnp.int32, u.shape, 0)
    p1 = carry_scr[7:8, :]
    p2 = carry_scr[6:7, :]
    u1 = jnp.where(row == 0, p1, pltpu.roll(u, 1, axis=0))
    u2 = jnp.where(row == 0, p2, jnp.where(row == 1, p1, pltpu.roll(u, 2, axis=0)))
    conv = u2 * cw_ref[0:1, :] + u1 * cw_ref[1:2, :] + u * cw_ref[2:3, :]
    carry_scr[...] = u[tm - 8:, :]
    hist_ref[...] = u[tm - (CONV_W - 1):, :]
    _merge(cb_ref[...], conv, o_ref[...], ga_ref[...], gb_ref[...], wco_ref, wao_ref, m_ref)


def _mix_sample_body(cb_ref, cc_ref, cx_ref, o_ref, ga_ref, gb_ref, cw_ref, wco_ref, wao_ref,
                     p1_ref, p2_ref, m_ref, u_ref, *, n_new):
    u = cc_ref[...] * cx_ref[...]
    t = lax.broadcasted_iota(jnp.int32, u.shape, 0) % n_new
    u1 = jnp.where(t == 0, p1_ref[...], pltpu.roll(u, 1, axis=0))
    u2 = jnp.where(t < 2, p2_ref[...], pltpu.roll(u, 2, axis=0))
    conv = u2 * cw_ref[0:1, :] + u1 * cw_ref[1:2, :] + u * cw_ref[2:3, :]
    u_ref[...] = u
    _merge(cb_ref[...], conv, o_ref[...], ga_ref[...], gb_ref[...], wco_ref, wao_ref, m_ref)


def _mix_prompt(pconv, o, gates, conv_w, w_co, w_ao, batch, seq):
    T = pconv.shape[0]
    C = pconv.shape[1] // 3
    D = gates.shape[1] // 2
    Da = o.shape[1]
    tm = MIX_TM
    assert seq % tm == 0
    tps = seq // tm
    row = lambda c: (lambda i: (i, c))
    const = lambda i: (0, 0)
    m, hist = pl.pallas_call(
        functools.partial(_mix_prompt_body, tiles_per_seq=tps),
        grid=(T // tm,),
        in_specs=[
            pl.BlockSpec((tm, C), row(0)), pl.BlockSpec((tm, C), row(1)), pl.BlockSpec((tm, C), row(2)),
            pl.BlockSpec((tm, Da), row(0)),
            pl.BlockSpec((tm, D), row(0)), pl.BlockSpec((tm, D), row(1)),
            pl.BlockSpec((CONV_W, C), const),
            pl.BlockSpec((C, D), const), pl.BlockSpec((Da, D), const),
        ],
        out_specs=[pl.BlockSpec((tm, D), row(0)),
                   pl.BlockSpec((None, CONV_W - 1, C), lambda i: (i // tps, 0, 0))],
        out_shape=[jax.ShapeDtypeStruct((T, D), BF16),
                   jax.ShapeDtypeStruct((batch, CONV_W - 1, C), F32)],
        scratch_shapes=[pltpu.VMEM((8, C), F32)],
        compiler_params=pltpu.CompilerParams(
            dimension_semantics=("arbitrary",),
            vmem_limit_bytes=_vmem_limit(3 * _nbytes((tm, C), F32), _nbytes((tm, Da), BF16),
                                         2 * _nbytes((tm, D), F32), _nbytes((C, D), BF16),
                                         _nbytes((Da, D), BF16), _nbytes((tm, D), BF16))),
        name="mix_prompt",
    )(pconv, pconv, pconv, o, gates, gates, conv_w, w_co, w_ao)
    return m, hist


def _mix_sample(pconv, o, gates, conv_w, w_co, w_ao, state, n_streams, n_new):
    T = pconv.shape[0]
    C = pconv.shape[1] // 3
    D = gates.shape[1] // 2
    Da = o.shape[1]
    assert T == n_streams * n_new and n_new >= CONV_W - 1
    zeros = jnp.zeros((n_streams, n_new, C), F32)
    p1 = zeros.at[:, 0].set(state[:, 1]).reshape(T, C)
    p2 = zeros.at[:, 0].set(state[:, 0]).at[:, 1].set(state[:, 1]).reshape(T, C)
    col = lambda c: (lambda i: (0, c))
    const = lambda i: (0, 0)
    m, u = pl.pallas_call(
        functools.partial(_mix_sample_body, n_new=n_new),
        grid=(1,),
        in_specs=[
            pl.BlockSpec((T, C), col(0)), pl.BlockSpec((T, C), col(1)), pl.BlockSpec((T, C), col(2)),
            pl.BlockSpec((T, Da), const),
            pl.BlockSpec((T, D), col(0)), pl.BlockSpec((T, D), col(1)),
            pl.BlockSpec((CONV_W, C), const),
            pl.BlockSpec((C, D), const), pl.BlockSpec((Da, D), const),
            pl.BlockSpec((T, C), const), pl.BlockSpec((T, C), const),
        ],
        out_specs=[pl.BlockSpec((T, D), const), pl.BlockSpec((T, C), const)],
        out_shape=[jax.ShapeDtypeStruct((T, D), BF16), jax.ShapeDtypeStruct((T, C), F32)],
        compiler_params=pltpu.CompilerParams(
            dimension_semantics=("arbitrary",),
            vmem_limit_bytes=_vmem_limit(6 * _nbytes((T, C), F32), _nbytes((T, Da), BF16),
                                         2 * _nbytes((T, D), F32), _nbytes((C, D), BF16),
                                         _nbytes((Da, D), BF16), _nbytes((T, D), BF16))),
        name="mix_sample",
    )(pconv, pconv, pconv, o, gates, gates, conv_w, w_co, w_ao, p1, p2)
    new_hist = u.reshape(n_streams, n_new, C)[:, n_new - (CONV_W - 1):]
    return m, new_hist


def _layer(x, w, *, batch, seq, sample_state=None):
    C = w["conv_w"].shape[1]
    Da = N_HEADS * HEAD_DIM
    D = x.shape[1]
    x1, h = _ffn(x, w["norm_ffn1"], w["w_gu1"], w["w_dn1"], w["norm_mix"], emit_x=True, h_dtype=BF16)
    pconv = _matmul(h, w["w_in"], 0, 3 * C, F32)
    q = _matmul(h, w["w_in"], 3 * C, Da, BF16, out_scale=Q_SCALE)
    k = _matmul(h, w["w_in"], 3 * C + Da, Da, F32)
    v = _matmul(h, w["w_in"], 3 * C + 2 * Da, Da, F32)
    gates = _matmul(h, w["w_in"], 3 * C + 3 * Da, 2 * D, F32)
    if sample_state is None:
        o = _attn_prompt(q, k, v, batch, seq)
        m, hist = _mix_prompt(pconv, o, gates, w["conv_w"], w["w_co"], w["w_ao"], batch, seq)
    else:
        conv_state, cache_k, cache_v, past_len = sample_state
        o = _attn_sample(q, k, v, cache_k.reshape(batch * past_len * N_HEADS, HEAD_DIM),
                         cache_v.reshape(batch * past_len * N_HEADS, HEAD_DIM), batch, seq, past_len)
        m, hist = _mix_sample(pconv, o, gates, w["conv_w"], w["w_co"], w["w_ao"], conv_state, batch, seq)
    x2 = _matmul(m, w["w_o"], 0, D, F32, residual=x1)
    return x2, hist, k, v


def kernel(x_prompt, x_sample, cache_k, cache_v, state_conv, norm_ffn1, ffn1_w_gate_up, ffn1_w_down,
           norm_mix, w_in, conv_w, w_conv_out, w_attn_out, w_o, norm_ffn2, ffn2_w_gate_up,
           ffn2_w_down, norm_final):
    depth = w_in.shape[0]
    B, T, D = x_prompt.shape
    S, n_new, _ = x_sample.shape
    past_len = cache_k.shape[2]
    xp = x_prompt.reshape(B * T, D)
    xs = x_sample.reshape(S * n_new, D)
    outs = [[] for _ in range(6)]
    for l in range(depth):
        w_gu1, w_dn1 = _ffn_weights(ffn1_w_gate_up[l], ffn1_w_down[l])
        w_gu2, w_dn2 = _ffn_weights(ffn2_w_gate_up[l], ffn2_w_down[l])
        w = {
            "norm_ffn1": norm_ffn1[l], "w_gu1": w_gu1, "w_dn1": w_dn1, "norm_mix": norm_mix[l],
            "w_in": w_in[l], "conv_w": conv_w[l],
            "w_co": w_conv_out[l].astype(BF16), "w_ao": w_attn_out[l].astype(BF16),
            "w_o": w_o[l],
        }
        w2 = (norm_ffn2[l], w_gu2, w_dn2, norm_final)
        xp2, c_p, k_p, v_p = _layer(xp, w, batch=B, seq=T)
        xs2, c_s, k_s, v_s = _layer(xs, w, batch=S, seq=n_new,
                                    sample_state=(state_conv[l], cache_k[l], cache_v[l], past_len))
        if l == depth - 1:
            xp = _ffn(xp2, *w2, emit_x=False, h_dtype=F32)
            xs = _ffn(xs2, *w2, emit_x=False, h_dtype=F32)
        else:
            xp = _ffn(xp2, *w2, emit_x=True, h_dtype=BF16)[0]
            xs = _ffn(xs2, *w2, emit_x=True, h_dtype=BF16)[0]
        for lst, val in zip(outs, (k_p.reshape(B, T, N_HEADS, HEAD_DIM), v_p.reshape(B, T, N_HEADS, HEAD_DIM),
                                   c_p, k_s.reshape(S, n_new, N_HEADS, HEAD_DIM),
                                   v_s.reshape(S, n_new, N_HEADS, HEAD_DIM), c_s)):
            lst.append(val)
    y_prompt = xp.reshape(B, T, D)
    y_sample = xs.reshape(S, n_new, D)
    return (y_prompt, y_sample) + tuple(jnp.stack(o, axis=0) for o in outs)
```

```python
import functools

import jax
import jax.numpy as jnp
from jax import lax
from jax.experimental import pallas as pl
from jax.experimental.pallas import tpu as pltpu

N_HEADS = 8
HEAD_DIM = 128
CONV_W = 3
EPS = 1e-6
FFN_SCALE = 0.5

F32 = jnp.float32
BF16 = jnp.bfloat16

V7X_VMEM_LIMIT_CAP = 56 * 1024 * 1024

FFN_TM = 512
FFN_TF = 512
MM_TM = 1024
MM_TN = 1024
MM_ROW_CHUNK = 1024
MIX_TM = 512
ATT_BLK = 256
ATT_HEADS_PER_STEP = 2
SAMPLE_KEYS_PER_STEP = 1024


def _vmem_limit(*block_bytes, scratch=0):
    est = 2 * sum(block_bytes) + scratch
    return int(min(V7X_VMEM_LIMIT_CAP, max(2 * est, 16 * 1024 * 1024)))


def _nbytes(shape, dtype):
    n = 1
    for s in shape:
        n *= s
    return n * jnp.dtype(dtype).itemsize


def _rmsnorm(x, g):
    ms = jnp.mean(x * x, axis=-1, keepdims=True)
    return x * lax.rsqrt(ms + EPS) * g


def _dot(a, b):
    return jnp.dot(a, b, preferred_element_type=F32)


def _dot_nt(a, b):
    return lax.dot_general(a, b, (((1,), (1,)), ((), ())), preferred_element_type=F32)


def _ffn_body(x_ref, gin_ref, wg_ref, wu_ref, wd_ref, gout_ref, *rest, emit_x, row_chunk):
    if emit_x:
        xo_ref, ho_ref, h_scr, acc_scr = rest
    else:
        ho_ref, h_scr, acc_scr = rest
    j = pl.program_id(1)

    @pl.when(j == 0)
    def _():
        h_scr[...] = _rmsnorm(x_ref[...], gin_ref[...]).astype(BF16)
        acc_scr[...] = jnp.zeros_like(acc_scr)

    tm = h_scr.shape[0]
    chunk = min(row_chunk, tm)
    for r in range(0, tm, chunk):
        rows = slice(r, r + chunk)
        h = h_scr[rows, :]
        g = _dot(h, wg_ref[...])
        u = _dot(h, wu_ref[...])
        act = (g * (1.0 / (1.0 + jnp.exp(-g))) * u).astype(BF16)
        acc_scr[rows, :] += _dot(act, wd_ref[...])

    @pl.when(j == pl.num_programs(1) - 1)
    def _():
        xn = x_ref[...] + FFN_SCALE * acc_scr[...]
        if emit_x:
            xo_ref[...] = xn
        ho_ref[...] = _rmsnorm(xn, gout_ref[...]).astype(ho_ref.dtype)


def _ffn(x, g_in, w_gu, w_dn, g_out, *, emit_x, h_dtype):
    T, D = x.shape
    d_ff = w_dn.shape[0]
    tm = min(FFN_TM, T)
    tf = FFN_TF
    assert T % tm == 0 and d_ff % tf == 0
    n_f = d_ff // tf
    grid = (T // tm, n_f)
    in_specs = [
        pl.BlockSpec((tm, D), lambda i, j: (i, 0)),
        pl.BlockSpec((1, D), lambda i, j: (0, 0)),
        pl.BlockSpec((D, tf), lambda i, j: (0, j)),
        pl.BlockSpec((D, tf), lambda i, j: (0, j + n_f)),
        pl.BlockSpec((tf, D), lambda i, j: (j, 0)),
        pl.BlockSpec((1, D), lambda i, j: (0, 0)),
    ]
    out_shape = [jax.ShapeDtypeStruct((T, D), h_dtype)]
    out_specs = [pl.BlockSpec((tm, D), lambda i, j: (i, 0))]
    if emit_x:
        out_shape = [jax.ShapeDtypeStruct((T, D), F32)] + out_shape
        out_specs = [pl.BlockSpec((tm, D), lambda i, j: (i, 0))] + out_specs
    limit = _vmem_limit(
        _nbytes((tm, D), F32), 2 * _nbytes((D, tf), BF16), _nbytes((tf, D), BF16),
        _nbytes((tm, D), F32) * (2 if emit_x else 1),
        scratch=_nbytes((tm, D), BF16) + _nbytes((tm, D), F32))
    outs = pl.pallas_call(
        functools.partial(_ffn_body, emit_x=emit_x, row_chunk=256 if not emit_x else FFN_TM),
        grid=grid, in_specs=in_specs, out_specs=out_specs, out_shape=out_shape,
        scratch_shapes=[pltpu.VMEM((tm, D), BF16), pltpu.VMEM((tm, D), F32)],
        compiler_params=pltpu.CompilerParams(
            dimension_semantics=("parallel", "arbitrary"), vmem_limit_bytes=limit),
        name="ffn_mid" if emit_x else "ffn_final",
    )(x, g_in.reshape(1, D), w_gu, w_gu, w_dn, g_out.reshape(1, D))
    return outs if emit_x else outs[0]


def _mm_body(x_ref, w_ref, *rest, out_scale, has_residual, row_chunk):
    if has_residual:
        r_ref, o_ref, wb_scr = rest
    else:
        o_ref, wb_scr = rest

    @pl.when(pl.program_id(1) == 0)
    def _():
        wb_scr[...] = w_ref[...].astype(BF16)

    w = wb_scr[...]
    tm = x_ref.shape[0]
    chunk = min(row_chunk, tm)
    for r in range(0, tm, chunk):
        rows = slice(r, r + chunk)
        y = _dot(x_ref[rows, :], w)
        if out_scale is not None:
            y = y * out_scale
        if has_residual:
            y = r_ref[rows, :] + y
        o_ref[rows, :] = y.astype(o_ref.dtype)


def _matmul(x, w, col0, n, out_dtype, residual=None, out_scale=None, row_chunk=MM_ROW_CHUNK):
    T, K = x.shape
    tm = min(MM_TM, T)
    tn = min(MM_TN, n)
    assert T % tm == 0 and n % tn == 0 and col0 % tn == 0
    c0 = col0 // tn
    grid = (n // tn, T // tm)
    in_specs = [
        pl.BlockSpec((tm, K), lambda j, i: (i, 0)),
        pl.BlockSpec((K, tn), lambda j, i: (0, j + c0)),
    ]
    args = [x, w]
    blocks = [_nbytes((tm, K), x.dtype), _nbytes((K, tn), w.dtype), _nbytes((tm, tn), out_dtype)]
    if residual is not None:
        in_specs.append(pl.BlockSpec((tm, tn), lambda j, i: (i, j)))
        args.append(residual)
        blocks.append(_nbytes((tm, tn), F32))
    return pl.pallas_call(
        functools.partial(_mm_body, out_scale=out_scale, has_residual=residual is not None,
                          row_chunk=row_chunk),
        grid=grid, in_specs=in_specs,
        out_specs=pl.BlockSpec((tm, tn), lambda j, i: (i, j)),
        out_shape=jax.ShapeDtypeStruct((T, n), out_dtype),
        scratch_shapes=[pltpu.VMEM((K, tn), BF16)],
        compiler_params=pltpu.CompilerParams(
            dimension_semantics=("arbitrary", "arbitrary"),
            vmem_limit_bytes=_vmem_limit(*blocks, scratch=_nbytes((K, tn), BF16))),
        name="proj",
    )(*args)


def _tri(bk):
    j = lax.broadcasted_iota(jnp.int32, (bk, bk), 0)
    s = lax.broadcasted_iota(jnp.int32, (bk, bk), 1)
    return (j >= s).astype(BF16)


LOG2E = 1.4426950408889634
Q_SCALE = HEAD_DIM ** -0.5 * LOG2E


def _softplus2(z2):
    return jnp.maximum(z2, 0.0) + jnp.log(1.0 + jnp.exp2(-jnp.abs(z2))) * LOG2E


def _cumsum_mxu(sp, tri):
    hi = sp.astype(BF16)
    lo = (sp - hi.astype(F32)).astype(BF16)
    if sp.shape[1] % 128 == 0:
        return _dot(jnp.concatenate([hi, lo], axis=1), jnp.concatenate([tri, tri], axis=0))
    return _dot(hi, tri) + _dot(lo, tri)


def _sb_weights(z2, tri, run, masked):
    sp = _softplus2(z2)
    if masked is not None:
        sp = masked(sp)
    tot = _cumsum_mxu(sp, tri)
    if run is not None:
        tot = tot + jnp.concatenate([run] * (z2.shape[1] // run.shape[1]), axis=1)
    a = jnp.exp2(z2 - tot)
    if masked is not None:
        a = masked(a)
    return a.astype(BF16), jnp.sum(sp, axis=1, keepdims=True)


def _attn_prompt_body(q_ref, k_ref, v_ref, tri_ref, o_ref):
    blk = ATT_BLK
    nb = q_ref.shape[0] // blk
    tri = tri_ref[...]
    t = lax.broadcasted_iota(jnp.int32, (blk, blk), 0)
    s = lax.broadcasted_iota(jnp.int32, (blk, blk), 1)
    causal = s < t

    def diag(x):
        head = jnp.where(causal, x[:blk], 0.0)
        return head if x.shape[0] == blk else jnp.concatenate([head, x[blk:]], axis=0)

    heads = [slice(h * HEAD_DIM, (h + 1) * HEAD_DIM) for h in range(ATT_HEADS_PER_STEP)]
    run = [None] * len(heads)
    acc = [None] * len(heads)
    for c in range(nb - 1, -1, -1):
        keys = slice(c * blk, (c + 1) * blk)
        for i, cols in enumerate(heads):
            k_blk = k_ref[keys, cols].astype(BF16)
            v_blk = v_ref[keys, cols].astype(BF16)
            z2 = _dot_nt(q_ref[c * blk:, cols], k_blk)
            if run[i] is not None:
                run[i] = jnp.concatenate([jnp.zeros((blk, HEAD_DIM), F32), run[i]], axis=0)
            a, tot = _sb_weights(z2, tri, run[i], diag)
            pv = _dot(a, v_blk)
            if run[i] is None:
                run[i], acc[i] = jnp.broadcast_to(tot, (blk, HEAD_DIM)), pv
            else:
                run[i] = run[i] + tot
                acc[i] = jnp.concatenate([pv[:blk], acc[i] + pv[blk:]], axis=0)
    for i, cols in enumerate(heads):
        o_ref[:, cols] = acc[i].astype(o_ref.dtype)


def _attn_prompt(q, k, v, batch, seq):
    blk = ATT_BLK
    assert seq % blk == 0 and N_HEADS % ATT_HEADS_PER_STEP == 0
    width = ATT_HEADS_PER_STEP * HEAD_DIM
    spec = pl.BlockSpec((seq, width), lambda b, h: (b, h))
    return pl.pallas_call(
        _attn_prompt_body, grid=(batch, N_HEADS // ATT_HEADS_PER_STEP),
        in_specs=[spec, spec, spec, pl.BlockSpec((blk, blk), lambda b, h: (0, 0))],
        out_specs=spec,
        out_shape=jax.ShapeDtypeStruct(q.shape, BF16),
        compiler_params=pltpu.CompilerParams(
            dimension_semantics=("parallel", "parallel"),
            vmem_limit_bytes=_vmem_limit(2 * _nbytes((seq, width), F32), 2 * _nbytes((seq, width), BF16),
                                         scratch=8 * ATT_HEADS_PER_STEP * _nbytes((seq, blk), F32))),
        name="attn_prompt",
    )(q, k, v, _tri(blk))


def _attn_sample_body(q_ref, kn_ref, vn_ref, kc_ref, vc_ref, tri_ref, trin_ref, o_ref,
                      run_scr, acc_scr, *, n_new):
    step = pl.program_id(1)
    rows = N_HEADS * n_new
    bk = ATT_BLK
    head_cols = [slice(h * HEAD_DIM, (h + 1) * HEAD_DIM) for h in range(N_HEADS)]
    head_rows = [slice(h * n_new, (h + 1) * n_new) for h in range(N_HEADS)]

    def scores(load_k):
        parts = [_dot_nt(q_ref[:, head_cols[h]], load_k(h)) for h in range(N_HEADS)]
        return jnp.concatenate(parts, axis=0)

    def weighted_values(a, load_v):
        return jnp.concatenate([_dot(a[head_rows[h], :], load_v(h)) for h in range(N_HEADS)], axis=0)

    @pl.when(step == 0)
    def _():
        z = scores(lambda h: kn_ref[:, head_cols[h]].astype(BF16))
        t = lax.broadcasted_iota(jnp.int32, (rows, n_new), 0) % n_new
        s = lax.broadcasted_iota(jnp.int32, (rows, n_new), 1)
        a, tot = _sb_weights(z, trin_ref[...], None, lambda x: jnp.where(s < t, x, 0.0))
        acc_scr[...] = weighted_values(a, lambda h: vn_ref[:, head_cols[h]].astype(BF16))
        run_scr[...] = jnp.broadcast_to(tot, run_scr.shape)

    @pl.when(step > 0)
    def _():
        run = run_scr[...]
        acc = acc_scr[...]
        for sb in range(kc_ref.shape[0] // (bk * N_HEADS) - 1, -1, -1):
            def head_rows_of(ref, h):
                return ref[pl.ds(sb * bk * N_HEADS + h, bk, stride=N_HEADS), :].astype(BF16)
            z = scores(lambda h: head_rows_of(kc_ref, h))
            a, tot = _sb_weights(z, tri_ref[...], run, None)
            acc = acc + weighted_values(a, lambda h: head_rows_of(vc_ref, h))
            run = run + tot
        acc_scr[...] = acc
        run_scr[...] = run

    @pl.when(step == pl.num_programs(1) - 1)
    def _():
        for h in range(N_HEADS):
            o_ref[:, head_cols[h]] = acc_scr[head_rows[h], :].astype(o_ref.dtype)


def _attn_sample(q, k_new, v_new, cache_k, cache_v, n_streams, n_new, past_len):
    bk = ATT_BLK
    keys_per_step = min(SAMPLE_KEYS_PER_STEP, past_len)
    assert past_len % keys_per_step == 0 and keys_per_step % bk == 0
    nkb = past_len // keys_per_step
    d_all = N_HEADS * HEAD_DIM
    rows = N_HEADS * n_new

    def cache_map(b, s):
        return (b * nkb + nkb - jnp.maximum(s, 1), 0)

    new_spec = pl.BlockSpec((n_new, d_all), lambda b, s: (b, 0))
    cache_spec = pl.BlockSpec((keys_per_step * N_HEADS, HEAD_DIM), cache_map)
    return pl.pallas_call(
        functools.partial(_attn_sample_body, n_new=n_new),
        grid=(n_streams, nkb + 1),
        in_specs=[new_spec, new_spec, new_spec, cache_spec, cache_spec,
                  pl.BlockSpec((bk, bk), lambda b, s: (0, 0)),
                  pl.BlockSpec((n_new, n_new), lambda b, s: (0, 0))],
        out_specs=new_spec,
        out_shape=jax.ShapeDtypeStruct(q.shape, BF16),
        scratch_shapes=[pltpu.VMEM((rows, HEAD_DIM), F32), pltpu.VMEM((rows, HEAD_DIM), F32)],
        compiler_params=pltpu.CompilerParams(
            dimension_semantics=("parallel", "arbitrary"),
            vmem_limit_bytes=_vmem_limit(2 * _nbytes((keys_per_step * N_HEADS, HEAD_DIM), F32))),
        name="attn_sample",
    )(q, k_new, v_new, cache_k, cache_v, _tri(bk), _tri(n_new))


def _sigmoid(x):
    return 1.0 / (1.0 + jnp.exp(-x))


def _merge(cb, conv, o, ga, gb, wco_ref, wao_ref, m_ref):
    y_a = _dot((cb * conv).astype(BF16), wco_ref[...])
    y_b = _dot(o, wao_ref[...])
    m_ref[...] = (_sigmoid(ga) * y_a + _sigmoid(gb) * y_b).astype(m_ref.dtype)


def _mix_prompt_body(cb_ref, cc_ref, cx_ref, o_ref, ga_ref, gb_ref, cw_ref, wco_ref, wao_ref,
                     m_ref, hist_ref, carry_scr, *, tiles_per_seq):
    i = pl.program_id(0)
    tm = cc_ref.shape[0]

    @pl.when(i % tiles_per_seq == 0)
    def _():
        carry_scr[...] = jnp.zeros_like(carry_scr)

    u = cc_ref[...] * cx_ref[...]
    row = lax.broadcasted_iota(jnp.int32, u.shape, 0)
    p1 = carry_scr[7:8, :]
    p2 = carry_scr[6:7, :]
    u1 = jnp.where(row == 0, p1, pltpu.roll(u, 1, axis=0))
    u2 = jnp.where(row == 0, p2, jnp.where(row == 1, p1, pltpu.roll(u, 2, axis=0)))
    conv = u2 * cw_ref[0:1, :] + u1 * cw_ref[1:2, :] + u * cw_ref[2:3, :]
    carry_scr[...] = u[tm - 8:, :]
    hist_ref[...] = u[tm - (CONV_W - 1):, :]
    _merge(cb_ref[...], conv, o_ref[...], ga_ref[...], gb_ref[...], wco_ref, wao_ref, m_ref)


def _mix_sample_body(cb_ref, cc_ref, cx_ref, o_ref, ga_ref, gb_ref, cw_ref, wco_ref, wao_ref,
                     p1_ref, p2_ref, m_ref, u_ref, *, n_new):
    u = cc_ref[...] * cx_ref[...]
    t = lax.broadcasted_iota(jnp.int32, u.shape, 0) % n_new
    u1 = jnp.where(t == 0, p1_ref[...], pltpu.roll(u, 1, axis=0))
    u2 = jnp.where(t < 2, p2_ref[...], pltpu.roll(u, 2, axis=0))
    conv = u2 * cw_ref[0:1, :] + u1 * cw_ref[1:2, :] + u * cw_ref[2:3, :]
    u_ref[...] = u
    _merge(cb_ref[...], conv, o_ref[...], ga_ref[...], gb_ref[...], wco_ref, wao_ref, m_ref)


def _mix_prompt(pconv, o, gates, conv_w, w_co, w_ao, batch, seq):
    T = pconv.shape[0]
    C = pconv.shape[1] // 3
    D = gates.shape[1] // 2
    Da = o.shape[1]
    tm = MIX_TM
    assert seq % tm == 0
    tps = seq // tm
    row = lambda c: (lambda i: (i, c))
    const = lambda i: (0, 0)
    m, hist = pl.pallas_call(
        functools.partial(_mix_prompt_body, tiles_per_seq=tps),
        grid=(T // tm,),
        in_specs=[
            pl.BlockSpec((tm, C), row(0)), pl.BlockSpec((tm, C), row(1)), pl.BlockSpec((tm, C), row(2)),
            pl.BlockSpec((tm, Da), row(0)),
            pl.BlockSpec((tm, D), row(0)), pl.BlockSpec((tm, D), row(1)),
            pl.BlockSpec((CONV_W, C), const),
            pl.BlockSpec((C, D), const, pipeline_mode=pl.Buffered(1)),
            pl.BlockSpec((Da, D), const, pipeline_mode=pl.Buffered(1)),
        ],
        out_specs=[pl.BlockSpec((tm, D), row(0)),
                   pl.BlockSpec((None, CONV_W - 1, C), lambda i: (i // tps, 0, 0))],
        out_shape=[jax.ShapeDtypeStruct((T, D), BF16),
                   jax.ShapeDtypeStruct((batch, CONV_W - 1, C), F32)],
        scratch_shapes=[pltpu.VMEM((8, C), F32)],
        compiler_params=pltpu.CompilerParams(
            dimension_semantics=("arbitrary",),
            vmem_limit_bytes=_vmem_limit(3 * _nbytes((tm, C), F32), _nbytes((tm, Da), BF16),
                                         2 * _nbytes((tm, D), F32), _nbytes((C, D), BF16),
                                         _nbytes((Da, D), BF16), _nbytes((tm, D), BF16))),
        name="mix_prompt",
    )(pconv, pconv, pconv, o, gates, gates, conv_w, w_co, w_ao)
    return m, hist


def _mix_sample(pconv, o, gates, conv_w, w_co, w_ao, state, n_streams, n_new):
    T = pconv.shape[0]
    C = pconv.shape[1] // 3
    D = gates.shape[1] // 2
    Da = o.shape[1]
    assert T == n_streams * n_new and n_new >= CONV_W - 1
    zeros = jnp.zeros((n_streams, n_new, C), F32)
    p1 = zeros.at[:, 0].set(state[:, 1]).reshape(T, C)
    p2 = zeros.at[:, 0].set(state[:, 0]).at[:, 1].set(state[:, 1]).reshape(T, C)
    col = lambda c: (lambda i: (0, c))
    const = lambda i: (0, 0)
    m, u = pl.pallas_call(
        functools.partial(_mix_sample_body, n_new=n_new),
        grid=(1,),
        in_specs=[
            pl.BlockSpec((T, C), col(0)), pl.BlockSpec((T, C), col(1)), pl.BlockSpec((T, C), col(2)),
            pl.BlockSpec((T, Da), const),
            pl.BlockSpec((T, D), col(0)), pl.BlockSpec((T, D), col(1)),
            pl.BlockSpec((CONV_W, C), const),
            pl.BlockSpec((C, D), const), pl.BlockSpec((Da, D), const),
            pl.BlockSpec((T, C), const), pl.BlockSpec((T, C), const),
        ],
        out_specs=[pl.BlockSpec((T, D), const), pl.BlockSpec((T, C), const)],
        out_shape=[jax.ShapeDtypeStruct((T, D), BF16), jax.ShapeDtypeStruct((T, C), F32)],
        compiler_params=pltpu.CompilerParams(
            dimension_semantics=("arbitrary",),
            vmem_limit_bytes=_vmem_limit(6 * _nbytes((T, C), F32), _nbytes((T, Da), BF16),
                                         2 * _nbytes((T, D), F32), _nbytes((C, D), BF16),
                                         _nbytes((Da, D), BF16), _nbytes((T, D), BF16))),
        name="mix_sample",
    )(pconv, pconv, pconv, o, gates, gates, conv_w, w_co, w_ao, p1, p2)
    new_hist = u.reshape(n_streams, n_new, C)[:, n_new - (CONV_W - 1):]
    return m, new_hist


def _layer(x, w, *, batch, seq, sample_state=None):
    C = w["conv_w"].shape[1]
    Da = N_HEADS * HEAD_DIM
    D = x.shape[1]
    x1, h = _ffn(x, w["norm_ffn1"], w["w_gu1"], w["w_dn1"], w["norm_mix"], emit_x=True, h_dtype=BF16)
    pconv = _matmul(h, w["w_in"], 0, 3 * C, F32, row_chunk=512)
    q = _matmul(h, w["w_in"], 3 * C, Da, BF16, out_scale=Q_SCALE)
    k = _matmul(h, w["w_in"], 3 * C + Da, Da, F32, row_chunk=256)
    v = _matmul(h, w["w_in"], 3 * C + 2 * Da, Da, F32)
    gates = _matmul(h, w["w_in"], 3 * C + 3 * Da, 2 * D, F32)
    if sample_state is None:
        o = _attn_prompt(q, k, v, batch, seq)
        m, hist = _mix_prompt(pconv, o, gates, w["conv_w"], w["w_co"], w["w_ao"], batch, seq)
    else:
        conv_state, cache_k, cache_v, past_len = sample_state
        o = _attn_sample(q, k, v, cache_k.reshape(batch * past_len * N_HEADS, HEAD_DIM),
                         cache_v.reshape(batch * past_len * N_HEADS, HEAD_DIM), batch, seq, past_len)
        m, hist = _mix_sample(pconv, o, gates, w["conv_w"], w["w_co"], w["w_ao"], conv_state, batch, seq)
    x2 = _matmul(m, w["w_o"], 0, D, F32, residual=x1, row_chunk=512)
    return x2, hist, k, v


def kernel(x_prompt, x_sample, cache_k, cache_v, state_conv, norm_ffn1, ffn1_w_gate_up, ffn1_w_down,
           norm_mix, w_in, conv_w, w_conv_out, w_attn_out, w_o, norm_ffn2, ffn2_w_gate_up,
           ffn2_w_down, norm_final):
    depth = w_in.shape[0]
    B, T, D = x_prompt.shape
    S, n_new, _ = x_sample.shape
    past_len = cache_k.shape[2]
    xp = x_prompt.reshape(B * T, D)
    xs = x_sample.reshape(S * n_new, D)
    outs = [[] for _ in range(6)]
    for l in range(depth):
        w = {
            "norm_ffn1": norm_ffn1[l], "w_gu1": ffn1_w_gate_up[l].astype(BF16),
            "w_dn1": ffn1_w_down[l].astype(BF16), "norm_mix": norm_mix[l],
            "w_in": w_in[l], "conv_w": conv_w[l],
            "w_co": w_conv_out[l].astype(BF16), "w_ao": w_attn_out[l].astype(BF16),
            "w_o": w_o[l],
        }
        w2 = (norm_ffn2[l], ffn2_w_gate_up[l].astype(BF16), ffn2_w_down[l].astype(BF16), norm_final)
        xp2, c_p, k_p, v_p = _layer(xp, w, batch=B, seq=T)
        xs2, c_s, k_s, v_s = _layer(xs, w, batch=S, seq=n_new,
                                    sample_state=(state_conv[l], cache_k[l], cache_v[l], past_len))
        if l == depth - 1:
            xp = _ffn(xp2, *w2, emit_x=False, h_dtype=F32)
            xs = _ffn(xs2, *w2, emit_x=False, h_dtype=F32)
        else:
            xp = _ffn(xp2, *w2, emit_x=True, h_dtype=BF16)[0]
            xs = _ffn(xs2, *w2, emit_x=True, h_dtype=BF16)[0]
        for lst, val in zip(outs, (k_p.reshape(B, T, N_HEADS, HEAD_DIM), v_p.reshape(B, T, N_HEADS, HEAD_DIM),
                                   c_p, k_s.reshape(S, n_new, N_HEADS, HEAD_DIM),
                                   v_s.reshape(S, n_new, N_HEADS, HEAD_DIM), c_s)):
            lst.append(val)
    y_prompt = xp.reshape(B, T, D)
    y_sample = xs.reshape(S, n_new, D)
    return (y_prompt, y_sample) + tuple(jnp.stack(o, axis=0) for o in outs)
```

```python
import functools

import jax
import jax.numpy as jnp
from jax import lax
from jax.experimental import pallas as pl
from jax.experimental.pallas import tpu as pltpu

N_HEADS = 8
HEAD_DIM = 128
CONV_W = 3
EPS = 1e-6
FFN_SCALE = 0.5

F32 = jnp.float32
BF16 = jnp.bfloat16

V7X_VMEM_LIMIT_CAP = 60 * 1024 * 1024

FFN_TM = 512
FFN_TF = 512
FFN_TF_WIDE = 1408
MM_TM = 1024
MM_TN = 1024
MIX_TM = 512
ATT_BLK = 256
ATT_HEADS_PER_STEP = 2
SAMPLE_KEYS_PER_STEP = 1024


def _vmem_limit(*block_bytes, scratch=0):
    est = 2 * sum(block_bytes) + scratch
    return int(min(V7X_VMEM_LIMIT_CAP, max(2 * est, 16 * 1024 * 1024)))


def _nbytes(shape, dtype):
    n = 1
    for s in shape:
        n *= s
    return n * jnp.dtype(dtype).itemsize


def _rmsnorm(x, g):
    ms = jnp.mean(x * x, axis=-1, keepdims=True)
    return x * lax.rsqrt(ms + EPS) * g


def _dot(a, b):
    return jnp.dot(a, b, preferred_element_type=F32)


def _dot_nt(a, b):
    return lax.dot_general(a, b, (((1,), (1,)), ((), ())), preferred_element_type=F32)


def _ffn_body(x_ref, gin_ref, wg_ref, wu_ref, wd_ref, gout_ref, *rest, emit_x):
    if emit_x:
        xo_ref, ho_ref, h_scr, acc_scr = rest
    else:
        ho_ref, h_scr, acc_scr = rest
    j = pl.program_id(1)

    @pl.when(j == 0)
    def _():
        h_scr[...] = _rmsnorm(x_ref[...], gin_ref[...]).astype(BF16)
        acc_scr[...] = jnp.zeros_like(acc_scr)

    h = h_scr[...]
    g = _dot(h, wg_ref[...])
    u = _dot(h, wu_ref[...])
    act = (g * (1.0 / (1.0 + jnp.exp(-g))) * u).astype(BF16)
    acc_scr[...] += _dot(act, wd_ref[...])

    @pl.when(j == pl.num_programs(1) - 1)
    def _():
        xn = x_ref[...] + FFN_SCALE * acc_scr[...]
        if emit_x:
            xo_ref[...] = xn
        ho_ref[...] = _rmsnorm(xn, gout_ref[...]).astype(ho_ref.dtype)


def _ffn_tiles(n_tokens):
    if n_tokens <= FFN_TM:
        return n_tokens, FFN_TF_WIDE
    return FFN_TM, FFN_TF


def _ffn(x, g_in, w_gu, w_dn, g_out, *, emit_x, h_dtype):
    T, D = x.shape
    d_ff = w_dn.shape[0]
    tm, tf = _ffn_tiles(T)
    assert T % tm == 0 and d_ff % tf == 0 and (emit_x or h_dtype == F32)
    n_f = d_ff // tf
    grid = (T // tm, n_f)
    in_specs = [
        pl.BlockSpec((tm, D), lambda i, j: (i, 0)),
        pl.BlockSpec((1, D), lambda i, j: (0, 0)),
        pl.BlockSpec((D, tf), lambda i, j: (0, j)),
        pl.BlockSpec((D, tf), lambda i, j: (0, j + n_f)),
        pl.BlockSpec((tf, D), lambda i, j: (j, 0)),
        pl.BlockSpec((1, D), lambda i, j: (0, 0)),
    ]
    out_shape = [jax.ShapeDtypeStruct((T, D), h_dtype)]
    out_specs = [pl.BlockSpec((tm, D), lambda i, j: (i, 0))]
    if emit_x:
        out_shape = [jax.ShapeDtypeStruct((T, D), F32)] + out_shape
        out_specs = [pl.BlockSpec((tm, D), lambda i, j: (i, 0))] + out_specs
    limit = _vmem_limit(
        _nbytes((tm, D), F32), 2 * _nbytes((D, tf), BF16), _nbytes((tf, D), BF16),
        _nbytes((tm, D), h_dtype), _nbytes((tm, D), F32) if emit_x else 0,
        scratch=_nbytes((tm, D), BF16) + _nbytes((tm, D), F32))
    outs = pl.pallas_call(
        functools.partial(_ffn_body, emit_x=emit_x),
        grid=grid, in_specs=in_specs, out_specs=out_specs, out_shape=out_shape,
        scratch_shapes=[pltpu.VMEM((tm, D), BF16), pltpu.VMEM((tm, D), F32)],
        compiler_params=pltpu.CompilerParams(
            dimension_semantics=("parallel", "arbitrary"), vmem_limit_bytes=limit),
        name="ffn_mid" if emit_x else "ffn_final",
    )(x, g_in.reshape(1, D), w_gu, w_gu, w_dn, g_out.reshape(1, D))
    return outs if emit_x else outs[0]


def _sigmoid(x):
    return 1.0 / (1.0 + jnp.exp(-x))


def _mm_body(x_ref, w_ref, *rest, out_scale, has_residual, sigmoid):
    if has_residual:
        r_ref, o_ref, wb_scr = rest
    else:
        o_ref, wb_scr = rest

    @pl.when(pl.program_id(1) == 0)
    def _():
        wb_scr[...] = w_ref[...].astype(BF16)

    y = _dot(x_ref[...], wb_scr[...])
    if out_scale is not None:
        y = y * out_scale
    if sigmoid:
        y = _sigmoid(y)
    if has_residual:
        y = r_ref[...] + y
    o_ref[...] = y.astype(o_ref.dtype)


def _matmul(x, w, col0, n, out_dtype, residual=None, out_scale=None, sigmoid=False):
    T, K = x.shape
    tm = min(MM_TM, T)
    tn = min(MM_TN, n)
    assert T % tm == 0 and n % tn == 0 and col0 % tn == 0
    c0 = col0 // tn
    grid = (n // tn, T // tm)
    in_specs = [
        pl.BlockSpec((tm, K), lambda j, i: (i, 0)),
        pl.BlockSpec((K, tn), lambda j, i: (0, j + c0)),
    ]
    args = [x, w]
    blocks = [_nbytes((tm, K), x.dtype), _nbytes((K, tn), w.dtype), _nbytes((tm, tn), out_dtype)]
    if residual is not None:
        in_specs.append(pl.BlockSpec((tm, tn), lambda j, i: (i, j)))
        args.append(residual)
        blocks.append(_nbytes((tm, tn), F32))
    return pl.pallas_call(
        functools.partial(_mm_body, out_scale=out_scale, has_residual=residual is not None,
                          sigmoid=sigmoid),
        grid=grid, in_specs=in_specs,
        out_specs=pl.BlockSpec((tm, tn), lambda j, i: (i, j)),
        out_shape=jax.ShapeDtypeStruct((T, n), out_dtype),
        scratch_shapes=[pltpu.VMEM((K, tn), BF16)],
        compiler_params=pltpu.CompilerParams(
            dimension_semantics=("arbitrary", "arbitrary"),
            vmem_limit_bytes=_vmem_limit(*blocks, scratch=_nbytes((K, tn), BF16))),
        name="proj",
    )(*args)


def _tri(bk):
    j = lax.broadcasted_iota(jnp.int32, (bk, bk), 0)
    s = lax.broadcasted_iota(jnp.int32, (bk, bk), 1)
    return (j >= s).astype(BF16)


LOG2E = 1.4426950408889634
Q_SCALE = HEAD_DIM ** -0.5 * LOG2E


def _softplus2(z2):
    return jnp.maximum(z2, 0.0) + jnp.log(1.0 + jnp.exp2(-jnp.abs(z2))) * LOG2E


def _cumsum_mxu(sp, tri):
    hi = sp.astype(BF16)
    lo = (sp - hi.astype(F32)).astype(BF16)
    if sp.shape[1] % 128 == 0:
        return _dot(jnp.concatenate([hi, lo], axis=1), jnp.concatenate([tri, tri], axis=0))
    return _dot(hi, tri) + _dot(lo, tri)


def _sb_weights(z2, tri, run, masked):
    sp = _softplus2(z2)
    if masked is not None:
        sp = masked(sp)
    tot = _cumsum_mxu(sp, tri)
    if run is not None:
        tot = tot + jnp.concatenate([run] * (z2.shape[1] // run.shape[1]), axis=1)
    a = jnp.exp2(z2 - tot)
    if masked is not None:
        a = masked(a)
    return a.astype(BF16), jnp.sum(sp, axis=1, keepdims=True)


def _attn_prompt_body(q_ref, k_ref, v_ref, tri_ref, o_ref):
    blk = ATT_BLK
    nb = q_ref.shape[0] // blk
    tri = tri_ref[...]
    t = lax.broadcasted_iota(jnp.int32, (blk, blk), 0)
    s = lax.broadcasted_iota(jnp.int32, (blk, blk), 1)
    causal = s < t

    def diag(x):
        head = jnp.where(causal, x[:blk], 0.0)
        return head if x.shape[0] == blk else jnp.concatenate([head, x[blk:]], axis=0)

    heads = [slice(h * HEAD_DIM, (h + 1) * HEAD_DIM) for h in range(ATT_HEADS_PER_STEP)]
    run = [None] * len(heads)
    acc = [None] * len(heads)
    for c in range(nb - 1, -1, -1):
        keys = slice(c * blk, (c + 1) * blk)
        for i, cols in enumerate(heads):
            k_blk = k_ref[keys, cols].astype(BF16)
            v_blk = v_ref[keys, cols].astype(BF16)
            z2 = _dot_nt(q_ref[c * blk:, cols], k_blk)
            if run[i] is not None:
                run[i] = jnp.concatenate([jnp.zeros((blk, HEAD_DIM), F32), run[i]], axis=0)
            a, tot = _sb_weights(z2, tri, run[i], diag)
            pv = _dot(a, v_blk)
            if run[i] is None:
                run[i], acc[i] = jnp.broadcast_to(tot, (blk, HEAD_DIM)), pv
            else:
                run[i] = run[i] + tot
                acc[i] = jnp.concatenate([pv[:blk], acc[i] + pv[blk:]], axis=0)
    for i, cols in enumerate(heads):
        o_ref[:, cols] = acc[i].astype(o_ref.dtype)


def _attn_prompt(q, k, v, batch, seq):
    blk = ATT_BLK
    assert seq % blk == 0 and N_HEADS % ATT_HEADS_PER_STEP == 0
    width = ATT_HEADS_PER_STEP * HEAD_DIM
    spec = pl.BlockSpec((seq, width), lambda b, h: (b, h))
    return pl.pallas_call(
        _attn_prompt_body, grid=(batch, N_HEADS // ATT_HEADS_PER_STEP),
        in_specs=[spec, spec, spec, pl.BlockSpec((blk, blk), lambda b, h: (0, 0))],
        out_specs=spec,
        out_shape=jax.ShapeDtypeStruct(q.shape, BF16),
        compiler_params=pltpu.CompilerParams(
            dimension_semantics=("parallel", "parallel"),
            vmem_limit_bytes=_vmem_limit(2 * _nbytes((seq, width), F32), 2 * _nbytes((seq, width), BF16),
                                         scratch=8 * ATT_HEADS_PER_STEP * _nbytes((seq, blk), F32))),
        name="attn_prompt",
    )(q, k, v, _tri(blk))


def _attn_sample_body(q_ref, kn_ref, vn_ref, kc_ref, vc_ref, tri_ref, trin_ref, o_ref,
                      run_scr, acc_scr, *, n_new):
    step = pl.program_id(1)
    rows = N_HEADS * n_new
    bk = ATT_BLK
    head_cols = [slice(h * HEAD_DIM, (h + 1) * HEAD_DIM) for h in range(N_HEADS)]
    head_rows = [slice(h * n_new, (h + 1) * n_new) for h in range(N_HEADS)]

    def scores(load_k):
        parts = [_dot_nt(q_ref[:, head_cols[h]], load_k(h)) for h in range(N_HEADS)]
        return jnp.concatenate(parts, axis=0)

    def weighted_values(a, load_v):
        return jnp.concatenate([_dot(a[head_rows[h], :], load_v(h)) for h in range(N_HEADS)], axis=0)

    @pl.when(step == 0)
    def _():
        z = scores(lambda h: kn_ref[:, head_cols[h]].astype(BF16))
        t = lax.broadcasted_iota(jnp.int32, (rows, n_new), 0) % n_new
        s = lax.broadcasted_iota(jnp.int32, (rows, n_new), 1)
        a, tot = _sb_weights(z, trin_ref[...], None, lambda x: jnp.where(s < t, x, 0.0))
        acc_scr[...] = weighted_values(a, lambda h: vn_ref[:, head_cols[h]].astype(BF16))
        run_scr[...] = jnp.broadcast_to(tot, run_scr.shape)

    @pl.when(step > 0)
    def _():
        run = run_scr[...]
        acc = acc_scr[...]
        for sb in range(kc_ref.shape[0] // (bk * N_HEADS) - 1, -1, -1):
            def head_rows_of(ref, h):
                return ref[pl.ds(sb * bk * N_HEADS + h, bk, stride=N_HEADS), :].astype(BF16)
            z = scores(lambda h: head_rows_of(kc_ref, h))
            a, tot = _sb_weights(z, tri_ref[...], run, None)
            acc = acc + weighted_values(a, lambda h: head_rows_of(vc_ref, h))
            run = run + tot
        acc_scr[...] = acc
        run_scr[...] = run

    @pl.when(step == pl.num_programs(1) - 1)
    def _():
        for h in range(N_HEADS):
            o_ref[:, head_cols[h]] = acc_scr[head_rows[h], :].astype(o_ref.dtype)


def _attn_sample(q, k_new, v_new, cache_k, cache_v, n_streams, n_new, past_len):
    bk = ATT_BLK
    keys_per_step = min(SAMPLE_KEYS_PER_STEP, past_len)
    assert past_len % keys_per_step == 0 and keys_per_step % bk == 0
    nkb = past_len // keys_per_step
    d_all = N_HEADS * HEAD_DIM
    rows = N_HEADS * n_new

    def cache_map(b, s):
        return (b * nkb + nkb - jnp.maximum(s, 1), 0)

    new_spec = pl.BlockSpec((n_new, d_all), lambda b, s: (b, 0))
    cache_spec = pl.BlockSpec((keys_per_step * N_HEADS, HEAD_DIM), cache_map)
    return pl.pallas_call(
        functools.partial(_attn_sample_body, n_new=n_new),
        grid=(n_streams, nkb + 1),
        in_specs=[new_spec, new_spec, new_spec, cache_spec, cache_spec,
                  pl.BlockSpec((bk, bk), lambda b, s: (0, 0)),
                  pl.BlockSpec((n_new, n_new), lambda b, s: (0, 0))],
        out_specs=new_spec,
        out_shape=jax.ShapeDtypeStruct(q.shape, BF16),
        scratch_shapes=[pltpu.VMEM((rows, HEAD_DIM), F32), pltpu.VMEM((rows, HEAD_DIM), F32)],
        compiler_params=pltpu.CompilerParams(
            dimension_semantics=("parallel", "arbitrary"),
            vmem_limit_bytes=_vmem_limit(2 * _nbytes((keys_per_step * N_HEADS, HEAD_DIM), F32))),
        name="attn_sample",
    )(q, k_new, v_new, cache_k, cache_v, _tri(bk), _tri(n_new))


def _merge(cb, conv, o, sga, sgb, wco_ref, wao_ref, m_ref):
    y_a = _dot((cb * conv).astype(BF16), wco_ref[...])
    y_b = _dot(o, wao_ref[...])
    m_ref[...] = (sga * y_a + sgb * y_b).astype(m_ref.dtype)


def _mix_prompt_body(cb_ref, cc_ref, cx_ref, o_ref, ga_ref, gb_ref, cw_ref, wco_ref, wao_ref,
                     m_ref, hist_ref, carry_scr, *, tiles_per_seq):
    i = pl.program_id(0)
    tm = cc_ref.shape[0]

    @pl.when(i % tiles_per_seq == 0)
    def _():
        carry_scr[...] = jnp.zeros_like(carry_scr)

    u = cc_ref[...] * cx_ref[...]
    row = lax.broadcasted_iota(jnp.int32, u.shape, 0)
    p1 = carry_scr[7:8, :]
    p2 = carry_scr[6:7, :]
    u1 = jnp.where(row == 0, p1, pltpu.roll(u, 1, axis=0))
    u2 = jnp.where(row == 0, p2, jnp.where(row == 1, p1, pltpu.roll(u, 2, axis=0)))
    conv = u2 * cw_ref[0:1, :] + u1 * cw_ref[1:2, :] + u * cw_ref[2:3, :]
    carry_scr[...] = u[tm - 8:, :]
    hist_ref[...] = u[tm - (CONV_W - 1):, :]
    _merge(cb_ref[...], conv, o_ref[...], ga_ref[...], gb_ref[...], wco_ref, wao_ref, m_ref)


def _mix_sample_body(cb_ref, cc_ref, cx_ref, o_ref, ga_ref, gb_ref, cw_ref, wco_ref, wao_ref,
                     p1_ref, p2_ref, m_ref, u_ref, *, n_new):
    u = cc_ref[...] * cx_ref[...]
    t = lax.broadcasted_iota(jnp.int32, u.shape, 0) % n_new
    u1 = jnp.where(t == 0, p1_ref[...], pltpu.roll(u, 1, axis=0))
    u2 = jnp.where(t < 2, p2_ref[...], pltpu.roll(u, 2, axis=0))
    conv = u2 * cw_ref[0:1, :] + u1 * cw_ref[1:2, :] + u * cw_ref[2:3, :]
    u_ref[...] = u
    _merge(cb_ref[...], conv, o_ref[...], ga_ref[...], gb_ref[...], wco_ref, wao_ref, m_ref)


def _mix_prompt(pconv, o, gates, conv_w, w_co, w_ao, batch, seq):
    T = pconv.shape[0]
    C = pconv.shape[1] // 3
    D = gates.shape[1] // 2
    Da = o.shape[1]
    tm = MIX_TM
    assert seq % tm == 0
    tps = seq // tm
    row = lambda c: (lambda i: (i, c))
    const = lambda i: (0, 0)
    m, hist = pl.pallas_call(
        functools.partial(_mix_prompt_body, tiles_per_seq=tps),
        grid=(T // tm,),
        in_specs=[
            pl.BlockSpec((tm, C), row(0)), pl.BlockSpec((tm, C), row(1)), pl.BlockSpec((tm, C), row(2)),
            pl.BlockSpec((tm, Da), row(0)),
            pl.BlockSpec((tm, D), row(0)), pl.BlockSpec((tm, D), row(1)),
            pl.BlockSpec((CONV_W, C), const),
            pl.BlockSpec((C, D), const, pipeline_mode=pl.Buffered(1)),
            pl.BlockSpec((Da, D), const, pipeline_mode=pl.Buffered(1)),
        ],
        out_specs=[pl.BlockSpec((tm, D), row(0)),
                   pl.BlockSpec((None, CONV_W - 1, C), lambda i: (i // tps, 0, 0))],
        out_shape=[jax.ShapeDtypeStruct((T, D), BF16),
                   jax.ShapeDtypeStruct((batch, CONV_W - 1, C), F32)],
        scratch_shapes=[pltpu.VMEM((8, C), F32)],
        compiler_params=pltpu.CompilerParams(
            dimension_semantics=("arbitrary",),
            vmem_limit_bytes=_vmem_limit(3 * _nbytes((tm, C), F32), _nbytes((tm, Da), BF16),
                                         2 * _nbytes((tm, D), gates.dtype), _nbytes((C, D), BF16),
                                         _nbytes((Da, D), BF16), _nbytes((tm, D), BF16))),
        name="mix_prompt",
    )(pconv, pconv, pconv, o, gates, gates, conv_w, w_co, w_ao)
    return m, hist


def _mix_sample(pconv, o, gates, conv_w, w_co, w_ao, state, n_streams, n_new):
    T = pconv.shape[0]
    C = pconv.shape[1] // 3
    D = gates.shape[1] // 2
    Da = o.shape[1]
    assert T == n_streams * n_new and n_new >= CONV_W - 1
    zeros = jnp.zeros((n_streams, n_new, C), F32)
    p1 = zeros.at[:, 0].set(state[:, 1]).reshape(T, C)
    p2 = zeros.at[:, 0].set(state[:, 0]).at[:, 1].set(state[:, 1]).reshape(T, C)
    col = lambda c: (lambda i: (0, c))
    const = lambda i: (0, 0)
    m, u = pl.pallas_call(
        functools.partial(_mix_sample_body, n_new=n_new),
        grid=(1,),
        in_specs=[
            pl.BlockSpec((T, C), col(0)), pl.BlockSpec((T, C), col(1)), pl.BlockSpec((T, C), col(2)),
            pl.BlockSpec((T, Da), const),
            pl.BlockSpec((T, D), col(0)), pl.BlockSpec((T, D), col(1)),
            pl.BlockSpec((CONV_W, C), const),
            pl.BlockSpec((C, D), const), pl.BlockSpec((Da, D), const),
            pl.BlockSpec((T, C), const), pl.BlockSpec((T, C), const),
        ],
        out_specs=[pl.BlockSpec((T, D), const), pl.BlockSpec((T, C), const)],
        out_shape=[jax.ShapeDtypeStruct((T, D), BF16), jax.ShapeDtypeStruct((T, C), F32)],
        compiler_params=pltpu.CompilerParams(
            dimension_semantics=("arbitrary",),
            vmem_limit_bytes=_vmem_limit(6 * _nbytes((T, C), F32), _nbytes((T, Da), BF16),
                                         2 * _nbytes((T, D), gates.dtype), _nbytes((C, D), BF16),
                                         _nbytes((Da, D), BF16), _nbytes((T, D), BF16))),
        name="mix_sample",
    )(pconv, pconv, pconv, o, gates, gates, conv_w, w_co, w_ao, p1, p2)
    new_hist = u.reshape(n_streams, n_new, C)[:, n_new - (CONV_W - 1):]
    return m, new_hist


def _layer(x, w, *, batch, seq, sample_state=None):
    C = w["conv_w"].shape[1]
    Da = N_HEADS * HEAD_DIM
    D = x.shape[1]
    x1, h = _ffn(x, w["norm_ffn1"], w["w_gu1"], w["w_dn1"], w["norm_mix"], emit_x=True, h_dtype=BF16)
    pconv = _matmul(h, w["w_in"], 0, 3 * C, F32)
    q = _matmul(h, w["w_in"], 3 * C, Da, BF16, out_scale=Q_SCALE)
    k = _matmul(h, w["w_in"], 3 * C + Da, Da, F32)
    v = _matmul(h, w["w_in"], 3 * C + 2 * Da, Da, F32)
    gates = _matmul(h, w["w_in"], 3 * C + 3 * Da, 2 * D, BF16, sigmoid=True)
    if sample_state is None:
        o = _attn_prompt(q, k, v, batch, seq)
        m, hist = _mix_prompt(pconv, o, gates, w["conv_w"], w["w_co"], w["w_ao"], batch, seq)
    else:
        conv_state, cache_k, cache_v, past_len = sample_state
        o = _attn_sample(q, k, v, cache_k.reshape(batch * past_len * N_HEADS, HEAD_DIM),
                         cache_v.reshape(batch * past_len * N_HEADS, HEAD_DIM), batch, seq, past_len)
        m, hist = _mix_sample(pconv, o, gates, w["conv_w"], w["w_co"], w["w_ao"], conv_state, batch, seq)
    x2 = _matmul(m, w["w_o"], 0, D, F32, residual=x1)
    return x2, hist, k, v


def kernel(x_prompt, x_sample, cache_k, cache_v, state_conv, norm_ffn1, ffn1_w_gate_up, ffn1_w_down,
           norm_mix, w_in, conv_w, w_conv_out, w_attn_out, w_o, norm_ffn2, ffn2_w_gate_up,
           ffn2_w_down, norm_final):
    depth = w_in.shape[0]
    B, T, D = x_prompt.shape
    S, n_new, _ = x_sample.shape
    past_len = cache_k.shape[2]
    xp = x_prompt.reshape(B * T, D)
    xs = x_sample.reshape(S * n_new, D)
    outs = [[] for _ in range(6)]
    for l in range(depth):
        w = {
            "norm_ffn1": norm_ffn1[l], "w_gu1": ffn1_w_gate_up[l].astype(BF16),
            "w_dn1": ffn1_w_down[l].astype(BF16), "norm_mix": norm_mix[l],
            "w_in": w_in[l], "conv_w": conv_w[l],
            "w_co": w_conv_out[l].astype(BF16), "w_ao": w_attn_out[l].astype(BF16),
            "w_o": w_o[l],
        }
        w2 = (norm_ffn2[l], ffn2_w_gate_up[l].astype(BF16), ffn2_w_down[l].astype(BF16), norm_final)
        xp2, c_p, k_p, v_p = _layer(xp, w, batch=B, seq=T)
        xs2, c_s, k_s, v_s = _layer(xs, w, batch=S, seq=n_new,
                                    sample_state=(state_conv[l], cache_k[l], cache_v[l], past_len))
        if l == depth - 1:
            xp = _ffn(xp2, *w2, emit_x=False, h_dtype=F32)
            xs = _ffn(xs2, *w2, emit_x=False, h_dtype=F32)
        else:
            xp = _ffn(xp2, *w2, emit_x=True, h_dtype=BF16)[0]
            xs = _ffn(xs2, *w2, emit_x=True, h_dtype=BF16)[0]
        for lst, val in zip(outs, (k_p.reshape(B, T, N_HEADS, HEAD_DIM), v_p.reshape(B, T, N_HEADS, HEAD_DIM),
                                   c_p, k_s.reshape(S, n_new, N_HEADS, HEAD_DIM),
                                   v_s.reshape(S, n_new, N_HEADS, HEAD_DIM), c_s)):
            lst.append(val)
    y_prompt = xp.reshape(B, T, D)
    y_sample = xs.reshape(S, n_new, D)
    return (y_prompt, y_sample) + tuple(jnp.stack(o, axis=0) for o in outs)
```

```python
import functools

import jax
import jax.numpy as jnp
from jax import lax
from jax.experimental import pallas as pl
from jax.experimental.pallas import tpu as pltpu

N_HEADS = 8
HEAD_DIM = 128
CONV_W = 3
EPS = 1e-6
FFN_SCALE = 0.5

F32 = jnp.float32
BF16 = jnp.bfloat16

V7X_VMEM_LIMIT_CAP = 60 * 1024 * 1024

FFN_TM = 512
FFN_TF = 512
MM_TM = 1024
MM_TN = 1024
MIX_TM = 512
ATT_BLK = 256
ATT_HEADS_PER_STEP = 2
SAMPLE_KEYS_PER_STEP = 1024


def _vmem_limit(*block_bytes, scratch=0):
    est = 2 * sum(block_bytes) + scratch
    return int(min(V7X_VMEM_LIMIT_CAP, max(2 * est, 16 * 1024 * 1024)))


def _nbytes(shape, dtype):
    n = 1
    for s in shape:
        n *= s
    return n * jnp.dtype(dtype).itemsize


def _rmsnorm(x, g):
    ms = jnp.mean(x * x, axis=-1, keepdims=True)
    return x * lax.rsqrt(ms + EPS) * g


def _dot(a, b):
    return jnp.dot(a, b, preferred_element_type=F32)


def _dot_nt(a, b):
    return lax.dot_general(a, b, (((1,), (1,)), ((), ())), preferred_element_type=F32)


def _ffn_body(x_ref, gin_ref, wg_ref, wu_ref, wd_ref, gout_ref, *rest, emit_x, n_side):
    side_in, rest = rest[:n_side], rest[n_side:]
    if emit_x:
        xo_ref, ho_ref = rest[:2]
        rest = rest[2:]
    else:
        ho_ref = rest[0]
        rest = rest[1:]
    side_out, (h_scr, acc_scr) = rest[:n_side], rest[n_side:]
    j = pl.program_id(1)

    @pl.when(j == 0)
    def _():
        h_scr[...] = _rmsnorm(x_ref[...], gin_ref[...]).astype(BF16)
        acc_scr[...] = jnp.zeros_like(acc_scr)

    h = h_scr[...]
    g = _dot(h, wg_ref[...])
    u = _dot(h, wu_ref[...])
    act = (g * (1.0 / (1.0 + jnp.exp(-g))) * u).astype(BF16)
    acc_scr[...] += _dot(act, wd_ref[...])

    for src, dst in zip(side_in, side_out):
        dst[...] = src[...].astype(dst.dtype)

    @pl.when(j == pl.num_programs(1) - 1)
    def _():
        xn = x_ref[...] + FFN_SCALE * acc_scr[...]
        if emit_x:
            xo_ref[...] = xn
        ho_ref[...] = _rmsnorm(xn, gout_ref[...]).astype(ho_ref.dtype)


def _grid_slab(shape, n_i, n_j):
    rows, cols = shape
    for nr, nc, imap in ((n_i, n_j, lambda i, j: (i, j)), (n_j, n_i, lambda i, j: (j, i))):
        if rows % nr == 0 and cols % nc == 0 and (rows // nr) % 16 == 0 and (cols // nc) % 128 == 0:
            return (rows // nr, cols // nc), imap
    return None


def _ffn_grid(n_tokens, d_ff):
    tm = min(FFN_TM, n_tokens)
    assert n_tokens % tm == 0 and d_ff % FFN_TF == 0
    return n_tokens // tm, d_ff // FFN_TF


def _ffn(x, g_in, w_gu, w_dn, g_out, *, emit_x, h_dtype, side_casts=()):
    T, D = x.shape
    d_ff = w_dn.shape[0]
    grid = _ffn_grid(T, d_ff)
    tm, tf = T // grid[0], FFN_TF
    n_f = grid[1]
    in_specs = [
        pl.BlockSpec((tm, D), lambda i, j: (i, 0)),
        pl.BlockSpec((1, D), lambda i, j: (0, 0)),
        pl.BlockSpec((D, tf), lambda i, j: (0, j)),
        pl.BlockSpec((D, tf), lambda i, j: (0, j + n_f)),
        pl.BlockSpec((tf, D), lambda i, j: (j, 0)),
        pl.BlockSpec((1, D), lambda i, j: (0, 0)),
    ]
    out_shape = [jax.ShapeDtypeStruct((T, D), h_dtype)]
    out_specs = [pl.BlockSpec((tm, D), lambda i, j: (i, 0))]
    if emit_x:
        out_shape = [jax.ShapeDtypeStruct((T, D), F32)] + out_shape
        out_specs = [pl.BlockSpec((tm, D), lambda i, j: (i, 0))] + out_specs
    side_bytes = 0
    for a in side_casts:
        blk, imap = _grid_slab(a.shape, *grid)
        in_specs.append(pl.BlockSpec(blk, imap))
        out_specs.append(pl.BlockSpec(blk, imap))
        out_shape.append(jax.ShapeDtypeStruct(a.shape, BF16))
        side_bytes += _nbytes(blk, F32) + _nbytes(blk, BF16)
    limit = _vmem_limit(
        _nbytes((tm, D), F32), 2 * _nbytes((D, tf), BF16), _nbytes((tf, D), BF16),
        _nbytes((tm, D), h_dtype), _nbytes((tm, D), F32) if emit_x else 0, side_bytes,
        scratch=_nbytes((tm, D), BF16) + _nbytes((tm, D), F32))
    return pl.pallas_call(
        functools.partial(_ffn_body, emit_x=emit_x, n_side=len(side_casts)),
        grid=grid, in_specs=in_specs, out_specs=out_specs, out_shape=out_shape,
        scratch_shapes=[pltpu.VMEM((tm, D), BF16), pltpu.VMEM((tm, D), F32)],
        compiler_params=pltpu.CompilerParams(
            dimension_semantics=("arbitrary", "arbitrary"), vmem_limit_bytes=limit),
        name="ffn_mid" if emit_x else "ffn_final",
    )(x, g_in.reshape(1, D), w_gu, w_gu, w_dn, g_out.reshape(1, D), *side_casts)


def _sigmoid(x):
    return 1.0 / (1.0 + jnp.exp(-x))


def _mm_body(x_ref, w_ref, *rest, out_scale, has_residual):
    if has_residual:
        r_ref, o_ref, wb_scr = rest
    else:
        o_ref, wb_scr = rest

    @pl.when(pl.program_id(1) == 0)
    def _():
        wb_scr[...] = w_ref[...].astype(BF16)

    y = _dot(x_ref[...], wb_scr[...])
    if out_scale is not None:
        y = y * out_scale
    if has_residual:
        y = r_ref[...] + y
    o_ref[...] = y.astype(o_ref.dtype)


def _matmul(x, w, col0, n, out_dtype, residual=None, out_scale=None):
    T, K = x.shape
    tm = min(MM_TM, T)
    tn = min(MM_TN, n)
    assert T % tm == 0 and n % tn == 0 and col0 % tn == 0
    c0 = col0 // tn
    grid = (n // tn, T // tm)
    in_specs = [
        pl.BlockSpec((tm, K), lambda j, i: (i, 0)),
        pl.BlockSpec((K, tn), lambda j, i: (0, j + c0)),
    ]
    args = [x, w]
    blocks = [_nbytes((tm, K), x.dtype), _nbytes((K, tn), w.dtype), _nbytes((tm, tn), out_dtype)]
    if residual is not None:
        in_specs.append(pl.BlockSpec((tm, tn), lambda j, i: (i, j)))
        args.append(residual)
        blocks.append(_nbytes((tm, tn), F32))
    return pl.pallas_call(
        functools.partial(_mm_body, out_scale=out_scale, has_residual=residual is not None),
        grid=grid, in_specs=in_specs,
        out_specs=pl.BlockSpec((tm, tn), lambda j, i: (i, j)),
        out_shape=jax.ShapeDtypeStruct((T, n), out_dtype),
        scratch_shapes=[pltpu.VMEM((K, tn), BF16)],
        compiler_params=pltpu.CompilerParams(
            dimension_semantics=("arbitrary", "arbitrary"),
            vmem_limit_bytes=_vmem_limit(*blocks, scratch=_nbytes((K, tn), BF16))),
        name="proj",
    )(*args)


def _tri(bk):
    j = lax.broadcasted_iota(jnp.int32, (bk, bk), 0)
    s = lax.broadcasted_iota(jnp.int32, (bk, bk), 1)
    return (j >= s).astype(BF16)


LOG2E = 1.4426950408889634
Q_SCALE = HEAD_DIM ** -0.5 * LOG2E


def _softplus2(z2):
    return jnp.maximum(z2, 0.0) + jnp.log(1.0 + jnp.exp2(-jnp.abs(z2))) * LOG2E


def _cumsum_mxu(sp, tri):
    hi = sp.astype(BF16)
    lo = (sp - hi.astype(F32)).astype(BF16)
    if sp.shape[1] % 128 == 0:
        return _dot(jnp.concatenate([hi, lo], axis=1), jnp.concatenate([tri, tri], axis=0))
    return _dot(hi, tri) + _dot(lo, tri)


def _sb_weights(z2, tri, run, masked):
    sp = _softplus2(z2)
    if masked is not None:
        sp = masked(sp)
    tot = _cumsum_mxu(sp, tri)
    if run is not None:
        tot = tot + jnp.concatenate([run] * (z2.shape[1] // run.shape[1]), axis=1)
    a = jnp.exp2(z2 - tot)
    if masked is not None:
        a = masked(a)
    return a.astype(BF16), jnp.sum(sp, axis=1, keepdims=True)


def _attn_prompt_body(q_ref, k_ref, v_ref, tri_ref, o_ref):
    blk = ATT_BLK
    nb = q_ref.shape[0] // blk
    tri = tri_ref[...]
    t = lax.broadcasted_iota(jnp.int32, (blk, blk), 0)
    s = lax.broadcasted_iota(jnp.int32, (blk, blk), 1)
    causal = s < t

    def diag(x):
        head = jnp.where(causal, x[:blk], 0.0)
        return head if x.shape[0] == blk else jnp.concatenate([head, x[blk:]], axis=0)

    heads = [slice(h * HEAD_DIM, (h + 1) * HEAD_DIM) for h in range(ATT_HEADS_PER_STEP)]
    run = [None] * len(heads)
    acc = [None] * len(heads)
    for c in range(nb - 1, -1, -1):
        keys = slice(c * blk, (c + 1) * blk)
        for i, cols in enumerate(heads):
            k_blk = k_ref[keys, cols].astype(BF16)
            v_blk = v_ref[keys, cols].astype(BF16)
            z2 = _dot_nt(q_ref[c * blk:, cols], k_blk)
            if run[i] is not None:
                run[i] = jnp.concatenate([jnp.zeros((blk, HEAD_DIM), F32), run[i]], axis=0)
            a, tot = _sb_weights(z2, tri, run[i], diag)
            pv = _dot(a, v_blk)
            if run[i] is None:
                run[i], acc[i] = jnp.broadcast_to(tot, (blk, HEAD_DIM)), pv
            else:
                run[i] = run[i] + tot
                acc[i] = jnp.concatenate([pv[:blk], acc[i] + pv[blk:]], axis=0)
    for i, cols in enumerate(heads):
        o_ref[:, cols] = acc[i].astype(o_ref.dtype)


def _attn_prompt(q, k, v, batch, seq):
    blk = ATT_BLK
    assert seq % blk == 0 and N_HEADS % ATT_HEADS_PER_STEP == 0
    width = ATT_HEADS_PER_STEP * HEAD_DIM
    spec = pl.BlockSpec((seq, width), lambda b, h: (b, h))
    return pl.pallas_call(
        _attn_prompt_body, grid=(batch, N_HEADS // ATT_HEADS_PER_STEP),
        in_specs=[spec, spec, spec, pl.BlockSpec((blk, blk), lambda b, h: (0, 0))],
        out_specs=spec,
        out_shape=jax.ShapeDtypeStruct(q.shape, BF16),
        compiler_params=pltpu.CompilerParams(
            dimension_semantics=("parallel", "parallel"),
            vmem_limit_bytes=_vmem_limit(2 * _nbytes((seq, width), F32), 2 * _nbytes((seq, width), BF16),
                                         scratch=8 * ATT_HEADS_PER_STEP * _nbytes((seq, blk), F32))),
        name="attn_prompt",
    )(q, k, v, _tri(blk))


def _attn_sample_body(q_ref, kn_ref, vn_ref, kc_ref, vc_ref, tri_ref, trin_ref, o_ref,
                      run_scr, acc_scr, *, n_new):
    step = pl.program_id(1)
    rows = N_HEADS * n_new
    bk = ATT_BLK
    head_cols = [slice(h * HEAD_DIM, (h + 1) * HEAD_DIM) for h in range(N_HEADS)]
    head_rows = [slice(h * n_new, (h + 1) * n_new) for h in range(N_HEADS)]

    def scores(load_k):
        parts = [_dot_nt(q_ref[:, head_cols[h]], load_k(h)) for h in range(N_HEADS)]
        return jnp.concatenate(parts, axis=0)

    def weighted_values(a, load_v):
        return jnp.concatenate([_dot(a[head_rows[h], :], load_v(h)) for h in range(N_HEADS)], axis=0)

    @pl.when(step == 0)
    def _():
        z = scores(lambda h: kn_ref[:, head_cols[h]].astype(BF16))
        t = lax.broadcasted_iota(jnp.int32, (rows, n_new), 0) % n_new
        s = lax.broadcasted_iota(jnp.int32, (rows, n_new), 1)
        a, tot = _sb_weights(z, trin_ref[...], None, lambda x: jnp.where(s < t, x, 0.0))
        acc_scr[...] = weighted_values(a, lambda h: vn_ref[:, head_cols[h]].astype(BF16))
        run_scr[...] = jnp.broadcast_to(tot, run_scr.shape)

    @pl.when(step > 0)
    def _():
        run = run_scr[...]
        acc = acc_scr[...]
        for sb in range(kc_ref.shape[0] // (bk * N_HEADS) - 1, -1, -1):
            def head_rows_of(ref, h):
                return ref[pl.ds(sb * bk * N_HEADS + h, bk, stride=N_HEADS), :].astype(BF16)
            z = scores(lambda h: head_rows_of(kc_ref, h))
            a, tot = _sb_weights(z, tri_ref[...], run, None)
            acc = acc + weighted_values(a, lambda h: head_rows_of(vc_ref, h))
            run = run + tot
        acc_scr[...] = acc
        run_scr[...] = run

    @pl.when(step == pl.num_programs(1) - 1)
    def _():
        for h in range(N_HEADS):
            o_ref[:, head_cols[h]] = acc_scr[head_rows[h], :].astype(o_ref.dtype)


def _attn_sample(q, k_new, v_new, cache_k, cache_v, n_streams, n_new, past_len):
    bk = ATT_BLK
    keys_per_step = min(SAMPLE_KEYS_PER_STEP, past_len)
    assert past_len % keys_per_step == 0 and keys_per_step % bk == 0
    nkb = past_len // keys_per_step
    d_all = N_HEADS * HEAD_DIM
    rows = N_HEADS * n_new

    def cache_map(b, s):
        return (b * nkb + nkb - jnp.maximum(s, 1), 0)

    new_spec = pl.BlockSpec((n_new, d_all), lambda b, s: (b, 0))
    cache_spec = pl.BlockSpec((keys_per_step * N_HEADS, HEAD_DIM), cache_map)
    return pl.pallas_call(
        functools.partial(_attn_sample_body, n_new=n_new),
        grid=(n_streams, nkb + 1),
        in_specs=[new_spec, new_spec, new_spec, cache_spec, cache_spec,
                  pl.BlockSpec((bk, bk), lambda b, s: (0, 0)),
                  pl.BlockSpec((n_new, n_new), lambda b, s: (0, 0))],
        out_specs=new_spec,
        out_shape=jax.ShapeDtypeStruct(q.shape, BF16),
        scratch_shapes=[pltpu.VMEM((rows, HEAD_DIM), F32), pltpu.VMEM((rows, HEAD_DIM), F32)],
        compiler_params=pltpu.CompilerParams(
            dimension_semantics=("parallel", "arbitrary"),
            vmem_limit_bytes=_vmem_limit(2 * _nbytes((keys_per_step * N_HEADS, HEAD_DIM), F32))),
        name="attn_sample",
    )(q, k_new, v_new, cache_k, cache_v, _tri(bk), _tri(n_new))


def _merge(cb, conv, o, ga, gb, wco_ref, wao_ref, m_ref):
    y_a = _dot((cb * conv).astype(BF16), wco_ref[...])
    y_b = _dot(o, wao_ref[...])
    m_ref[...] = (_sigmoid(ga) * y_a + _sigmoid(gb) * y_b).astype(m_ref.dtype)


def _mix_prompt_body(cb_ref, cc_ref, cx_ref, o_ref, ga_ref, gb_ref, cw_ref, wco_ref, wao_ref,
                     m_ref, hist_ref, carry_scr, *, tiles_per_seq):
    i = pl.program_id(0)
    tm = cc_ref.shape[0]

    @pl.when(i % tiles_per_seq == 0)
    def _():
        carry_scr[...] = jnp.zeros_like(carry_scr)

    u = cc_ref[...] * cx_ref[...]
    row = lax.broadcasted_iota(jnp.int32, u.shape, 0)
    p1 = carry_scr[7:8, :]
    p2 = carry_scr[6:7, :]
    u1 = jnp.where(row == 0, p1, pltpu.roll(u, 1, axis=0))
    u2 = jnp.where(row == 0, p2, jnp.where(row == 1, p1, pltpu.roll(u, 2, axis=0)))
    conv = u2 * cw_ref[0:1, :] + u1 * cw_ref[1:2, :] + u * cw_ref[2:3, :]
    carry_scr[...] = u[tm - 8:, :]
    hist_ref[...] = u[tm - (CONV_W - 1):, :]
    _merge(cb_ref[...], conv, o_ref[...], ga_ref[...], gb_ref[...], wco_ref, wao_ref, m_ref)


def _mix_sample_body(cb_ref, cc_ref, cx_ref, o_ref, ga_ref, gb_ref, cw_ref, wco_ref, wao_ref,
                     p1_ref, p2_ref, m_ref, u_ref, *, n_new):
    u = cc_ref[...] * cx_ref[...]
    t = lax.broadcasted_iota(jnp.int32, u.shape, 0) % n_new
    u1 = jnp.where(t == 0, p1_ref[...], pltpu.roll(u, 1, axis=0))
    u2 = jnp.where(t < 2, p2_ref[...], pltpu.roll(u, 2, axis=0))
    conv = u2 * cw_ref[0:1, :] + u1 * cw_ref[1:2, :] + u * cw_ref[2:3, :]
    u_ref[...] = u
    _merge(cb_ref[...], conv, o_ref[...], ga_ref[...], gb_ref[...], wco_ref, wao_ref, m_ref)


def _mix_prompt(pconv, o, gates, conv_w, w_co, w_ao, batch, seq):
    T = pconv.shape[0]
    C = pconv.shape[1] // 3
    D = gates.shape[1] // 2
    Da = o.shape[1]
    tm = MIX_TM
    assert seq % tm == 0
    tps = seq // tm
    row = lambda c: (lambda i: (i, c))
    const = lambda i: (0, 0)
    m, hist = pl.pallas_call(
        functools.partial(_mix_prompt_body, tiles_per_seq=tps),
        grid=(T // tm,),
        in_specs=[
            pl.BlockSpec((tm, C), row(0)), pl.BlockSpec((tm, C), row(1)), pl.BlockSpec((tm, C), row(2)),
            pl.BlockSpec((tm, Da), row(0)),
            pl.BlockSpec((tm, D), row(0)), pl.BlockSpec((tm, D), row(1)),
            pl.BlockSpec((CONV_W, C), const),
            pl.BlockSpec((C, D), const, pipeline_mode=pl.Buffered(1)),
            pl.BlockSpec((Da, D), const, pipeline_mode=pl.Buffered(1)),
        ],
        out_specs=[pl.BlockSpec((tm, D), row(0)),
                   pl.BlockSpec((None, CONV_W - 1, C), lambda i: (i // tps, 0, 0))],
        out_shape=[jax.ShapeDtypeStruct((T, D), BF16),
                   jax.ShapeDtypeStruct((batch, CONV_W - 1, C), F32)],
        scratch_shapes=[pltpu.VMEM((8, C), F32)],
        compiler_params=pltpu.CompilerParams(
            dimension_semantics=("arbitrary",),
            vmem_limit_bytes=_vmem_limit(3 * _nbytes((tm, C), F32), _nbytes((tm, Da), BF16),
                                         2 * _nbytes((tm, D), gates.dtype), _nbytes((C, D), BF16),
                                         _nbytes((Da, D), BF16), _nbytes((tm, D), BF16))),
        name="mix_prompt",
    )(pconv, pconv, pconv, o, gates, gates, conv_w, w_co, w_ao)
    return m, hist


def _mix_sample(pconv, o, gates, conv_w, w_co, w_ao, state, n_streams, n_new):
    T = pconv.shape[0]
    C = pconv.shape[1] // 3
    D = gates.shape[1] // 2
    Da = o.shape[1]
    assert T == n_streams * n_new and n_new >= CONV_W - 1
    zeros = jnp.zeros((n_streams, n_new, C), F32)
    p1 = zeros.at[:, 0].set(state[:, 1]).reshape(T, C)
    p2 = zeros.at[:, 0].set(state[:, 0]).at[:, 1].set(state[:, 1]).reshape(T, C)
    col = lambda c: (lambda i: (0, c))
    const = lambda i: (0, 0)
    m, u = pl.pallas_call(
        functools.partial(_mix_sample_body, n_new=n_new),
        grid=(1,),
        in_specs=[
            pl.BlockSpec((T, C), col(0)), pl.BlockSpec((T, C), col(1)), pl.BlockSpec((T, C), col(2)),
            pl.BlockSpec((T, Da), const),
            pl.BlockSpec((T, D), col(0)), pl.BlockSpec((T, D), col(1)),
            pl.BlockSpec((CONV_W, C), const),
            pl.BlockSpec((C, D), const), pl.BlockSpec((Da, D), const),
            pl.BlockSpec((T, C), const), pl.BlockSpec((T, C), const),
        ],
        out_specs=[pl.BlockSpec((T, D), const), pl.BlockSpec((T, C), const)],
        out_shape=[jax.ShapeDtypeStruct((T, D), BF16), jax.ShapeDtypeStruct((T, C), F32)],
        compiler_params=pltpu.CompilerParams(
            dimension_semantics=("arbitrary",),
            vmem_limit_bytes=_vmem_limit(6 * _nbytes((T, C), F32), _nbytes((T, Da), BF16),
                                         2 * _nbytes((T, D), gates.dtype), _nbytes((C, D), BF16),
                                         _nbytes((Da, D), BF16), _nbytes((T, D), BF16))),
        name="mix_sample",
    )(pconv, pconv, pconv, o, gates, gates, conv_w, w_co, w_ao, p1, p2)
    new_hist = u.reshape(n_streams, n_new, C)[:, n_new - (CONV_W - 1):]
    return m, new_hist


def _layer(x, w, *, batch, seq, sample_state=None, side_casts=()):
    C = w["conv_w"].shape[1]
    Da = N_HEADS * HEAD_DIM
    D = x.shape[1]
    x1, h, *side = _ffn(x, w["norm_ffn1"], w["w_gu1"], w["w_dn1"], w["norm_mix"], emit_x=True,
                        h_dtype=BF16, side_casts=side_casts)
    pconv = _matmul(h, w["w_in"], 0, 3 * C, F32)
    q = _matmul(h, w["w_in"], 3 * C, Da, BF16, out_scale=Q_SCALE)
    k = _matmul(h, w["w_in"], 3 * C + Da, Da, F32)
    v = _matmul(h, w["w_in"], 3 * C + 2 * Da, Da, F32)
    gates = _matmul(h, w["w_in"], 3 * C + 3 * Da, 2 * D, F32)
    if sample_state is None:
        o = _attn_prompt(q, k, v, batch, seq)
        m, hist = _mix_prompt(pconv, o, gates, w["conv_w"], w["w_co"], w["w_ao"], batch, seq)
    else:
        conv_state, cache_k, cache_v, past_len = sample_state
        o = _attn_sample(q, k, v, cache_k.reshape(batch * past_len * N_HEADS, HEAD_DIM),
                         cache_v.reshape(batch * past_len * N_HEADS, HEAD_DIM), batch, seq, past_len)
        m, hist = _mix_sample(pconv, o, gates, w["conv_w"], w["w_co"], w["w_ao"], conv_state, batch, seq)
    x2 = _matmul(m, w["w_o"], 0, D, F32, residual=x1)
    return x2, hist, k, v, side


def kernel(x_prompt, x_sample, cache_k, cache_v, state_conv, norm_ffn1, ffn1_w_gate_up, ffn1_w_down,
           norm_mix, w_in, conv_w, w_conv_out, w_attn_out, w_o, norm_ffn2, ffn2_w_gate_up,
           ffn2_w_down, norm_final):
    depth = w_in.shape[0]
    B, T, D = x_prompt.shape
    S, n_new, _ = x_sample.shape
    past_len = cache_k.shape[2]
    xp = x_prompt.reshape(B * T, D)
    xs = x_sample.reshape(S * n_new, D)
    outs = [[] for _ in range(6)]
    for l in range(depth):
        w = {
            "norm_ffn1": norm_ffn1[l], "w_gu1": ffn1_w_gate_up[l].astype(BF16),
            "w_dn1": ffn1_w_down[l].astype(BF16), "norm_mix": norm_mix[l],
            "w_in": w_in[l], "conv_w": conv_w[l],
            "w_co": w_conv_out[l].astype(BF16), "w_ao": w_attn_out[l].astype(BF16),
            "w_o": w_o[l],
        }
        ffn2_f32 = (ffn2_w_gate_up[l], ffn2_w_down[l])
        grid_p = _ffn_grid(B * T, ffn1_w_down.shape[1])
        fuse = all(_grid_slab(a.shape, *grid_p) is not None for a in ffn2_f32)
        xp2, c_p, k_p, v_p, ffn2_bf = _layer(xp, w, batch=B, seq=T, side_casts=ffn2_f32 if fuse else ())
        if not fuse:
            ffn2_bf = [a.astype(BF16) for a in ffn2_f32]
        xs2, c_s, k_s, v_s, _ = _layer(xs, w, batch=S, seq=n_new,
                                       sample_state=(state_conv[l], cache_k[l], cache_v[l], past_len))
        w2 = (norm_ffn2[l], *ffn2_bf, norm_final)
        if l == depth - 1:
            xp = _ffn(xp2, *w2, emit_x=False, h_dtype=F32)[0]
            xs = _ffn(xs2, *w2, emit_x=False, h_dtype=F32)[0]
        else:
            xp = _ffn(xp2, *w2, emit_x=True, h_dtype=BF16)[0]
            xs = _ffn(xs2, *w2, emit_x=True, h_dtype=BF16)[0]
        for lst, val in zip(outs, (k_p.reshape(B, T, N_HEADS, HEAD_DIM), v_p.reshape(B, T, N_HEADS, HEAD_DIM),
                                   c_p, k_s.reshape(S, n_new, N_HEADS, HEAD_DIM),
                                   v_s.reshape(S, n_new, N_HEADS, HEAD_DIM), c_s)):
            lst.append(val)
    y_prompt = xp.reshape(B, T, D)
    y_sample = xs.reshape(S, n_new, D)
    return (y_prompt, y_sample) + tuple(jnp.stack(o, axis=0) for o in outs)
```

```python
import functools

import jax
import jax.numpy as jnp
from jax import lax
from jax.experimental import pallas as pl
from jax.experimental.pallas import tpu as pltpu

N_HEADS = 8
HEAD_DIM = 128
CONV_W = 3
EPS = 1e-6
FFN_SCALE = 0.5

F32 = jnp.float32
BF16 = jnp.bfloat16

V7X_VMEM_LIMIT_CAP = 60 * 1024 * 1024

FFN_TM = 512
FFN_TF = 512
FFN_TF_F32 = 256
MM_TM = 1024
MM_TN = 1024
MIX_TM = 512
ATT_BLK = 256
ATT_HEADS_PER_STEP = 2
SAMPLE_KEYS_PER_STEP = 1024


def _vmem_limit(*block_bytes, scratch=0):
    est = 2 * sum(block_bytes) + scratch
    return int(min(V7X_VMEM_LIMIT_CAP, max(2 * est, 16 * 1024 * 1024)))


def _nbytes(shape, dtype):
    n = 1
    for s in shape:
        n *= s
    return n * jnp.dtype(dtype).itemsize


def _rmsnorm(x, g):
    ms = jnp.mean(x * x, axis=-1, keepdims=True)
    return x * lax.rsqrt(ms + EPS) * g


def _dot(a, b):
    return jnp.dot(a, b, preferred_element_type=F32)


def _dot_nt(a, b):
    return lax.dot_general(a, b, (((1,), (1,)), ((), ())), preferred_element_type=F32)


def _ffn_body(x_ref, gin_ref, wg_ref, wu_ref, wd_ref, gout_ref, *rest, emit_x, n_side, keep_weights):
    side_in, rest = rest[:n_side], rest[n_side:]
    if emit_x:
        xo_ref, ho_ref = rest[:2]
        rest = rest[2:]
    else:
        ho_ref = rest[0]
        rest = rest[1:]
    side_out, rest = rest[:n_side], rest[n_side:]
    if keep_weights:
        weight_out, rest = rest[:3], rest[3:]
    h_scr, acc_scr = rest
    j = pl.program_id(1)

    @pl.when(j == 0)
    def _():
        h_scr[...] = _rmsnorm(x_ref[...], gin_ref[...]).astype(BF16)
        acc_scr[...] = jnp.zeros_like(acc_scr)

    wg, wu, wd = wg_ref[...], wu_ref[...], wd_ref[...]
    if keep_weights:
        wg, wu, wd = wg.astype(BF16), wu.astype(BF16), wd.astype(BF16)
        for dst, tile in zip(weight_out, (wg, wu, wd)):
            dst[...] = tile

    h = h_scr[...]
    g = _dot(h, wg)
    u = _dot(h, wu)
    act = (g * (1.0 / (1.0 + jnp.exp(-g))) * u).astype(BF16)
    acc_scr[...] += _dot(act, wd)

    for src, dst in zip(side_in, side_out):
        dst[...] = src[...].astype(dst.dtype)

    @pl.when(j == pl.num_programs(1) - 1)
    def _():
        xn = x_ref[...] + FFN_SCALE * acc_scr[...]
        if emit_x:
            xo_ref[...] = xn
        ho_ref[...] = _rmsnorm(xn, gout_ref[...]).astype(ho_ref.dtype)


def _grid_slab(shape, n_i, n_j):
    rows, cols = shape
    for nr, nc, imap in ((n_i, n_j, lambda i, j: (i, j)), (n_j, n_i, lambda i, j: (j, i))):
        if rows % nr == 0 and cols % nc == 0 and (rows // nr) % 16 == 0 and (cols // nc) % 128 == 0:
            return (rows // nr, cols // nc), imap
    return None


def _ffn_grid(n_tokens, d_ff, keep_weights=False):
    tm = min(FFN_TM, n_tokens)
    tf = FFN_TF_F32 if keep_weights else FFN_TF
    assert n_tokens % tm == 0 and d_ff % tf == 0
    return n_tokens // tm, d_ff // tf


def _ffn(x, g_in, w_gu, w_dn, g_out, *, emit_x, h_dtype, side_casts=(), keep_weights=False):
    T, D = x.shape
    d_ff = w_dn.shape[0]
    grid = _ffn_grid(T, d_ff, keep_weights)
    tm, tf = T // grid[0], d_ff // grid[1]
    n_f = grid[1]
    w_gate, w_up = w_gu if isinstance(w_gu, tuple) else (w_gu, w_gu)
    up0 = 0 if isinstance(w_gu, tuple) else n_f
    w_dtype = w_dn.dtype
    assert (w_dtype == F32) == keep_weights and (grid[0] == 1 or not keep_weights)
    in_specs = [
        pl.BlockSpec((tm, D), lambda i, j: (i, 0)),
        pl.BlockSpec((1, D), lambda i, j: (0, 0)),
        pl.BlockSpec((D, tf), lambda i, j: (0, j)),
        pl.BlockSpec((D, tf), lambda i, j: (0, j + up0)),
        pl.BlockSpec((tf, D), lambda i, j: (j, 0)),
        pl.BlockSpec((1, D), lambda i, j: (0, 0)),
    ]
    out_shape = [jax.ShapeDtypeStruct((T, D), h_dtype)]
    out_specs = [pl.BlockSpec((tm, D), lambda i, j: (i, 0))]
    if emit_x:
        out_shape = [jax.ShapeDtypeStruct((T, D), F32)] + out_shape
        out_specs = [pl.BlockSpec((tm, D), lambda i, j: (i, 0))] + out_specs
    side_bytes = 0
    for a in side_casts:
        blk, imap = _grid_slab(a.shape, *grid)
        in_specs.append(pl.BlockSpec(blk, imap))
        out_specs.append(pl.BlockSpec(blk, imap))
        out_shape.append(jax.ShapeDtypeStruct(a.shape, BF16))
        side_bytes += _nbytes(blk, F32) + _nbytes(blk, BF16)
    if keep_weights:
        out_specs += [pl.BlockSpec((D, tf), lambda i, j: (0, j)),
                      pl.BlockSpec((D, tf), lambda i, j: (0, j)),
                      pl.BlockSpec((tf, D), lambda i, j: (j, 0))]
        out_shape += [jax.ShapeDtypeStruct((D, d_ff), BF16), jax.ShapeDtypeStruct((D, d_ff), BF16),
                      jax.ShapeDtypeStruct((d_ff, D), BF16)]
        side_bytes += 3 * _nbytes((D, tf), BF16)
    limit = _vmem_limit(
        _nbytes((tm, D), F32), 3 * _nbytes((D, tf), w_dtype),
        _nbytes((tm, D), h_dtype), _nbytes((tm, D), F32) if emit_x else 0, side_bytes,
        scratch=_nbytes((tm, D), BF16) + _nbytes((tm, D), F32))
    return pl.pallas_call(
        functools.partial(_ffn_body, emit_x=emit_x, n_side=len(side_casts), keep_weights=keep_weights),
        grid=grid, in_specs=in_specs, out_specs=out_specs, out_shape=out_shape,
        scratch_shapes=[pltpu.VMEM((tm, D), BF16), pltpu.VMEM((tm, D), F32)],
        compiler_params=pltpu.CompilerParams(
            dimension_semantics=("arbitrary", "arbitrary"), vmem_limit_bytes=limit),
        name="ffn_mid" if emit_x else "ffn_final",
    )(x, g_in.reshape(1, D), w_gate, w_up, w_dn, g_out.reshape(1, D), *side_casts)


def _sigmoid(x):
    return 1.0 / (1.0 + jnp.exp(-x))


def _mm_body(x_ref, w_ref, *rest, out_scale, has_residual, n_side):
    if has_residual:
        r_ref, rest = rest[0], rest[1:]
    side_in, o_ref, side_out, wb_scr = rest[:n_side], rest[n_side], rest[n_side + 1:-1], rest[-1]

    @pl.when(pl.program_id(1) == 0)
    def _():
        wb_scr[...] = w_ref[...].astype(BF16)

    y = _dot(x_ref[...], wb_scr[...])
    if out_scale is not None:
        y = y * out_scale
    if has_residual:
        y = r_ref[...] + y
    o_ref[...] = y.astype(o_ref.dtype)
    for src, dst in zip(side_in, side_out):
        dst[...] = src[...].astype(dst.dtype)


def _mm_grid(n_tokens, n):
    tm, tn = min(MM_TM, n_tokens), min(MM_TN, n)
    assert n_tokens % tm == 0 and n % tn == 0
    return n // tn, n_tokens // tm


def _matmul(x, w, col0, n, out_dtype, residual=None, out_scale=None, side_casts=()):
    T, K = x.shape
    grid = _mm_grid(T, n)
    tm, tn = T // grid[1], n // grid[0]
    assert col0 % tn == 0
    c0 = col0 // tn
    in_specs = [
        pl.BlockSpec((tm, K), lambda j, i: (i, 0)),
        pl.BlockSpec((K, tn), lambda j, i: (0, j + c0)),
    ]
    args = [x, w]
    blocks = [_nbytes((tm, K), x.dtype), _nbytes((K, tn), w.dtype), _nbytes((tm, tn), out_dtype)]
    if residual is not None:
        in_specs.append(pl.BlockSpec((tm, tn), lambda j, i: (i, j)))
        args.append(residual)
        blocks.append(_nbytes((tm, tn), F32))
    out_specs = [pl.BlockSpec((tm, tn), lambda j, i: (i, j))]
    out_shape = [jax.ShapeDtypeStruct((T, n), out_dtype)]
    for a in side_casts:
        blk, imap = _grid_slab(a.shape, *grid)
        in_specs.append(pl.BlockSpec(blk, imap))
        args.append(a)
        out_specs.append(pl.BlockSpec(blk, imap))
        out_shape.append(jax.ShapeDtypeStruct(a.shape, BF16))
        blocks.append(_nbytes(blk, F32) + _nbytes(blk, BF16))
    outs = pl.pallas_call(
        functools.partial(_mm_body, out_scale=out_scale, has_residual=residual is not None,
                          n_side=len(side_casts)),
        grid=grid, in_specs=in_specs, out_specs=out_specs, out_shape=out_shape,
        scratch_shapes=[pltpu.VMEM((K, tn), BF16)],
        compiler_params=pltpu.CompilerParams(
            dimension_semantics=("arbitrary", "arbitrary"),
            vmem_limit_bytes=_vmem_limit(*blocks, scratch=_nbytes((K, tn), BF16))),
        name="proj",
    )(*args)
    return outs if side_casts else outs[0]


def _tri(bk):
    j = lax.broadcasted_iota(jnp.int32, (bk, bk), 0)
    s = lax.broadcasted_iota(jnp.int32, (bk, bk), 1)
    return (j >= s).astype(BF16)


LOG2E = 1.4426950408889634
Q_SCALE = HEAD_DIM ** -0.5 * LOG2E


def _softplus2(z2):
    return jnp.maximum(z2, 0.0) + jnp.log(1.0 + jnp.exp2(-jnp.abs(z2))) * LOG2E


def _cumsum_mxu(sp, tri):
    hi = sp.astype(BF16)
    lo = (sp - hi.astype(F32)).astype(BF16)
    if sp.shape[1] % 128 == 0:
        return _dot(jnp.concatenate([hi, lo], axis=1), jnp.concatenate([tri, tri], axis=0))
    return _dot(hi, tri) + _dot(lo, tri)


def _sb_weights(z2, tri, run, masked):
    sp = _softplus2(z2)
    if masked is not None:
        sp = masked(sp)
    tot = _cumsum_mxu(sp, tri)
    if run is not None:
        tot = tot + jnp.concatenate([run] * (z2.shape[1] // run.shape[1]), axis=1)
    a = jnp.exp2(z2 - tot)
    if masked is not None:
        a = masked(a)
    return a.astype(BF16), jnp.sum(sp, axis=1, keepdims=True)


def _attn_prompt_body(q_ref, k_ref, v_ref, tri_ref, o_ref):
    blk = ATT_BLK
    nb = q_ref.shape[0] // blk
    tri = tri_ref[...]
    t = lax.broadcasted_iota(jnp.int32, (blk, blk), 0)
    s = lax.broadcasted_iota(jnp.int32, (blk, blk), 1)
    causal = s < t

    def diag(x):
        head = jnp.where(causal, x[:blk], 0.0)
        return head if x.shape[0] == blk else jnp.concatenate([head, x[blk:]], axis=0)

    heads = [slice(h * HEAD_DIM, (h + 1) * HEAD_DIM) for h in range(ATT_HEADS_PER_STEP)]
    run = [None] * len(heads)
    acc = [None] * len(heads)
    for c in range(nb - 1, -1, -1):
        keys = slice(c * blk, (c + 1) * blk)
        for i, cols in enumerate(heads):
            k_blk = k_ref[keys, cols].astype(BF16)
            v_blk = v_ref[keys, cols].astype(BF16)
            z2 = _dot_nt(q_ref[c * blk:, cols], k_blk)
            if run[i] is not None:
                run[i] = jnp.concatenate([jnp.zeros((blk, HEAD_DIM), F32), run[i]], axis=0)
            a, tot = _sb_weights(z2, tri, run[i], diag)
            pv = _dot(a, v_blk)
            if run[i] is None:
                run[i], acc[i] = jnp.broadcast_to(tot, (blk, HEAD_DIM)), pv
            else:
                run[i] = run[i] + tot
                acc[i] = jnp.concatenate([pv[:blk], acc[i] + pv[blk:]], axis=0)
    for i, cols in enumerate(heads):
        o_ref[:, cols] = acc[i].astype(o_ref.dtype)


def _attn_prompt(q, k, v, batch, seq):
    blk = ATT_BLK
    assert seq % blk == 0 and N_HEADS % ATT_HEADS_PER_STEP == 0
    width = ATT_HEADS_PER_STEP * HEAD_DIM
    spec = pl.BlockSpec((seq, width), lambda b, h: (b, h))
    return pl.pallas_call(
        _attn_prompt_body, grid=(batch, N_HEADS // ATT_HEADS_PER_STEP),
        in_specs=[spec, spec, spec, pl.BlockSpec((blk, blk), lambda b, h: (0, 0))],
        out_specs=spec,
        out_shape=jax.ShapeDtypeStruct(q.shape, BF16),
        compiler_params=pltpu.CompilerParams(
            dimension_semantics=("parallel", "parallel"),
            vmem_limit_bytes=_vmem_limit(2 * _nbytes((seq, width), F32), 2 * _nbytes((seq, width), BF16),
                                         scratch=8 * ATT_HEADS_PER_STEP * _nbytes((seq, blk), F32))),
        name="attn_prompt",
    )(q, k, v, _tri(blk))


def _attn_sample_body(q_ref, kn_ref, vn_ref, kc_ref, vc_ref, tri_ref, trin_ref, o_ref,
                      run_scr, acc_scr, *, n_new):
    step = pl.program_id(1)
    rows = N_HEADS * n_new
    bk = ATT_BLK
    head_cols = [slice(h * HEAD_DIM, (h + 1) * HEAD_DIM) for h in range(N_HEADS)]
    head_rows = [slice(h * n_new, (h + 1) * n_new) for h in range(N_HEADS)]

    def scores(load_k):
        parts = [_dot_nt(q_ref[:, head_cols[h]], load_k(h)) for h in range(N_HEADS)]
        return jnp.concatenate(parts, axis=0)

    def weighted_values(a, load_v):
        return jnp.concatenate([_dot(a[head_rows[h], :], load_v(h)) for h in range(N_HEADS)], axis=0)

    @pl.when(step == 0)
    def _():
        z = scores(lambda h: kn_ref[:, head_cols[h]].astype(BF16))
        t = lax.broadcasted_iota(jnp.int32, (rows, n_new), 0) % n_new
        s = lax.broadcasted_iota(jnp.int32, (rows, n_new), 1)
        a, tot = _sb_weights(z, trin_ref[...], None, lambda x: jnp.where(s < t, x, 0.0))
        acc_scr[...] = weighted_values(a, lambda h: vn_ref[:, head_cols[h]].astype(BF16))
        run_scr[...] = jnp.broadcast_to(tot, run_scr.shape)

    @pl.when(step > 0)
    def _():
        run = run_scr[...]
        acc = acc_scr[...]
        for sb in range(kc_ref.shape[0] // (bk * N_HEADS) - 1, -1, -1):
            def head_rows_of(ref, h):
                return ref[pl.ds(sb * bk * N_HEADS + h, bk, stride=N_HEADS), :].astype(BF16)
            z = scores(lambda h: head_rows_of(kc_ref, h))
            a, tot = _sb_weights(z, tri_ref[...], run, None)
            acc = acc + weighted_values(a, lambda h: head_rows_of(vc_ref, h))
            run = run + tot
        acc_scr[...] = acc
        run_scr[...] = run

    @pl.when(step == pl.num_programs(1) - 1)
    def _():
        for h in range(N_HEADS):
            o_ref[:, head_cols[h]] = acc_scr[head_rows[h], :].astype(o_ref.dtype)


def _attn_sample(q, k_new, v_new, cache_k, cache_v, n_streams, n_new, past_len):
    bk = ATT_BLK
    keys_per_step = min(SAMPLE_KEYS_PER_STEP, past_len)
    assert past_len % keys_per_step == 0 and keys_per_step % bk == 0
    nkb = past_len // keys_per_step
    d_all = N_HEADS * HEAD_DIM
    rows = N_HEADS * n_new

    def cache_map(b, s):
        return (b * nkb + nkb - jnp.maximum(s, 1), 0)

    new_spec = pl.BlockSpec((n_new, d_all), lambda b, s: (b, 0))
    cache_spec = pl.BlockSpec((keys_per_step * N_HEADS, HEAD_DIM), cache_map)
    return pl.pallas_call(
        functools.partial(_attn_sample_body, n_new=n_new),
        grid=(n_streams, nkb + 1),
        in_specs=[new_spec, new_spec, new_spec, cache_spec, cache_spec,
                  pl.BlockSpec((bk, bk), lambda b, s: (0, 0)),
                  pl.BlockSpec((n_new, n_new), lambda b, s: (0, 0))],
        out_specs=new_spec,
        out_shape=jax.ShapeDtypeStruct(q.shape, BF16),
        scratch_shapes=[pltpu.VMEM((rows, HEAD_DIM), F32), pltpu.VMEM((rows, HEAD_DIM), F32)],
        compiler_params=pltpu.CompilerParams(
            dimension_semantics=("parallel", "arbitrary"),
            vmem_limit_bytes=_vmem_limit(2 * _nbytes((keys_per_step * N_HEADS, HEAD_DIM), F32))),
        name="attn_sample",
    )(q, k_new, v_new, cache_k, cache_v, _tri(bk), _tri(n_new))


def _merge(cb, conv, o, ga, gb, wco_ref, wao_ref, m_ref):
    y_a = _dot((cb * conv).astype(BF16), wco_ref[...])
    y_b = _dot(o, wao_ref[...])
    m_ref[...] = (_sigmoid(ga) * y_a + _sigmoid(gb) * y_b).astype(m_ref.dtype)


def _mix_prompt_body(cb_ref, cc_ref, cx_ref, o_ref, ga_ref, gb_ref, cw_ref, wco_ref, wao_ref,
                     m_ref, hist_ref, carry_scr, *, tiles_per_seq):
    i = pl.program_id(0)
    tm = cc_ref.shape[0]

    @pl.when(i % tiles_per_seq == 0)
    def _():
        carry_scr[...] = jnp.zeros_like(carry_scr)

    u = cc_ref[...] * cx_ref[...]
    row = lax.broadcasted_iota(jnp.int32, u.shape, 0)
    p1 = carry_scr[7:8, :]
    p2 = carry_scr[6:7, :]
    u1 = jnp.where(row == 0, p1, pltpu.roll(u, 1, axis=0))
    u2 = jnp.where(row == 0, p2, jnp.where(row == 1, p1, pltpu.roll(u, 2, axis=0)))
    conv = u2 * cw_ref[0:1, :] + u1 * cw_ref[1:2, :] + u * cw_ref[2:3, :]
    carry_scr[...] = u[tm - 8:, :]
    hist_ref[...] = u[tm - (CONV_W - 1):, :]
    _merge(cb_ref[...], conv, o_ref[...], ga_ref[...], gb_ref[...], wco_ref, wao_ref, m_ref)


def _mix_sample_body(cb_ref, cc_ref, cx_ref, o_ref, ga_ref, gb_ref, cw_ref, wco_ref, wao_ref,
                     p1_ref, p2_ref, m_ref, u_ref, *, n_new):
    u = cc_ref[...] * cx_ref[...]
    t = lax.broadcasted_iota(jnp.int32, u.shape, 0) % n_new
    u1 = jnp.where(t == 0, p1_ref[...], pltpu.roll(u, 1, axis=0))
    u2 = jnp.where(t < 2, p2_ref[...], pltpu.roll(u, 2, axis=0))
    conv = u2 * cw_ref[0:1, :] + u1 * cw_ref[1:2, :] + u * cw_ref[2:3, :]
    u_ref[...] = u
    _merge(cb_ref[...], conv, o_ref[...], ga_ref[...], gb_ref[...], wco_ref, wao_ref, m_ref)


def _mix_prompt(pconv, o, gates, conv_w, w_co, w_ao, batch, seq):
    T = pconv.shape[0]
    C = pconv.shape[1] // 3
    D = gates.shape[1] // 2
    Da = o.shape[1]
    tm = MIX_TM
    assert seq % tm == 0
    tps = seq // tm
    row = lambda c: (lambda i: (i, c))
    const = lambda i: (0, 0)
    m, hist = pl.pallas_call(
        functools.partial(_mix_prompt_body, tiles_per_seq=tps),
        grid=(T // tm,),
        in_specs=[
            pl.BlockSpec((tm, C), row(0)), pl.BlockSpec((tm, C), row(1)), pl.BlockSpec((tm, C), row(2)),
            pl.BlockSpec((tm, Da), row(0)),
            pl.BlockSpec((tm, D), row(0)), pl.BlockSpec((tm, D), row(1)),
            pl.BlockSpec((CONV_W, C), const),
            pl.BlockSpec((C, D), const, pipeline_mode=pl.Buffered(1)),
            pl.BlockSpec((Da, D), const, pipeline_mode=pl.Buffered(1)),
        ],
        out_specs=[pl.BlockSpec((tm, D), row(0)),
                   pl.BlockSpec((None, CONV_W - 1, C), lambda i: (i // tps, 0, 0))],
        out_shape=[jax.ShapeDtypeStruct((T, D), BF16),
                   jax.ShapeDtypeStruct((batch, CONV_W - 1, C), F32)],
        scratch_shapes=[pltpu.VMEM((8, C), F32)],
        compiler_params=pltpu.CompilerParams(
            dimension_semantics=("arbitrary",),
            vmem_limit_bytes=_vmem_limit(3 * _nbytes((tm, C), F32), _nbytes((tm, Da), BF16),
                                         2 * _nbytes((tm, D), gates.dtype), _nbytes((C, D), BF16),
                                         _nbytes((Da, D), BF16), _nbytes((tm, D), BF16))),
        name="mix_prompt",
    )(pconv, pconv, pconv, o, gates, gates, conv_w, w_co, w_ao)
    return m, hist


def _mix_sample(pconv, o, gates, conv_w, w_co, w_ao, state, n_streams, n_new):
    T = pconv.shape[0]
    C = pconv.shape[1] // 3
    D = gates.shape[1] // 2
    Da = o.shape[1]
    assert T == n_streams * n_new and n_new >= CONV_W - 1
    zeros = jnp.zeros((n_streams, n_new, C), F32)
    p1 = zeros.at[:, 0].set(state[:, 1]).reshape(T, C)
    p2 = zeros.at[:, 0].set(state[:, 0]).at[:, 1].set(state[:, 1]).reshape(T, C)
    col = lambda c: (lambda i: (0, c))
    const = lambda i: (0, 0)
    m, u = pl.pallas_call(
        functools.partial(_mix_sample_body, n_new=n_new),
        grid=(1,),
        in_specs=[
            pl.BlockSpec((T, C), col(0)), pl.BlockSpec((T, C), col(1)), pl.BlockSpec((T, C), col(2)),
            pl.BlockSpec((T, Da), const),
            pl.BlockSpec((T, D), col(0)), pl.BlockSpec((T, D), col(1)),
            pl.BlockSpec((CONV_W, C), const),
            pl.BlockSpec((C, D), const), pl.BlockSpec((Da, D), const),
            pl.BlockSpec((T, C), const), pl.BlockSpec((T, C), const),
        ],
        out_specs=[pl.BlockSpec((T, D), const), pl.BlockSpec((T, C), const)],
        out_shape=[jax.ShapeDtypeStruct((T, D), BF16), jax.ShapeDtypeStruct((T, C), F32)],
        compiler_params=pltpu.CompilerParams(
            dimension_semantics=("arbitrary",),
            vmem_limit_bytes=_vmem_limit(6 * _nbytes((T, C), F32), _nbytes((T, Da), BF16),
                                         2 * _nbytes((T, D), gates.dtype), _nbytes((C, D), BF16),
                                         _nbytes((Da, D), BF16), _nbytes((T, D), BF16))),
        name="mix_sample",
    )(pconv, pconv, pconv, o, gates, gates, conv_w, w_co, w_ao, p1, p2)
    new_hist = u.reshape(n_streams, n_new, C)[:, n_new - (CONV_W - 1):]
    return m, new_hist


def _mixer(x1, h, w, *, batch, seq, sample_state=None, merge_weights=None):
    C = w["conv_w"].shape[1]
    Da = N_HEADS * HEAD_DIM
    T, D = x1.shape
    pconv = _matmul(h, w["w_in"], 0, 3 * C, F32)
    q = _matmul(h, w["w_in"], 3 * C, Da, BF16, out_scale=Q_SCALE)
    k = _matmul(h, w["w_in"], 3 * C + Da, Da, F32)
    v = _matmul(h, w["w_in"], 3 * C + 2 * Da, Da, F32)
    merge_f32 = (w["w_co"], w["w_ao"])
    if merge_weights is None and all(_grid_slab(a.shape, *_mm_grid(T, 2 * D)) for a in merge_f32):
        gates, *merge_weights = _matmul(h, w["w_in"], 3 * C + 3 * Da, 2 * D, F32, side_casts=merge_f32)
    else:
        gates = _matmul(h, w["w_in"], 3 * C + 3 * Da, 2 * D, F32)
    if merge_weights is None:
        merge_weights = [a.astype(BF16) for a in merge_f32]
    w_co, w_ao = merge_weights
    if sample_state is None:
        o = _attn_prompt(q, k, v, batch, seq)
        m, hist = _mix_prompt(pconv, o, gates, w["conv_w"], w_co, w_ao, batch, seq)
    else:
        conv_state, cache_k, cache_v, past_len = sample_state
        o = _attn_sample(q, k, v, cache_k.reshape(batch * past_len * N_HEADS, HEAD_DIM),
                         cache_v.reshape(batch * past_len * N_HEADS, HEAD_DIM), batch, seq, past_len)
        m, hist = _mix_sample(pconv, o, gates, w["conv_w"], w_co, w_ao, conv_state, batch, seq)
    x2 = _matmul(m, w["w_o"], 0, D, F32, residual=x1)
    return x2, hist, k, v, merge_weights


def kernel(x_prompt, x_sample, cache_k, cache_v, state_conv, norm_ffn1, ffn1_w_gate_up, ffn1_w_down,
           norm_mix, w_in, conv_w, w_conv_out, w_attn_out, w_o, norm_ffn2, ffn2_w_gate_up,
           ffn2_w_down, norm_final):
    depth = w_in.shape[0]
    B, T, D = x_prompt.shape
    S, n_new, _ = x_sample.shape
    past_len = cache_k.shape[2]
    xp = x_prompt.reshape(B * T, D)
    xs = x_sample.reshape(S * n_new, D)
    outs = [[] for _ in range(6)]
    for l in range(depth):
        w = {"w_in": w_in[l], "conv_w": conv_w[l], "w_co": w_conv_out[l], "w_ao": w_attn_out[l],
             "w_o": w_o[l]}
        ffn1 = dict(emit_x=True, h_dtype=BF16)
        d_ff = ffn1_w_down.shape[1]
        if _ffn_grid(S * n_new, d_ff)[0] == 1:
            xs1, hs, w_gate1, w_up1, w_dn1 = _ffn(xs, norm_ffn1[l], ffn1_w_gate_up[l], ffn1_w_down[l],
                                                  norm_mix[l], keep_weights=True, **ffn1)
            w_gu1 = (w_gate1, w_up1)
        else:
            w_gu1, w_dn1 = ffn1_w_gate_up[l].astype(BF16), ffn1_w_down[l].astype(BF16)
            xs1, hs = _ffn(xs, norm_ffn1[l], w_gu1, w_dn1, norm_mix[l], **ffn1)
        ffn2_f32 = (ffn2_w_gate_up[l], ffn2_w_down[l])
        fuse = all(_grid_slab(a.shape, *_ffn_grid(B * T, d_ff)) for a in ffn2_f32)
        xp1, hp, *ffn2_bf = _ffn(xp, norm_ffn1[l], w_gu1, w_dn1, norm_mix[l],
                                 side_casts=ffn2_f32 if fuse else (), **ffn1)
        if not fuse:
            ffn2_bf = [a.astype(BF16) for a in ffn2_f32]
        xp2, c_p, k_p, v_p, merge_w = _mixer(xp1, hp, w, batch=B, seq=T)
        xs2, c_s, k_s, v_s, _ = _mixer(xs1, hs, w, batch=S, seq=n_new, merge_weights=merge_w,
                                       sample_state=(state_conv[l], cache_k[l], cache_v[l], past_len))
        w2 = (norm_ffn2[l], *ffn2_bf, norm_final)
        if l == depth - 1:
            xp = _ffn(xp2, *w2, emit_x=False, h_dtype=F32)[0]
            xs = _ffn(xs2, *w2, emit_x=False, h_dtype=F32)[0]
        else:
            xp = _ffn(xp2, *w2, emit_x=True, h_dtype=BF16)[0]
            xs = _ffn(xs2, *w2, emit_x=True, h_dtype=BF16)[0]
        for lst, val in zip(outs, (k_p.reshape(B, T, N_HEADS, HEAD_DIM), v_p.reshape(B, T, N_HEADS, HEAD_DIM),
                                   c_p, k_s.reshape(S, n_new, N_HEADS, HEAD_DIM),
                                   v_s.reshape(S, n_new, N_HEADS, HEAD_DIM), c_s)):
            lst.append(val)
    y_prompt = xp.reshape(B, T, D)
    y_sample = xs.reshape(S, n_new, D)
    return (y_prompt, y_sample) + tuple(jnp.stack(o, axis=0) for o in outs)
```

```python
import functools

import jax
import jax.numpy as jnp
from jax import lax
from jax.experimental import pallas as pl
from jax.experimental.pallas import tpu as pltpu

N_HEADS = 8
HEAD_DIM = 128
CONV_W = 3
EPS = 1e-6
FFN_SCALE = 0.5

F32 = jnp.float32
BF16 = jnp.bfloat16

V7X_VMEM_LIMIT_CAP = 60 * 1024 * 1024

FFN_TM = 512
FFN_TF = 512
FFN_TF_F32 = 256
MM_TM = 1024
MM_TN = 1024
MIX_TM = 512
ATT_BLK = 256
ATT_HEADS_PER_STEP = 2
ATT_DEAD_LOG2 = 160.0
SAMPLE_KEYS_PER_STEP = 1024


def _vmem_limit(*block_bytes, scratch=0):
    est = 2 * sum(block_bytes) + scratch
    return int(min(V7X_VMEM_LIMIT_CAP, max(2 * est, 16 * 1024 * 1024)))


def _nbytes(shape, dtype):
    n = 1
    for s in shape:
        n *= s
    return n * jnp.dtype(dtype).itemsize


def _rmsnorm(x, g):
    ms = jnp.mean(x * x, axis=-1, keepdims=True)
    return x * lax.rsqrt(ms + EPS) * g


def _dot(a, b):
    return jnp.dot(a, b, preferred_element_type=F32)


def _dot_nt(a, b):
    return lax.dot_general(a, b, (((1,), (1,)), ((), ())), preferred_element_type=F32)


def _ffn_body(x_ref, gin_ref, wg_ref, wu_ref, wd_ref, gout_ref, *rest, emit_x, n_side, keep_weights):
    side_in, rest = rest[:n_side], rest[n_side:]
    if emit_x:
        xo_ref, ho_ref = rest[:2]
        rest = rest[2:]
    else:
        ho_ref = rest[0]
        rest = rest[1:]
    side_out, rest = rest[:n_side], rest[n_side:]
    if keep_weights:
        weight_out, rest = rest[:3], rest[3:]
    h_scr, acc_scr = rest
    j = pl.program_id(1)

    @pl.when(j == 0)
    def _():
        h_scr[...] = _rmsnorm(x_ref[...], gin_ref[...]).astype(BF16)
        acc_scr[...] = jnp.zeros_like(acc_scr)

    wg, wu, wd = wg_ref[...], wu_ref[...], wd_ref[...]
    if keep_weights:
        wg, wu, wd = wg.astype(BF16), wu.astype(BF16), wd.astype(BF16)
        for dst, tile in zip(weight_out, (wg, wu, wd)):
            dst[...] = tile

    h = h_scr[...]
    g = _dot(h, wg)
    u = _dot(h, wu)
    act = (g * (1.0 / (1.0 + jnp.exp(-g))) * u).astype(BF16)
    acc_scr[...] += _dot(act, wd)

    for src, dst in zip(side_in, side_out):
        dst[...] = src[...].astype(dst.dtype)

    @pl.when(j == pl.num_programs(1) - 1)
    def _():
        xn = x_ref[...] + FFN_SCALE * acc_scr[...]
        if emit_x:
            xo_ref[...] = xn
        ho_ref[...] = _rmsnorm(xn, gout_ref[...]).astype(ho_ref.dtype)


def _grid_slab(shape, n_i, n_j):
    rows, cols = shape
    for nr, nc, imap in ((n_i, n_j, lambda i, j: (i, j)), (n_j, n_i, lambda i, j: (j, i))):
        if rows % nr == 0 and cols % nc == 0 and (rows // nr) % 16 == 0 and (cols // nc) % 128 == 0:
            return (rows // nr, cols // nc), imap
    return None


def _ffn_grid(n_tokens, d_ff, keep_weights=False):
    tm = min(FFN_TM, n_tokens)
    tf = FFN_TF_F32 if keep_weights else FFN_TF
    assert n_tokens % tm == 0 and d_ff % tf == 0
    return n_tokens // tm, d_ff // tf


def _ffn(x, g_in, w_gu, w_dn, g_out, *, emit_x, h_dtype, side_casts=(), keep_weights=False):
    T, D = x.shape
    d_ff = w_dn.shape[0]
    grid = _ffn_grid(T, d_ff, keep_weights)
    tm, tf = T // grid[0], d_ff // grid[1]
    n_f = grid[1]
    w_gate, w_up = w_gu if isinstance(w_gu, tuple) else (w_gu, w_gu)
    up0 = 0 if isinstance(w_gu, tuple) else n_f
    w_dtype = w_dn.dtype
    assert (w_dtype == F32) == keep_weights and (grid[0] == 1 or not keep_weights)
    in_specs = [
        pl.BlockSpec((tm, D), lambda i, j: (i, 0)),
        pl.BlockSpec((1, D), lambda i, j: (0, 0)),
        pl.BlockSpec((D, tf), lambda i, j: (0, j)),
        pl.BlockSpec((D, tf), lambda i, j: (0, j + up0)),
        pl.BlockSpec((tf, D), lambda i, j: (j, 0)),
        pl.BlockSpec((1, D), lambda i, j: (0, 0)),
    ]
    out_shape = [jax.ShapeDtypeStruct((T, D), h_dtype)]
    out_specs = [pl.BlockSpec((tm, D), lambda i, j: (i, 0))]
    if emit_x:
        out_shape = [jax.ShapeDtypeStruct((T, D), F32)] + out_shape
        out_specs = [pl.BlockSpec((tm, D), lambda i, j: (i, 0))] + out_specs
    side_bytes = 0
    for a in side_casts:
        blk, imap = _grid_slab(a.shape, *grid)
        in_specs.append(pl.BlockSpec(blk, imap))
        out_specs.append(pl.BlockSpec(blk, imap))
        out_shape.append(jax.ShapeDtypeStruct(a.shape, BF16))
        side_bytes += _nbytes(blk, F32) + _nbytes(blk, BF16)
    if keep_weights:
        out_specs += [pl.BlockSpec((D, tf), lambda i, j: (0, j)),
                      pl.BlockSpec((D, tf), lambda i, j: (0, j)),
                      pl.BlockSpec((tf, D), lambda i, j: (j, 0))]
        out_shape += [jax.ShapeDtypeStruct((D, d_ff), BF16), jax.ShapeDtypeStruct((D, d_ff), BF16),
                      jax.ShapeDtypeStruct((d_ff, D), BF16)]
        side_bytes += 3 * _nbytes((D, tf), BF16)
    limit = _vmem_limit(
        _nbytes((tm, D), F32), 3 * _nbytes((D, tf), w_dtype),
        _nbytes((tm, D), h_dtype), _nbytes((tm, D), F32) if emit_x else 0, side_bytes,
        scratch=_nbytes((tm, D), BF16) + _nbytes((tm, D), F32))
    return pl.pallas_call(
        functools.partial(_ffn_body, emit_x=emit_x, n_side=len(side_casts), keep_weights=keep_weights),
        grid=grid, in_specs=in_specs, out_specs=out_specs, out_shape=out_shape,
        scratch_shapes=[pltpu.VMEM((tm, D), BF16), pltpu.VMEM((tm, D), F32)],
        compiler_params=pltpu.CompilerParams(
            dimension_semantics=("arbitrary", "arbitrary"), vmem_limit_bytes=limit),
        name="ffn_mid" if emit_x else "ffn_final",
    )(x, g_in.reshape(1, D), w_gate, w_up, w_dn, g_out.reshape(1, D), *side_casts)


def _sigmoid(x):
    return 1.0 / (1.0 + jnp.exp(-x))


def _mm_body(x_ref, w_ref, *rest, out_scale, has_residual, n_side):
    if has_residual:
        r_ref, rest = rest[0], rest[1:]
    side_in, o_ref, side_out, wb_scr = rest[:n_side], rest[n_side], rest[n_side + 1:-1], rest[-1]

    @pl.when(pl.program_id(1) == 0)
    def _():
        wb_scr[...] = w_ref[...].astype(BF16)

    y = _dot(x_ref[...], wb_scr[...])
    if out_scale is not None:
        y = y * out_scale
    if has_residual:
        y = r_ref[...] + y
    o_ref[...] = y.astype(o_ref.dtype)
    for src, dst in zip(side_in, side_out):
        dst[...] = src[...].astype(dst.dtype)


def _mm_grid(n_tokens, n):
    tm, tn = min(MM_TM, n_tokens), min(MM_TN, n)
    assert n_tokens % tm == 0 and n % tn == 0
    return n // tn, n_tokens // tm


def _matmul(x, w, col0, n, out_dtype, residual=None, out_scale=None, side_casts=()):
    T, K = x.shape
    grid = _mm_grid(T, n)
    tm, tn = T // grid[1], n // grid[0]
    assert col0 % tn == 0
    c0 = col0 // tn
    in_specs = [
        pl.BlockSpec((tm, K), lambda j, i: (i, 0)),
        pl.BlockSpec((K, tn), lambda j, i: (0, j + c0)),
    ]
    args = [x, w]
    blocks = [_nbytes((tm, K), x.dtype), _nbytes((K, tn), w.dtype), _nbytes((tm, tn), out_dtype)]
    if residual is not None:
        in_specs.append(pl.BlockSpec((tm, tn), lambda j, i: (i, j)))
        args.append(residual)
        blocks.append(_nbytes((tm, tn), F32))
    out_specs = [pl.BlockSpec((tm, tn), lambda j, i: (i, j))]
    out_shape = [jax.ShapeDtypeStruct((T, n), out_dtype)]
    for a in side_casts:
        blk, imap = _grid_slab(a.shape, *grid)
        in_specs.append(pl.BlockSpec(blk, imap))
        args.append(a)
        out_specs.append(pl.BlockSpec(blk, imap))
        out_shape.append(jax.ShapeDtypeStruct(a.shape, BF16))
        blocks.append(_nbytes(blk, F32) + _nbytes(blk, BF16))
    outs = pl.pallas_call(
        functools.partial(_mm_body, out_scale=out_scale, has_residual=residual is not None,
                          n_side=len(side_casts)),
        grid=grid, in_specs=in_specs, out_specs=out_specs, out_shape=out_shape,
        scratch_shapes=[pltpu.VMEM((K, tn), BF16)],
        compiler_params=pltpu.CompilerParams(
            dimension_semantics=("arbitrary", "arbitrary"),
            vmem_limit_bytes=_vmem_limit(*blocks, scratch=_nbytes((K, tn), BF16))),
        name="proj",
    )(*args)
    return outs if side_casts else outs[0]


def _tri(bk):
    j = lax.broadcasted_iota(jnp.int32, (bk, bk), 0)
    s = lax.broadcasted_iota(jnp.int32, (bk, bk), 1)
    return (j >= s).astype(BF16)


LOG2E = 1.4426950408889634
Q_SCALE = HEAD_DIM ** -0.5 * LOG2E


def _softplus2(z2):
    return jnp.maximum(z2, 0.0) + jnp.log(1.0 + jnp.exp2(-jnp.abs(z2))) * LOG2E


def _cumsum_mxu(sp, tri):
    hi = sp.astype(BF16)
    lo = (sp - hi.astype(F32)).astype(BF16)
    if sp.shape[1] % 128 == 0:
        return _dot(jnp.concatenate([hi, lo], axis=1), jnp.concatenate([tri, tri], axis=0))
    return _dot(hi, tri) + _dot(lo, tri)


def _sb_weights(z2, tri, run, masked):
    sp = _softplus2(z2)
    if masked is not None:
        sp = masked(sp)
    tot = _cumsum_mxu(sp, tri)
    if run is not None:
        tot = tot + jnp.concatenate([run] * (z2.shape[1] // run.shape[1]), axis=1)
    a = jnp.exp2(z2 - tot)
    if masked is not None:
        a = masked(a)
    return a.astype(BF16), jnp.sum(sp, axis=1, keepdims=True)


def _attn_prompt_body(q_ref, k_ref, v_ref, tri_ref, o_ref, run_scr, acc_scr):
    blk = ATT_BLK
    seq = q_ref.shape[0]
    nb = seq // blk
    tri = tri_ref[...]
    heads = [slice(h * HEAD_DIM, (h + 1) * HEAD_DIM) for h in range(ATT_HEADS_PER_STEP)]
    block = lambda b: slice(b * blk, (b + 1) * blk)

    def key_block(ref, b, cols):
        return ref[block(b), cols].astype(BF16)

    def stacked_scores(cols, shift):
        return jnp.concatenate([_dot_nt(q_ref[block(b), cols], key_block(k_ref, b - shift, cols))
                                for b in range(shift, nb)], axis=0)

    def stacked_values(a, cols, shift):
        return jnp.concatenate([_dot(a[block(b - shift), :], key_block(v_ref, b - shift, cols))
                                for b in range(shift, nb)], axis=0)

    t = lax.broadcasted_iota(jnp.int32, (seq, blk), 0) % blk
    s = lax.broadcasted_iota(jnp.int32, (seq, blk), 1)
    causal = s < t
    run, acc = [], []
    for cols in heads:
        a, tot = _sb_weights(stacked_scores(cols, 0), tri, None, lambda x: jnp.where(causal, x, 0.0))
        run.append(jnp.broadcast_to(tot, (seq, HEAD_DIM)))
        acc.append(stacked_values(a, cols, 0))
    if nb > 1:
        for i, cols in enumerate(heads):
            a, tot = _sb_weights(stacked_scores(cols, 1), tri, run[i][blk:], None)
            run[i] = jnp.concatenate([run[i][:blk], run[i][blk:] + tot], axis=0)
            acc[i] = jnp.concatenate([acc[i][:blk], acc[i][blk:] + stacked_values(a, cols, 1)], axis=0)
    for i, cols in enumerate(heads):
        o_ref[:, cols] = acc[i].astype(o_ref.dtype)
    if nb <= 2:
        return

    far = 2 * blk
    alive = None
    for i in range(len(heads)):
        run_scr[i] = run[i][far:]
        acc_scr[i] = acc[i][far:]
        lowest = jnp.min(run[i][far:])
        alive = lowest if alive is None else jnp.minimum(alive, lowest)

    @pl.when(alive <= ATT_DEAD_LOG2)
    def _():
        run = [run_scr[i] for i in range(len(heads))]
        acc = [acc_scr[i] for i in range(len(heads))]
        for c in range(nb - 3, -1, -1):
            top = c * blk
            for i, cols in enumerate(heads):
                z2 = _dot_nt(q_ref[far + top:, cols], key_block(k_ref, c, cols))
                a, tot = _sb_weights(z2, tri, run[i][top:], None)
                pv = _dot(a, key_block(v_ref, c, cols))
                run[i] = jnp.concatenate([run[i][:top], run[i][top:] + tot], axis=0) if top else run[i] + tot
                acc[i] = jnp.concatenate([acc[i][:top], acc[i][top:] + pv], axis=0) if top else acc[i] + pv
        for i, cols in enumerate(heads):
            o_ref[far:, cols] = acc[i].astype(o_ref.dtype)


def _attn_prompt(q, k, v, batch, seq):
    blk = ATT_BLK
    assert seq % blk == 0 and N_HEADS % ATT_HEADS_PER_STEP == 0
    hd = HEAD_DIM
    width = ATT_HEADS_PER_STEP * hd
    spec = pl.BlockSpec((seq, width), lambda b, h: (b, h))
    return pl.pallas_call(
        _attn_prompt_body, grid=(batch, N_HEADS // ATT_HEADS_PER_STEP),
        in_specs=[spec, spec, spec, pl.BlockSpec((blk, blk), lambda b, h: (0, 0))],
        out_specs=spec,
        out_shape=jax.ShapeDtypeStruct(q.shape, BF16),
        scratch_shapes=[pltpu.VMEM((ATT_HEADS_PER_STEP, max(seq - 2 * blk, 8), hd), F32)] * 2,
        compiler_params=pltpu.CompilerParams(
            dimension_semantics=("parallel", "parallel"),
            vmem_limit_bytes=_vmem_limit(2 * _nbytes((seq, width), F32), 2 * _nbytes((seq, width), BF16),
                                         scratch=8 * ATT_HEADS_PER_STEP * _nbytes((seq, blk), F32))),
        name="attn_prompt",
    )(q, k, v, _tri(blk))


def _attn_sample_body(q_ref, kn_ref, vn_ref, kc_ref, vc_ref, tri_ref, trin_ref, o_ref,
                      run_scr, acc_scr, *, n_new):
    step = pl.program_id(1)
    rows = N_HEADS * n_new
    bk = ATT_BLK
    head_cols = [slice(h * HEAD_DIM, (h + 1) * HEAD_DIM) for h in range(N_HEADS)]
    head_rows = [slice(h * n_new, (h + 1) * n_new) for h in range(N_HEADS)]

    def scores(load_k):
        parts = [_dot_nt(q_ref[:, head_cols[h]], load_k(h)) for h in range(N_HEADS)]
        return jnp.concatenate(parts, axis=0)

    def weighted_values(a, load_v):
        return jnp.concatenate([_dot(a[head_rows[h], :], load_v(h)) for h in range(N_HEADS)], axis=0)

    @pl.when(step == 0)
    def _():
        z = scores(lambda h: kn_ref[:, head_cols[h]].astype(BF16))
        t = lax.broadcasted_iota(jnp.int32, (rows, n_new), 0) % n_new
        s = lax.broadcasted_iota(jnp.int32, (rows, n_new), 1)
        a, tot = _sb_weights(z, trin_ref[...], None, lambda x: jnp.where(s < t, x, 0.0))
        acc_scr[...] = weighted_values(a, lambda h: vn_ref[:, head_cols[h]].astype(BF16))
        run_scr[...] = jnp.broadcast_to(tot, run_scr.shape)

    @pl.when(step > 0)
    def _():
        run = run_scr[...]
        acc = acc_scr[...]
        for sb in range(kc_ref.shape[0] // (bk * N_HEADS) - 1, -1, -1):
            def head_rows_of(ref, h):
                return ref[pl.ds(sb * bk * N_HEADS + h, bk, stride=N_HEADS), :].astype(BF16)
            z = scores(lambda h: head_rows_of(kc_ref, h))
            a, tot = _sb_weights(z, tri_ref[...], run, None)
            acc = acc + weighted_values(a, lambda h: head_rows_of(vc_ref, h))
            run = run + tot
        acc_scr[...] = acc
        run_scr[...] = run

    @pl.when(step == pl.num_programs(1) - 1)
    def _():
        for h in range(N_HEADS):
            o_ref[:, head_cols[h]] = acc_scr[head_rows[h], :].astype(o_ref.dtype)


def _attn_sample(q, k_new, v_new, cache_k, cache_v, n_streams, n_new, past_len):
    bk = ATT_BLK
    keys_per_step = min(SAMPLE_KEYS_PER_STEP, past_len)
    assert past_len % keys_per_step == 0 and keys_per_step % bk == 0
    nkb = past_len // keys_per_step
    d_all = N_HEADS * HEAD_DIM
    rows = N_HEADS * n_new

    def cache_map(b, s):
        return (b * nkb + nkb - jnp.maximum(s, 1), 0)

    new_spec = pl.BlockSpec((n_new, d_all), lambda b, s: (b, 0))
    cache_spec = pl.BlockSpec((keys_per_step * N_HEADS, HEAD_DIM), cache_map)
    return pl.pallas_call(
        functools.partial(_attn_sample_body, n_new=n_new),
        grid=(n_streams, nkb + 1),
        in_specs=[new_spec, new_spec, new_spec, cache_spec, cache_spec,
                  pl.BlockSpec((bk, bk), lambda b, s: (0, 0)),
                  pl.BlockSpec((n_new, n_new), lambda b, s: (0, 0))],
        out_specs=new_spec,
        out_shape=jax.ShapeDtypeStruct(q.shape, BF16),
        scratch_shapes=[pltpu.VMEM((rows, HEAD_DIM), F32), pltpu.VMEM((rows, HEAD_DIM), F32)],
        compiler_params=pltpu.CompilerParams(
            dimension_semantics=("parallel", "arbitrary"),
            vmem_limit_bytes=_vmem_limit(2 * _nbytes((keys_per_step * N_HEADS, HEAD_DIM), F32))),
        name="attn_sample",
    )(q, k_new, v_new, cache_k, cache_v, _tri(bk), _tri(n_new))


def _merge(cb, conv, o, ga, gb, wco_ref, wao_ref, m_ref):
    y_a = _dot((cb * conv).astype(BF16), wco_ref[...])
    y_b = _dot(o, wao_ref[...])
    m_ref[...] = (_sigmoid(ga) * y_a + _sigmoid(gb) * y_b).astype(m_ref.dtype)


def _mix_prompt_body(cb_ref, cc_ref, cx_ref, o_ref, ga_ref, gb_ref, cw_ref, wco_ref, wao_ref,
                     m_ref, hist_ref, carry_scr, *, tiles_per_seq):
    i = pl.program_id(0)
    tm = cc_ref.shape[0]

    @pl.when(i % tiles_per_seq == 0)
    def _():
        carry_scr[...] = jnp.zeros_like(carry_scr)

    u = cc_ref[...] * cx_ref[...]
    row = lax.broadcasted_iota(jnp.int32, u.shape, 0)
    p1 = carry_scr[7:8, :]
    p2 = carry_scr[6:7, :]
    u1 = jnp.where(row == 0, p1, pltpu.roll(u, 1, axis=0))
    u2 = jnp.where(row == 0, p2, jnp.where(row == 1, p1, pltpu.roll(u, 2, axis=0)))
    conv = u2 * cw_ref[0:1, :] + u1 * cw_ref[1:2, :] + u * cw_ref[2:3, :]
    carry_scr[...] = u[tm - 8:, :]
    hist_ref[...] = u[tm - (CONV_W - 1):, :]
    _merge(cb_ref[...], conv, o_ref[...], ga_ref[...], gb_ref[...], wco_ref, wao_ref, m_ref)


def _mix_sample_body(cb_ref, cc_ref, cx_ref, o_ref, ga_ref, gb_ref, cw_ref, wco_ref, wao_ref,
                     p1_ref, p2_ref, m_ref, u_ref, *, n_new):
    u = cc_ref[...] * cx_ref[...]
    t = lax.broadcasted_iota(jnp.int32, u.shape, 0) % n_new
    u1 = jnp.where(t == 0, p1_ref[...], pltpu.roll(u, 1, axis=0))
    u2 = jnp.where(t < 2, p2_ref[...], pltpu.roll(u, 2, axis=0))
    conv = u2 * cw_ref[0:1, :] + u1 * cw_ref[1:2, :] + u * cw_ref[2:3, :]
    u_ref[...] = u
    _merge(cb_ref[...], conv, o_ref[...], ga_ref[...], gb_ref[...], wco_ref, wao_ref, m_ref)


def _mix_prompt(pconv, o, gates, conv_w, w_co, w_ao, batch, seq):
    T = pconv.shape[0]
    C = pconv.shape[1] // 3
    D = gates.shape[1] // 2
    Da = o.shape[1]
    tm = MIX_TM
    assert seq % tm == 0
    tps = seq // tm
    row = lambda c: (lambda i: (i, c))
    const = lambda i: (0, 0)
    m, hist = pl.pallas_call(
        functools.partial(_mix_prompt_body, tiles_per_seq=tps),
        grid=(T // tm,),
        in_specs=[
            pl.BlockSpec((tm, C), row(0)), pl.BlockSpec((tm, C), row(1)), pl.BlockSpec((tm, C), row(2)),
            pl.BlockSpec((tm, Da), row(0)),
            pl.BlockSpec((tm, D), row(0)), pl.BlockSpec((tm, D), row(1)),
            pl.BlockSpec((CONV_W, C), const),
            pl.BlockSpec((C, D), const, pipeline_mode=pl.Buffered(1)),
            pl.BlockSpec((Da, D), const, pipeline_mode=pl.Buffered(1)),
        ],
        out_specs=[pl.BlockSpec((tm, D), row(0)),
                   pl.BlockSpec((None, CONV_W - 1, C), lambda i: (i // tps, 0, 0))],
        out_shape=[jax.ShapeDtypeStruct((T, D), BF16),
                   jax.ShapeDtypeStruct((batch, CONV_W - 1, C), F32)],
        scratch_shapes=[pltpu.VMEM((8, C), F32)],
        compiler_params=pltpu.CompilerParams(
            dimension_semantics=("arbitrary",),
            vmem_limit_bytes=_vmem_limit(3 * _nbytes((tm, C), F32), _nbytes((tm, Da), BF16),
                                         2 * _nbytes((tm, D), gates.dtype), _nbytes((C, D), BF16),
                                         _nbytes((Da, D), BF16), _nbytes((tm, D), BF16))),
        name="mix_prompt",
    )(pconv, pconv, pconv, o, gates, gates, conv_w, w_co, w_ao)
    return m, hist


def _mix_sample(pconv, o, gates, conv_w, w_co, w_ao, state, n_streams, n_new):
    T = pconv.shape[0]
    C = pconv.shape[1] // 3
    D = gates.shape[1] // 2
    Da = o.shape[1]
    assert T == n_streams * n_new and n_new >= CONV_W - 1
    zeros = jnp.zeros((n_streams, n_new, C), F32)
    p1 = zeros.at[:, 0].set(state[:, 1]).reshape(T, C)
    p2 = zeros.at[:, 0].set(state[:, 0]).at[:, 1].set(state[:, 1]).reshape(T, C)
    col = lambda c: (lambda i: (0, c))
    const = lambda i: (0, 0)
    m, u = pl.pallas_call(
        functools.partial(_mix_sample_body, n_new=n_new),
        grid=(1,),
        in_specs=[
            pl.BlockSpec((T, C), col(0)), pl.BlockSpec((T, C), col(1)), pl.BlockSpec((T, C), col(2)),
            pl.BlockSpec((T, Da), const),
            pl.BlockSpec((T, D), col(0)), pl.BlockSpec((T, D), col(1)),
            pl.BlockSpec((CONV_W, C), const),
            pl.BlockSpec((C, D), const), pl.BlockSpec((Da, D), const),
            pl.BlockSpec((T, C), const), pl.BlockSpec((T, C), const),
        ],
        out_specs=[pl.BlockSpec((T, D), const), pl.BlockSpec((T, C), const)],
        out_shape=[jax.ShapeDtypeStruct((T, D), BF16), jax.ShapeDtypeStruct((T, C), F32)],
        compiler_params=pltpu.CompilerParams(
            dimension_semantics=("arbitrary",),
            vmem_limit_bytes=_vmem_limit(6 * _nbytes((T, C), F32), _nbytes((T, Da), BF16),
                                         2 * _nbytes((T, D), gates.dtype), _nbytes((C, D), BF16),
                                         _nbytes((Da, D), BF16), _nbytes((T, D), BF16))),
        name="mix_sample",
    )(pconv, pconv, pconv, o, gates, gates, conv_w, w_co, w_ao, p1, p2)
    new_hist = u.reshape(n_streams, n_new, C)[:, n_new - (CONV_W - 1):]
    return m, new_hist


def _mixer(x1, h, w, *, batch, seq, sample_state=None, merge_weights=None):
    C = w["conv_w"].shape[1]
    Da = N_HEADS * HEAD_DIM
    T, D = x1.shape
    pconv = _matmul(h, w["w_in"], 0, 3 * C, F32)
    q = _matmul(h, w["w_in"], 3 * C, Da, BF16, out_scale=Q_SCALE)
    k = _matmul(h, w["w_in"], 3 * C + Da, Da, F32)
    v = _matmul(h, w["w_in"], 3 * C + 2 * Da, Da, F32)
    merge_f32 = (w["w_co"], w["w_ao"])
    if merge_weights is None and all(_grid_slab(a.shape, *_mm_grid(T, 2 * D)) for a in merge_f32):
        gates, *merge_weights = _matmul(h, w["w_in"], 3 * C + 3 * Da, 2 * D, F32, side_casts=merge_f32)
    else:
        gates = _matmul(h, w["w_in"], 3 * C + 3 * Da, 2 * D, F32)
    if merge_weights is None:
        merge_weights = [a.astype(BF16) for a in merge_f32]
    w_co, w_ao = merge_weights
    if sample_state is None:
        o = _attn_prompt(q, k, v, batch, seq)
        m, hist = _mix_prompt(pconv, o, gates, w["conv_w"], w_co, w_ao, batch, seq)
    else:
        conv_state, cache_k, cache_v, past_len = sample_state
        o = _attn_sample(q, k, v, cache_k.reshape(batch * past_len * N_HEADS, HEAD_DIM),
                         cache_v.reshape(batch * past_len * N_HEADS, HEAD_DIM), batch, seq, past_len)
        m, hist = _mix_sample(pconv, o, gates, w["conv_w"], w_co, w_ao, conv_state, batch, seq)
    x2 = _matmul(m, w["w_o"], 0, D, F32, residual=x1)
    return x2, hist, k, v, merge_weights


def kernel(x_prompt, x_sample, cache_k, cache_v, state_conv, norm_ffn1, ffn1_w_gate_up, ffn1_w_down,
           norm_mix, w_in, conv_w, w_conv_out, w_attn_out, w_o, norm_ffn2, ffn2_w_gate_up,
           ffn2_w_down, norm_final):
    depth = w_in.shape[0]
    B, T, D = x_prompt.shape
    S, n_new, _ = x_sample.shape
    past_len = cache_k.shape[2]
    xp = x_prompt.reshape(B * T, D)
    xs = x_sample.reshape(S * n_new, D)
    outs = [[] for _ in range(6)]
    for l in range(depth):
        w = {"w_in": w_in[l], "conv_w": conv_w[l], "w_co": w_conv_out[l], "w_ao": w_attn_out[l],
             "w_o": w_o[l]}
        ffn1 = dict(emit_x=True, h_dtype=BF16)
        d_ff = ffn1_w_down.shape[1]
        if _ffn_grid(S * n_new, d_ff)[0] == 1:
            xs1, hs, w_gate1, w_up1, w_dn1 = _ffn(xs, norm_ffn1[l], ffn1_w_gate_up[l], ffn1_w_down[l],
                                                  norm_mix[l], keep_weights=True, **ffn1)
            w_gu1 = (w_gate1, w_up1)
        else:
            w_gu1, w_dn1 = ffn1_w_gate_up[l].astype(BF16), ffn1_w_down[l].astype(BF16)
            xs1, hs = _ffn(xs, norm_ffn1[l], w_gu1, w_dn1, norm_mix[l], **ffn1)
        ffn2_f32 = (ffn2_w_gate_up[l], ffn2_w_down[l])
        fuse = all(_grid_slab(a.shape, *_ffn_grid(B * T, d_ff)) for a in ffn2_f32)
        xp1, hp, *ffn2_bf = _ffn(xp, norm_ffn1[l], w_gu1, w_dn1, norm_mix[l],
                                 side_casts=ffn2_f32 if fuse else (), **ffn1)
        if not fuse:
            ffn2_bf = [a.astype(BF16) for a in ffn2_f32]
        xp2, c_p, k_p, v_p, merge_w = _mixer(xp1, hp, w, batch=B, seq=T)
        xs2, c_s, k_s, v_s, _ = _mixer(xs1, hs, w, batch=S, seq=n_new, merge_weights=merge_w,
                                       sample_state=(state_conv[l], cache_k[l], cache_v[l], past_len))
        w2 = (norm_ffn2[l], *ffn2_bf, norm_final)
        if l == depth - 1:
            xp = _ffn(xp2, *w2, emit_x=False, h_dtype=F32)[0]
            xs = _ffn(xs2, *w2, emit_x=False, h_dtype=F32)[0]
        else:
            xp = _ffn(xp2, *w2, emit_x=True, h_dtype=BF16)[0]
            xs = _ffn(xs2, *w2, emit_x=True, h_dtype=BF16)[0]
        for lst, val in zip(outs, (k_p.reshape(B, T, N_HEADS, HEAD_DIM), v_p.reshape(B, T, N_HEADS, HEAD_DIM),
                                   c_p, k_s.reshape(S, n_new, N_HEADS, HEAD_DIM),
                                   v_s.reshape(S, n_new, N_HEADS, HEAD_DIM), c_s)):
            lst.append(val)
    y_prompt = xp.reshape(B, T, D)
    y_sample = xs.reshape(S, n_new, D)
    return (y_prompt, y_sample) + tuple(jnp.stack(o, axis=0) for o in outs)
```

```python
import functools

import jax
import jax.numpy as jnp
from jax import lax
from jax.experimental import pallas as pl
from jax.experimental.pallas import tpu as pltpu

N_HEADS = 8
HEAD_DIM = 128
CONV_W = 3
EPS = 1e-6
FFN_SCALE = 0.5

F32 = jnp.float32
BF16 = jnp.bfloat16

V7X_VMEM_LIMIT_CAP = 60 * 1024 * 1024

FFN_TM = 512
FFN_TF = 512
FFN_TF_F32 = 256
MM_TM = 1024
MM_TN = 1024
MIX_TM = 512
ATT_BLK = 256
ATT_HEADS_PER_STEP = 2
ATT_DEAD_LOG2 = 160.0


def _vmem_limit(*block_bytes, scratch=0):
    est = 2 * sum(block_bytes) + scratch
    return int(min(V7X_VMEM_LIMIT_CAP, max(2 * est, 16 * 1024 * 1024)))


def _nbytes(shape, dtype):
    n = 1
    for s in shape:
        n *= s
    return n * jnp.dtype(dtype).itemsize


def _rmsnorm(x, g):
    ms = jnp.mean(x * x, axis=-1, keepdims=True)
    return x * lax.rsqrt(ms + EPS) * g


def _dot(a, b):
    return jnp.dot(a, b, preferred_element_type=F32)


def _dot_nt(a, b):
    return lax.dot_general(a, b, (((1,), (1,)), ((), ())), preferred_element_type=F32)


def _ffn_body(x_ref, gin_ref, wg_ref, wu_ref, wd_ref, gout_ref, *rest, emit_x, n_side, keep_weights):
    side_in, rest = rest[:n_side], rest[n_side:]
    if emit_x:
        xo_ref, ho_ref = rest[:2]
        rest = rest[2:]
    else:
        ho_ref = rest[0]
        rest = rest[1:]
    side_out, rest = rest[:n_side], rest[n_side:]
    if keep_weights:
        weight_out, rest = rest[:3], rest[3:]
    h_scr, acc_scr = rest
    j = pl.program_id(1)

    @pl.when(j == 0)
    def _():
        h_scr[...] = _rmsnorm(x_ref[...], gin_ref[...]).astype(BF16)
        acc_scr[...] = jnp.zeros_like(acc_scr)

    wg, wu, wd = wg_ref[...], wu_ref[...], wd_ref[...]
    if keep_weights:
        wg, wu, wd = wg.astype(BF16), wu.astype(BF16), wd.astype(BF16)
        for dst, tile in zip(weight_out, (wg, wu, wd)):
            dst[...] = tile

    h = h_scr[...]
    g = _dot(h, wg)
    u = _dot(h, wu)
    act = (g * (1.0 / (1.0 + jnp.exp(-g))) * u).astype(BF16)
    acc_scr[...] += _dot(act, wd)

    for src, dst in zip(side_in, side_out):
        dst[...] = src[...].astype(dst.dtype)

    @pl.when(j == pl.num_programs(1) - 1)
    def _():
        xn = x_ref[...] + FFN_SCALE * acc_scr[...]
        if emit_x:
            xo_ref[...] = xn
        ho_ref[...] = _rmsnorm(xn, gout_ref[...]).astype(ho_ref.dtype)


def _grid_slab(shape, n_i, n_j):
    rows, cols = shape
    for nr, nc, imap in ((n_i, n_j, lambda i, j: (i, j)), (n_j, n_i, lambda i, j: (j, i))):
        if rows % nr == 0 and cols % nc == 0 and (rows // nr) % 16 == 0 and (cols // nc) % 128 == 0:
            return (rows // nr, cols // nc), imap
    return None


def _ffn_grid(n_tokens, d_ff, keep_weights=False):
    tm = min(FFN_TM, n_tokens)
    tf = FFN_TF_F32 if keep_weights else FFN_TF
    assert n_tokens % tm == 0 and d_ff % tf == 0
    return n_tokens // tm, d_ff // tf


def _ffn(x, g_in, w_gu, w_dn, g_out, *, emit_x, h_dtype, side_casts=(), keep_weights=False):
    T, D = x.shape
    d_ff = w_dn.shape[0]
    grid = _ffn_grid(T, d_ff, keep_weights)
    tm, tf = T // grid[0], d_ff // grid[1]
    n_f = grid[1]
    w_gate, w_up = w_gu if isinstance(w_gu, tuple) else (w_gu, w_gu)
    up0 = 0 if isinstance(w_gu, tuple) else n_f
    w_dtype = w_dn.dtype
    assert (w_dtype == F32) == keep_weights and (grid[0] == 1 or not keep_weights)
    in_specs = [
        pl.BlockSpec((tm, D), lambda i, j: (i, 0)),
        pl.BlockSpec((1, D), lambda i, j: (0, 0)),
        pl.BlockSpec((D, tf), lambda i, j: (0, j)),
        pl.BlockSpec((D, tf), lambda i, j: (0, j + up0)),
        pl.BlockSpec((tf, D), lambda i, j: (j, 0)),
        pl.BlockSpec((1, D), lambda i, j: (0, 0)),
    ]
    out_shape = [jax.ShapeDtypeStruct((T, D), h_dtype)]
    out_specs = [pl.BlockSpec((tm, D), lambda i, j: (i, 0))]
    if emit_x:
        out_shape = [jax.ShapeDtypeStruct((T, D), F32)] + out_shape
        out_specs = [pl.BlockSpec((tm, D), lambda i, j: (i, 0))] + out_specs
    side_bytes = 0
    for a in side_casts:
        blk, imap = _grid_slab(a.shape, *grid)
        in_specs.append(pl.BlockSpec(blk, imap))
        out_specs.append(pl.BlockSpec(blk, imap))
        out_shape.append(jax.ShapeDtypeStruct(a.shape, BF16))
        side_bytes += _nbytes(blk, F32) + _nbytes(blk, BF16)
    if keep_weights:
        out_specs += [pl.BlockSpec((D, tf), lambda i, j: (0, j)),
                      pl.BlockSpec((D, tf), lambda i, j: (0, j)),
                      pl.BlockSpec((tf, D), lambda i, j: (j, 0))]
        out_shape += [jax.ShapeDtypeStruct((D, d_ff), BF16), jax.ShapeDtypeStruct((D, d_ff), BF16),
                      jax.ShapeDtypeStruct((d_ff, D), BF16)]
        side_bytes += 3 * _nbytes((D, tf), BF16)
    limit = _vmem_limit(
        _nbytes((tm, D), F32), 3 * _nbytes((D, tf), w_dtype),
        _nbytes((tm, D), h_dtype), _nbytes((tm, D), F32) if emit_x else 0, side_bytes,
        scratch=_nbytes((tm, D), BF16) + _nbytes((tm, D), F32))
    return pl.pallas_call(
        functools.partial(_ffn_body, emit_x=emit_x, n_side=len(side_casts), keep_weights=keep_weights),
        grid=grid, in_specs=in_specs, out_specs=out_specs, out_shape=out_shape,
        scratch_shapes=[pltpu.VMEM((tm, D), BF16), pltpu.VMEM((tm, D), F32)],
        compiler_params=pltpu.CompilerParams(
            dimension_semantics=("arbitrary", "arbitrary"), vmem_limit_bytes=limit),
        name="ffn_mid" if emit_x else "ffn_final",
    )(x, g_in.reshape(1, D), w_gate, w_up, w_dn, g_out.reshape(1, D), *side_casts)


def _sigmoid(x):
    return 1.0 / (1.0 + jnp.exp(-x))


def _mm_body(x_ref, w_ref, *rest, out_scale, has_residual, n_side):
    if has_residual:
        r_ref, rest = rest[0], rest[1:]
    side_in, o_ref, side_out, wb_scr = rest[:n_side], rest[n_side], rest[n_side + 1:-1], rest[-1]

    @pl.when(pl.program_id(1) == 0)
    def _():
        wb_scr[...] = w_ref[...].astype(BF16)

    y = _dot(x_ref[...], wb_scr[...])
    if out_scale is not None:
        y = y * out_scale
    if has_residual:
        y = r_ref[...] + y
    o_ref[...] = y.astype(o_ref.dtype)
    for src, dst in zip(side_in, side_out):
        dst[...] = src[...].astype(dst.dtype)


def _mm_grid(n_tokens, n):
    tm, tn = min(MM_TM, n_tokens), min(MM_TN, n)
    assert n_tokens % tm == 0 and n % tn == 0
    return n // tn, n_tokens // tm


def _matmul(x, w, col0, n, out_dtype, residual=None, out_scale=None, side_casts=()):
    T, K = x.shape
    grid = _mm_grid(T, n)
    tm, tn = T // grid[1], n // grid[0]
    assert col0 % tn == 0
    c0 = col0 // tn
    in_specs = [
        pl.BlockSpec((tm, K), lambda j, i: (i, 0)),
        pl.BlockSpec((K, tn), lambda j, i: (0, j + c0)),
    ]
    args = [x, w]
    blocks = [_nbytes((tm, K), x.dtype), _nbytes((K, tn), w.dtype), _nbytes((tm, tn), out_dtype)]
    if residual is not None:
        in_specs.append(pl.BlockSpec((tm, tn), lambda j, i: (i, j)))
        args.append(residual)
        blocks.append(_nbytes((tm, tn), F32))
    out_specs = [pl.BlockSpec((tm, tn), lambda j, i: (i, j))]
    out_shape = [jax.ShapeDtypeStruct((T, n), out_dtype)]
    for a in side_casts:
        blk, imap = _grid_slab(a.shape, *grid)
        in_specs.append(pl.BlockSpec(blk, imap))
        args.append(a)
        out_specs.append(pl.BlockSpec(blk, imap))
        out_shape.append(jax.ShapeDtypeStruct(a.shape, BF16))
        blocks.append(_nbytes(blk, F32) + _nbytes(blk, BF16))
    outs = pl.pallas_call(
        functools.partial(_mm_body, out_scale=out_scale, has_residual=residual is not None,
                          n_side=len(side_casts)),
        grid=grid, in_specs=in_specs, out_specs=out_specs, out_shape=out_shape,
        scratch_shapes=[pltpu.VMEM((K, tn), BF16)],
        compiler_params=pltpu.CompilerParams(
            dimension_semantics=("arbitrary", "arbitrary"),
            vmem_limit_bytes=_vmem_limit(*blocks, scratch=_nbytes((K, tn), BF16))),
        name="proj",
    )(*args)
    return outs if side_casts else outs[0]


def _tri(bk):
    j = lax.broadcasted_iota(jnp.int32, (bk, bk), 0)
    s = lax.broadcasted_iota(jnp.int32, (bk, bk), 1)
    return (j >= s).astype(BF16)


LOG2E = 1.4426950408889634
Q_SCALE = HEAD_DIM ** -0.5 * LOG2E


def _softplus2(z2):
    return jnp.maximum(z2, 0.0) + jnp.log(1.0 + jnp.exp2(-jnp.abs(z2))) * LOG2E


def _cumsum_mxu(sp, tri):
    hi = sp.astype(BF16)
    lo = (sp - hi.astype(F32)).astype(BF16)
    if sp.shape[1] % 128 == 0:
        return _dot(jnp.concatenate([hi, lo], axis=1), jnp.concatenate([tri, tri], axis=0))
    return _dot(hi, tri) + _dot(lo, tri)


def _sb_weights(z2, tri, run, masked):
    sp = _softplus2(z2)
    if masked is not None:
        sp = masked(sp)
    tot = _cumsum_mxu(sp, tri)
    if run is not None:
        tot = tot + jnp.concatenate([run] * (z2.shape[1] // run.shape[1]), axis=1)
    a = jnp.exp2(z2 - tot)
    if masked is not None:
        a = masked(a)
    return a.astype(BF16), jnp.sum(sp, axis=1, keepdims=True)


def _attn_prompt_body(q_ref, k_ref, v_ref, tri_ref, o_ref, run_scr, acc_scr):
    blk = ATT_BLK
    seq = q_ref.shape[0]
    nb = seq // blk
    tri = tri_ref[...]
    heads = [slice(h * HEAD_DIM, (h + 1) * HEAD_DIM) for h in range(ATT_HEADS_PER_STEP)]
    block = lambda b: slice(b * blk, (b + 1) * blk)

    def key_block(ref, b, cols):
        return ref[block(b), cols].astype(BF16)

    def stacked_scores(cols, shift):
        return jnp.concatenate([_dot_nt(q_ref[block(b), cols], key_block(k_ref, b - shift, cols))
                                for b in range(shift, nb)], axis=0)

    def stacked_values(a, cols, shift):
        return jnp.concatenate([_dot(a[block(b - shift), :], key_block(v_ref, b - shift, cols))
                                for b in range(shift, nb)], axis=0)

    t = lax.broadcasted_iota(jnp.int32, (seq, blk), 0) % blk
    s = lax.broadcasted_iota(jnp.int32, (seq, blk), 1)
    causal = s < t
    run, acc = [], []
    for cols in heads:
        a, tot = _sb_weights(stacked_scores(cols, 0), tri, None, lambda x: jnp.where(causal, x, 0.0))
        run.append(jnp.broadcast_to(tot, (seq, HEAD_DIM)))
        acc.append(stacked_values(a, cols, 0))
    if nb > 1:
        for i, cols in enumerate(heads):
            a, tot = _sb_weights(stacked_scores(cols, 1), tri, run[i][blk:], None)
            run[i] = jnp.concatenate([run[i][:blk], run[i][blk:] + tot], axis=0)
            acc[i] = jnp.concatenate([acc[i][:blk], acc[i][blk:] + stacked_values(a, cols, 1)], axis=0)
    for i, cols in enumerate(heads):
        o_ref[:, cols] = acc[i].astype(o_ref.dtype)
    if nb <= 2:
        return

    far = 2 * blk
    alive = None
    for i in range(len(heads)):
        run_scr[i] = run[i][far:]
        acc_scr[i] = acc[i][far:]
        lowest = jnp.min(run[i][far:])
        alive = lowest if alive is None else jnp.minimum(alive, lowest)

    @pl.when(alive <= ATT_DEAD_LOG2)
    def _():
        run = [run_scr[i] for i in range(len(heads))]
        acc = [acc_scr[i] for i in range(len(heads))]
        for c in range(nb - 3, -1, -1):
            top = c * blk
            for i, cols in enumerate(heads):
                z2 = _dot_nt(q_ref[far + top:, cols], key_block(k_ref, c, cols))
                a, tot = _sb_weights(z2, tri, run[i][top:], None)
                pv = _dot(a, key_block(v_ref, c, cols))
                run[i] = jnp.concatenate([run[i][:top], run[i][top:] + tot], axis=0) if top else run[i] + tot
                acc[i] = jnp.concatenate([acc[i][:top], acc[i][top:] + pv], axis=0) if top else acc[i] + pv
        for i, cols in enumerate(heads):
            o_ref[far:, cols] = acc[i].astype(o_ref.dtype)


def _attn_prompt(q, k, v, batch, seq):
    blk = ATT_BLK
    assert seq % blk == 0 and N_HEADS % ATT_HEADS_PER_STEP == 0
    hd = HEAD_DIM
    width = ATT_HEADS_PER_STEP * hd
    spec = pl.BlockSpec((seq, width), lambda b, h: (b, h))
    return pl.pallas_call(
        _attn_prompt_body, grid=(batch, N_HEADS // ATT_HEADS_PER_STEP),
        in_specs=[spec, spec, spec, pl.BlockSpec((blk, blk), lambda b, h: (0, 0))],
        out_specs=spec,
        out_shape=jax.ShapeDtypeStruct(q.shape, BF16),
        scratch_shapes=[pltpu.VMEM((ATT_HEADS_PER_STEP, max(seq - 2 * blk, 8), hd), F32)] * 2,
        compiler_params=pltpu.CompilerParams(
            dimension_semantics=("parallel", "parallel"),
            vmem_limit_bytes=_vmem_limit(2 * _nbytes((seq, width), F32), 2 * _nbytes((seq, width), BF16),
                                         scratch=8 * ATT_HEADS_PER_STEP * _nbytes((seq, blk), F32))),
        name="attn_prompt",
    )(q, k, v, _tri(blk))


def _head_cols(h):
    return slice(h * HEAD_DIM, (h + 1) * HEAD_DIM)


def _sample_scores(q_ref, load_k):
    return jnp.concatenate([_dot_nt(q_ref[:, _head_cols(h)], load_k(h)) for h in range(N_HEADS)], axis=0)


def _sample_values(a, load_v, n_new):
    return jnp.concatenate([_dot(a[h * n_new:(h + 1) * n_new, :], load_v(h)) for h in range(N_HEADS)], axis=0)


def _sample_cache_sweep(q_ref, kc_ref, vc_ref, tri, run, acc, n_new):
    bk = ATT_BLK
    for sb in range(kc_ref.shape[0] // (bk * N_HEADS) - 1, -1, -1):
        def head_rows_of(ref, h):
            return ref[pl.ds(sb * bk * N_HEADS + h, bk, stride=N_HEADS), :].astype(BF16)
        z = _sample_scores(q_ref, lambda h: head_rows_of(kc_ref, h))
        a, tot = _sb_weights(z, tri, run, None)
        acc = acc + _sample_values(a, lambda h: head_rows_of(vc_ref, h), n_new)
        run = run + tot
    return run, acc


def _sample_store(o_ref, acc, n_new):
    for h in range(N_HEADS):
        o_ref[:, _head_cols(h)] = acc[h * n_new:(h + 1) * n_new, :].astype(o_ref.dtype)


def _attn_sample_near_body(q_ref, kn_ref, vn_ref, kc_ref, vc_ref, tri_ref, trin_ref,
                           o_ref, run_ref, acc_ref, alive_ref, *, n_new):
    rows = N_HEADS * n_new
    z = _sample_scores(q_ref, lambda h: kn_ref[:, _head_cols(h)].astype(BF16))
    t = lax.broadcasted_iota(jnp.int32, (rows, n_new), 0) % n_new
    s = lax.broadcasted_iota(jnp.int32, (rows, n_new), 1)
    a, tot = _sb_weights(z, trin_ref[...], None, lambda x: jnp.where(s < t, x, 0.0))
    acc = _sample_values(a, lambda h: vn_ref[:, _head_cols(h)].astype(BF16), n_new)
    run = jnp.broadcast_to(tot, (rows, HEAD_DIM))
    run, acc = _sample_cache_sweep(q_ref, kc_ref, vc_ref, tri_ref[...], run, acc, n_new)
    _sample_store(o_ref, acc, n_new)
    run_ref[...] = run
    acc_ref[...] = acc
    alive_ref[...] = jnp.broadcast_to((jnp.min(run) <= ATT_DEAD_LOG2).astype(jnp.int32), alive_ref.shape)


def _attn_sample_far_body(alive_ref, src_ref, q_ref, kc_ref, vc_ref, tri_ref, run_ref, acc_ref, near_ref,
                          o_ref, *, n_new):
    del src_ref
    b = pl.program_id(0)

    @pl.when(alive_ref[b] == 0)
    def _():
        o_ref[...] = near_ref[...]

    @pl.when(alive_ref[b] != 0)
    def _():
        _, acc = _sample_cache_sweep(q_ref, kc_ref, vc_ref, tri_ref[...], run_ref[...], acc_ref[...], n_new)
        _sample_store(o_ref, acc, n_new)


def _attn_sample(q, k_new, v_new, cache_k, cache_v, n_streams, n_new, past_len):
    bk = ATT_BLK
    hd = HEAD_DIM
    assert past_len % bk == 0
    d_all = N_HEADS * hd
    rows = N_HEADS * n_new
    near_rows = bk * N_HEADS
    far_rows = (past_len - bk) * N_HEADS
    last_near = past_len // bk - 1

    new_spec = pl.BlockSpec((n_new, d_all), lambda b: (b, 0))
    near_spec = pl.BlockSpec((None, near_rows, hd), lambda b: (b, last_near, 0))
    state_spec = pl.BlockSpec((rows, hd), lambda b: (b, 0))
    alive_spec = pl.BlockSpec((None, 8, 128), lambda b: (b, 0, 0))
    tri = _tri(bk)
    o_near, run, acc, alive = pl.pallas_call(
        functools.partial(_attn_sample_near_body, n_new=n_new),
        grid=(n_streams,),
        in_specs=[new_spec, new_spec, new_spec, near_spec, near_spec,
                  pl.BlockSpec((bk, bk), lambda b: (0, 0)),
                  pl.BlockSpec((n_new, n_new), lambda b: (0, 0))],
        out_specs=[new_spec, state_spec, state_spec, alive_spec],
        out_shape=[jax.ShapeDtypeStruct(q.shape, BF16),
                   jax.ShapeDtypeStruct((n_streams * rows, hd), F32),
                   jax.ShapeDtypeStruct((n_streams * rows, hd), F32),
                   jax.ShapeDtypeStruct((n_streams, 8, 128), jnp.int32)],
        compiler_params=pltpu.CompilerParams(
            dimension_semantics=("arbitrary",),
            vmem_limit_bytes=_vmem_limit(2 * _nbytes((near_rows, hd), F32))),
        name="attn_sample_near",
    )(q, k_new, v_new, cache_k, cache_v, tri, _tri(n_new))
    if far_rows == 0:
        return o_near

    alive = alive[:, 0, 0]
    src = lax.cummax(jnp.where(alive > 0, jnp.arange(n_streams, dtype=jnp.int32), 0))
    row_map = lambda b, alive, src: (b, 0)
    far_spec = pl.BlockSpec((None, far_rows, hd), lambda b, alive, src: (src[b], 0, 0))
    return pl.pallas_call(
        functools.partial(_attn_sample_far_body, n_new=n_new),
        grid_spec=pltpu.PrefetchScalarGridSpec(
            num_scalar_prefetch=2, grid=(n_streams,),
            in_specs=[pl.BlockSpec((n_new, d_all), row_map), far_spec, far_spec,
                      pl.BlockSpec((bk, bk), lambda b, alive, src: (0, 0)),
                      pl.BlockSpec((rows, hd), row_map), pl.BlockSpec((rows, hd), row_map),
                      pl.BlockSpec((n_new, d_all), row_map)],
            out_specs=pl.BlockSpec((n_new, d_all), row_map)),
        out_shape=jax.ShapeDtypeStruct(q.shape, BF16),
        compiler_params=pltpu.CompilerParams(
            dimension_semantics=("arbitrary",),
            vmem_limit_bytes=_vmem_limit(2 * _nbytes((far_rows, hd), F32))),
        name="attn_sample_far",
    )(alive, src, q, cache_k, cache_v, tri, run, acc, o_near)


def _merge(cb, conv, o, ga, gb, wco_ref, wao_ref, m_ref):
    y_a = _dot((cb * conv).astype(BF16), wco_ref[...])
    y_b = _dot(o, wao_ref[...])
    m_ref[...] = (_sigmoid(ga) * y_a + _sigmoid(gb) * y_b).astype(m_ref.dtype)


def _mix_prompt_body(cb_ref, cc_ref, cx_ref, o_ref, ga_ref, gb_ref, cw_ref, wco_ref, wao_ref,
                     m_ref, hist_ref, carry_scr, *, tiles_per_seq):
    i = pl.program_id(0)
    tm = cc_ref.shape[0]

    @pl.when(i % tiles_per_seq == 0)
    def _():
        carry_scr[...] = jnp.zeros_like(carry_scr)

    u = cc_ref[...] * cx_ref[...]
    row = lax.broadcasted_iota(jnp.int32, u.shape, 0)
    p1 = carry_scr[7:8, :]
    p2 = carry_scr[6:7, :]
    u1 = jnp.where(row == 0, p1, pltpu.roll(u, 1, axis=0))
    u2 = jnp.where(row == 0, p2, jnp.where(row == 1, p1, pltpu.roll(u, 2, axis=0)))
    conv = u2 * cw_ref[0:1, :] + u1 * cw_ref[1:2, :] + u * cw_ref[2:3, :]
    carry_scr[...] = u[tm - 8:, :]
    hist_ref[...] = u[tm - (CONV_W - 1):, :]
    _merge(cb_ref[...], conv, o_ref[...], ga_ref[...], gb_ref[...], wco_ref, wao_ref, m_ref)


def _mix_sample_body(cb_ref, cc_ref, cx_ref, o_ref, ga_ref, gb_ref, cw_ref, wco_ref, wao_ref,
                     p1_ref, p2_ref, m_ref, u_ref, *, n_new):
    u = cc_ref[...] * cx_ref[...]
    t = lax.broadcasted_iota(jnp.int32, u.shape, 0) % n_new
    u1 = jnp.where(t == 0, p1_ref[...], pltpu.roll(u, 1, axis=0))
    u2 = jnp.where(t < 2, p2_ref[...], pltpu.roll(u, 2, axis=0))
    conv = u2 * cw_ref[0:1, :] + u1 * cw_ref[1:2, :] + u * cw_ref[2:3, :]
    u_ref[...] = u
    _merge(cb_ref[...], conv, o_ref[...], ga_ref[...], gb_ref[...], wco_ref, wao_ref, m_ref)


def _mix_prompt(pconv, o, gates, conv_w, w_co, w_ao, batch, seq):
    T = pconv.shape[0]
    C = pconv.shape[1] // 3
    D = gates.shape[1] // 2
    Da = o.shape[1]
    tm = MIX_TM
    assert seq % tm == 0
    tps = seq // tm
    row = lambda c: (lambda i: (i, c))
    const = lambda i: (0, 0)
    m, hist = pl.pallas_call(
        functools.partial(_mix_prompt_body, tiles_per_seq=tps),
        grid=(T // tm,),
        in_specs=[
            pl.BlockSpec((tm, C), row(0)), pl.BlockSpec((tm, C), row(1)), pl.BlockSpec((tm, C), row(2)),
            pl.BlockSpec((tm, Da), row(0)),
            pl.BlockSpec((tm, D), row(0)), pl.BlockSpec((tm, D), row(1)),
            pl.BlockSpec((CONV_W, C), const),
            pl.BlockSpec((C, D), const, pipeline_mode=pl.Buffered(1)),
            pl.BlockSpec((Da, D), const, pipeline_mode=pl.Buffered(1)),
        ],
        out_specs=[pl.BlockSpec((tm, D), row(0)),
                   pl.BlockSpec((None, CONV_W - 1, C), lambda i: (i // tps, 0, 0))],
        out_shape=[jax.ShapeDtypeStruct((T, D), BF16),
                   jax.ShapeDtypeStruct((batch, CONV_W - 1, C), F32)],
        scratch_shapes=[pltpu.VMEM((8, C), F32)],
        compiler_params=pltpu.CompilerParams(
            dimension_semantics=("arbitrary",),
            vmem_limit_bytes=_vmem_limit(3 * _nbytes((tm, C), F32), _nbytes((tm, Da), BF16),
                                         2 * _nbytes((tm, D), gates.dtype), _nbytes((C, D), BF16),
                                         _nbytes((Da, D), BF16), _nbytes((tm, D), BF16))),
        name="mix_prompt",
    )(pconv, pconv, pconv, o, gates, gates, conv_w, w_co, w_ao)
    return m, hist


def _mix_sample(pconv, o, gates, conv_w, w_co, w_ao, state, n_streams, n_new):
    T = pconv.shape[0]
    C = pconv.shape[1] // 3
    D = gates.shape[1] // 2
    Da = o.shape[1]
    assert T == n_streams * n_new and n_new >= CONV_W - 1
    zeros = jnp.zeros((n_streams, n_new, C), F32)
    p1 = zeros.at[:, 0].set(state[:, 1]).reshape(T, C)
    p2 = zeros.at[:, 0].set(state[:, 0]).at[:, 1].set(state[:, 1]).reshape(T, C)
    col = lambda c: (lambda i: (0, c))
    const = lambda i: (0, 0)
    m, u = pl.pallas_call(
        functools.partial(_mix_sample_body, n_new=n_new),
        grid=(1,),
        in_specs=[
            pl.BlockSpec((T, C), col(0)), pl.BlockSpec((T, C), col(1)), pl.BlockSpec((T, C), col(2)),
            pl.BlockSpec((T, Da), const),
            pl.BlockSpec((T, D), col(0)), pl.BlockSpec((T, D), col(1)),
            pl.BlockSpec((CONV_W, C), const),
            pl.BlockSpec((C, D), const), pl.BlockSpec((Da, D), const),
            pl.BlockSpec((T, C), const), pl.BlockSpec((T, C), const),
        ],
        out_specs=[pl.BlockSpec((T, D), const), pl.BlockSpec((T, C), const)],
        out_shape=[jax.ShapeDtypeStruct((T, D), BF16), jax.ShapeDtypeStruct((T, C), F32)],
        compiler_params=pltpu.CompilerParams(
            dimension_semantics=("arbitrary",),
            vmem_limit_bytes=_vmem_limit(6 * _nbytes((T, C), F32), _nbytes((T, Da), BF16),
                                         2 * _nbytes((T, D), gates.dtype), _nbytes((C, D), BF16),
                                         _nbytes((Da, D), BF16), _nbytes((T, D), BF16))),
        name="mix_sample",
    )(pconv, pconv, pconv, o, gates, gates, conv_w, w_co, w_ao, p1, p2)
    new_hist = u.reshape(n_streams, n_new, C)[:, n_new - (CONV_W - 1):]
    return m, new_hist


def _mixer(x1, h, w, *, batch, seq, sample_state=None, merge_weights=None):
    C = w["conv_w"].shape[1]
    Da = N_HEADS * HEAD_DIM
    T, D = x1.shape
    pconv = _matmul(h, w["w_in"], 0, 3 * C, F32)
    q = _matmul(h, w["w_in"], 3 * C, Da, BF16, out_scale=Q_SCALE)
    k = _matmul(h, w["w_in"], 3 * C + Da, Da, F32)
    v = _matmul(h, w["w_in"], 3 * C + 2 * Da, Da, F32)
    merge_f32 = (w["w_co"], w["w_ao"])
    if merge_weights is None and all(_grid_slab(a.shape, *_mm_grid(T, 2 * D)) for a in merge_f32):
        gates, *merge_weights = _matmul(h, w["w_in"], 3 * C + 3 * Da, 2 * D, F32, side_casts=merge_f32)
    else:
        gates = _matmul(h, w["w_in"], 3 * C + 3 * Da, 2 * D, F32)
    if merge_weights is None:
        merge_weights = [a.astype(BF16) for a in merge_f32]
    w_co, w_ao = merge_weights
    if sample_state is None:
        o = _attn_prompt(q, k, v, batch, seq)
        m, hist = _mix_prompt(pconv, o, gates, w["conv_w"], w_co, w_ao, batch, seq)
    else:
        conv_state, cache_k, cache_v, past_len = sample_state
        o = _attn_sample(q, k, v, cache_k.reshape(batch, past_len * N_HEADS, HEAD_DIM),
                         cache_v.reshape(batch, past_len * N_HEADS, HEAD_DIM), batch, seq, past_len)
        m, hist = _mix_sample(pconv, o, gates, w["conv_w"], w_co, w_ao, conv_state, batch, seq)
    x2 = _matmul(m, w["w_o"], 0, D, F32, residual=x1)
    return x2, hist, k, v, merge_weights


def kernel(x_prompt, x_sample, cache_k, cache_v, state_conv, norm_ffn1, ffn1_w_gate_up, ffn1_w_down,
           norm_mix, w_in, conv_w, w_conv_out, w_attn_out, w_o, norm_ffn2, ffn2_w_gate_up,
           ffn2_w_down, norm_final):
    depth = w_in.shape[0]
    B, T, D = x_prompt.shape
    S, n_new, _ = x_sample.shape
    past_len = cache_k.shape[2]
    xp = x_prompt.reshape(B * T, D)
    xs = x_sample.reshape(S * n_new, D)
    outs = [[] for _ in range(6)]
    for l in range(depth):
        w = {"w_in": w_in[l], "conv_w": conv_w[l], "w_co": w_conv_out[l], "w_ao": w_attn_out[l],
             "w_o": w_o[l]}
        ffn1 = dict(emit_x=True, h_dtype=BF16)
        d_ff = ffn1_w_down.shape[1]
        if _ffn_grid(S * n_new, d_ff)[0] == 1:
            xs1, hs, w_gate1, w_up1, w_dn1 = _ffn(xs, norm_ffn1[l], ffn1_w_gate_up[l], ffn1_w_down[l],
                                                  norm_mix[l], keep_weights=True, **ffn1)
            w_gu1 = (w_gate1, w_up1)
        else:
            w_gu1, w_dn1 = ffn1_w_gate_up[l].astype(BF16), ffn1_w_down[l].astype(BF16)
            xs1, hs = _ffn(xs, norm_ffn1[l], w_gu1, w_dn1, norm_mix[l], **ffn1)
        ffn2_f32 = (ffn2_w_gate_up[l], ffn2_w_down[l])
        fuse = all(_grid_slab(a.shape, *_ffn_grid(B * T, d_ff)) for a in ffn2_f32)
        xp1, hp, *ffn2_bf = _ffn(xp, norm_ffn1[l], w_gu1, w_dn1, norm_mix[l],
                                 side_casts=ffn2_f32 if fuse else (), **ffn1)
        if not fuse:
            ffn2_bf = [a.astype(BF16) for a in ffn2_f32]
        xp2, c_p, k_p, v_p, merge_w = _mixer(xp1, hp, w, batch=B, seq=T)
        xs2, c_s, k_s, v_s, _ = _mixer(xs1, hs, w, batch=S, seq=n_new, merge_weights=merge_w,
                                       sample_state=(state_conv[l], cache_k[l], cache_v[l], past_len))
        w2 = (norm_ffn2[l], *ffn2_bf, norm_final)
        if l == depth - 1:
            xp = _ffn(xp2, *w2, emit_x=False, h_dtype=F32)[0]
            xs = _ffn(xs2, *w2, emit_x=False, h_dtype=F32)[0]
        else:
            xp = _ffn(xp2, *w2, emit_x=True, h_dtype=BF16)[0]
            xs = _ffn(xs2, *w2, emit_x=True, h_dtype=BF16)[0]
        for lst, val in zip(outs, (k_p.reshape(B, T, N_HEADS, HEAD_DIM), v_p.reshape(B, T, N_HEADS, HEAD_DIM),
                                   c_p, k_s.reshape(S, n_new, N_HEADS, HEAD_DIM),
                                   v_s.reshape(S, n_new, N_HEADS, HEAD_DIM), c_s)):
            lst.append(val)
    y_prompt = xp.reshape(B, T, D)
    y_sample = xs.reshape(S, n_new, D)
    return (y_prompt, y_sample) + tuple(jnp.stack(o, axis=0) for o in outs)
```

```python
import functools

import jax
import jax.numpy as jnp
from jax import lax
from jax.experimental import pallas as pl
from jax.experimental.pallas import tpu as pltpu

N_HEADS = 8
HEAD_DIM = 128
CONV_W = 3
EPS = 1e-6
FFN_SCALE = 0.5

F32 = jnp.float32
BF16 = jnp.bfloat16

V7X_VMEM_LIMIT_CAP = 60 * 1024 * 1024

FFN_TM = 512
FFN_TF = 512
FFN_TF_F32 = 256
MM_TM = 1024
MM_TN = 1024
MIX_TM = 512
ATT_BLK = 256
ATT_HEADS_PER_STEP = 2
ATT_DEAD_LOG2 = 160.0


def _vmem_limit(*block_bytes, scratch=0):
    est = 2 * sum(block_bytes) + scratch
    return int(min(V7X_VMEM_LIMIT_CAP, max(2 * est, 16 * 1024 * 1024)))


def _nbytes(shape, dtype):
    n = 1
    for s in shape:
        n *= s
    return n * jnp.dtype(dtype).itemsize


def _rmsnorm(x, g):
    ms = jnp.mean(x * x, axis=-1, keepdims=True)
    return x * lax.rsqrt(ms + EPS) * g


def _dot(a, b):
    return jnp.dot(a, b, preferred_element_type=F32)


def _dot_nt(a, b):
    return lax.dot_general(a, b, (((1,), (1,)), ((), ())), preferred_element_type=F32)


def _ffn_body(x_ref, gin_ref, wg_ref, wu_ref, wd_ref, gout_ref, *rest, emit_x, n_side, keep_weights):
    side_in, rest = rest[:n_side], rest[n_side:]
    if emit_x:
        xo_ref, ho_ref = rest[:2]
        rest = rest[2:]
    else:
        ho_ref = rest[0]
        rest = rest[1:]
    side_out, rest = rest[:n_side], rest[n_side:]
    if keep_weights:
        weight_out, rest = rest[:3], rest[3:]
    h_scr, acc_scr = rest
    j = pl.program_id(1)

    @pl.when(j == 0)
    def _():
        h_scr[...] = _rmsnorm(x_ref[...], gin_ref[...]).astype(BF16)
        acc_scr[...] = jnp.zeros_like(acc_scr)

    wg, wu, wd = wg_ref[...], wu_ref[...], wd_ref[...]
    if keep_weights:
        wg, wu, wd = wg.astype(BF16), wu.astype(BF16), wd.astype(BF16)
        for dst, tile in zip(weight_out, (wg, wu, wd)):
            dst[...] = tile

    h = h_scr[...]
    g = _dot(h, wg)
    u = _dot(h, wu)
    act = (g * (1.0 / (1.0 + jnp.exp(-g))) * u).astype(BF16)
    acc_scr[...] += _dot(act, wd)

    for src, dst in zip(side_in, side_out):
        dst[...] = src[...].astype(dst.dtype)

    @pl.when(j == pl.num_programs(1) - 1)
    def _():
        xn = x_ref[...] + FFN_SCALE * acc_scr[...]
        if emit_x:
            xo_ref[...] = xn
        ho_ref[...] = _rmsnorm(xn, gout_ref[...]).astype(ho_ref.dtype)


def _grid_slab(shape, n_i, n_j):
    rows, cols = shape
    for nr, nc, imap in ((n_i, n_j, lambda i, j: (i, j)), (n_j, n_i, lambda i, j: (j, i))):
        if rows % nr == 0 and cols % nc == 0 and (rows // nr) % 16 == 0 and (cols // nc) % 128 == 0:
            return (rows // nr, cols // nc), imap
    return None


def _ffn_grid(n_tokens, d_ff, keep_weights=False):
    tm = min(FFN_TM, n_tokens)
    tf = FFN_TF_F32 if keep_weights else FFN_TF
    assert n_tokens % tm == 0 and d_ff % tf == 0
    return n_tokens // tm, d_ff // tf


def _ffn(x, g_in, w_gu, w_dn, g_out, *, emit_x, h_dtype, side_casts=(), keep_weights=False):
    T, D = x.shape
    d_ff = w_dn.shape[0]
    grid = _ffn_grid(T, d_ff, keep_weights)
    tm, tf = T // grid[0], d_ff // grid[1]
    n_f = grid[1]
    w_gate, w_up = w_gu if isinstance(w_gu, tuple) else (w_gu, w_gu)
    up0 = 0 if isinstance(w_gu, tuple) else n_f
    w_dtype = w_dn.dtype
    assert (w_dtype == F32) == keep_weights and (grid[0] == 1 or not keep_weights)
    in_specs = [
        pl.BlockSpec((tm, D), lambda i, j: (i, 0)),
        pl.BlockSpec((1, D), lambda i, j: (0, 0)),
        pl.BlockSpec((D, tf), lambda i, j: (0, j)),
        pl.BlockSpec((D, tf), lambda i, j: (0, j + up0)),
        pl.BlockSpec((tf, D), lambda i, j: (j, 0)),
        pl.BlockSpec((1, D), lambda i, j: (0, 0)),
    ]
    out_shape = [jax.ShapeDtypeStruct((T, D), h_dtype)]
    out_specs = [pl.BlockSpec((tm, D), lambda i, j: (i, 0))]
    if emit_x:
        out_shape = [jax.ShapeDtypeStruct((T, D), F32)] + out_shape
        out_specs = [pl.BlockSpec((tm, D), lambda i, j: (i, 0))] + out_specs
    side_bytes = 0
    for a in side_casts:
        blk, imap = _grid_slab(a.shape, *grid)
        in_specs.append(pl.BlockSpec(blk, imap))
        out_specs.append(pl.BlockSpec(blk, imap))
        out_shape.append(jax.ShapeDtypeStruct(a.shape, BF16))
        side_bytes += _nbytes(blk, F32) + _nbytes(blk, BF16)
    if keep_weights:
        out_specs += [pl.BlockSpec((D, tf), lambda i, j: (0, j)),
                      pl.BlockSpec((D, tf), lambda i, j: (0, j)),
                      pl.BlockSpec((tf, D), lambda i, j: (j, 0))]
        out_shape += [jax.ShapeDtypeStruct((D, d_ff), BF16), jax.ShapeDtypeStruct((D, d_ff), BF16),
                      jax.ShapeDtypeStruct((d_ff, D), BF16)]
        side_bytes += 3 * _nbytes((D, tf), BF16)
    limit = _vmem_limit(
        _nbytes((tm, D), F32), 3 * _nbytes((D, tf), w_dtype),
        _nbytes((tm, D), h_dtype), _nbytes((tm, D), F32) if emit_x else 0, side_bytes,
        scratch=_nbytes((tm, D), BF16) + _nbytes((tm, D), F32))
    return pl.pallas_call(
        functools.partial(_ffn_body, emit_x=emit_x, n_side=len(side_casts), keep_weights=keep_weights),
        grid=grid, in_specs=in_specs, out_specs=out_specs, out_shape=out_shape,
        scratch_shapes=[pltpu.VMEM((tm, D), BF16), pltpu.VMEM((tm, D), F32)],
        compiler_params=pltpu.CompilerParams(
            dimension_semantics=("arbitrary", "arbitrary"), vmem_limit_bytes=limit),
        name="ffn_mid" if emit_x else "ffn_final",
    )(x, g_in.reshape(1, D), w_gate, w_up, w_dn, g_out.reshape(1, D), *side_casts)


def _sigmoid(x):
    return 1.0 / (1.0 + jnp.exp(-x))


def _mm_body(x_ref, w_ref, *rest, out_scale, has_residual, has_tail, n_side):
    rest = list(rest)
    xt_ref = rest.pop(0) if has_tail else None
    r_ref = rest.pop(0) if has_residual else None
    rt_ref = rest.pop(0) if has_residual and has_tail else None
    side_in, rest = rest[:n_side], rest[n_side:]
    o_ref = rest.pop(0)
    ot_ref = rest.pop(0) if has_tail else None
    side_out, wb_scr = rest[:n_side], rest[n_side]
    i = pl.program_id(1)

    @pl.when(i == 0)
    def _():
        wb_scr[...] = w_ref[...].astype(BF16)

    def project(x, r, o):
        y = _dot(x[...], wb_scr[...])
        if out_scale is not None:
            y = y * out_scale
        if has_residual:
            y = r[...] + y
        o[...] = y.astype(o.dtype)

    if has_tail:
        last = pl.num_programs(1) - 1
        pl.when(i < last)(lambda: project(x_ref, r_ref, o_ref))
        pl.when(i == last)(lambda: project(xt_ref, rt_ref, ot_ref))
    else:
        project(x_ref, r_ref, o_ref)
    for src, dst in zip(side_in, side_out):
        dst[...] = src[...].astype(dst.dtype)


def _mm_grid(n_tokens, n):
    tm, tn = min(MM_TM, n_tokens), min(MM_TN, n)
    assert n_tokens % tm == 0 and n % tn == 0
    return n // tn, n_tokens // tm


def _matmul(x, w, col0, n, out_dtype, residual=None, out_scale=None, side_casts=(), tail=None):
    T, K = x.shape
    n_j, n_main = _mm_grid(T, n)
    tm, tn = T // n_main, n // n_j
    assert col0 % tn == 0
    c0 = col0 // tn
    main = lambda i: jnp.minimum(i, n_main - 1)
    in_specs = [
        pl.BlockSpec((tm, K), lambda j, i: (main(i), 0)),
        pl.BlockSpec((K, tn), lambda j, i: (0, j + c0)),
    ]
    args = [x, w]
    blocks = [_nbytes((tm, K), x.dtype), _nbytes((K, tn), w.dtype), _nbytes((tm, tn), out_dtype)]
    out_specs = [pl.BlockSpec((tm, tn), lambda j, i: (main(i), j))]
    out_shape = [jax.ShapeDtypeStruct((T, n), out_dtype)]
    if tail is not None:
        x_tail, r_tail = tail
        t_tail = x_tail.shape[0]
        assert t_tail <= tm and (r_tail is None) == (residual is None)
        in_specs.append(pl.BlockSpec((t_tail, K), lambda j, i: (0, 0)))
        args.append(x_tail)
        out_specs.append(pl.BlockSpec((t_tail, tn), lambda j, i: (0, j)))
        out_shape.append(jax.ShapeDtypeStruct((t_tail, n), out_dtype))
        blocks += [_nbytes((t_tail, K), x.dtype), _nbytes((t_tail, tn), out_dtype)]
    if residual is not None:
        in_specs.append(pl.BlockSpec((tm, tn), lambda j, i: (main(i), j)))
        args.append(residual)
        blocks.append(_nbytes((tm, tn), F32))
        if tail is not None:
            in_specs.append(pl.BlockSpec((t_tail, tn), lambda j, i: (0, j)))
            args.append(r_tail)
            blocks.append(_nbytes((t_tail, tn), F32))
    for a in side_casts:
        blk, imap = _grid_slab(a.shape, n_j, n_main)
        slab_map = lambda j, i, imap=imap: imap(j, main(i))
        in_specs.append(pl.BlockSpec(blk, slab_map))
        args.append(a)
        out_specs.append(pl.BlockSpec(blk, slab_map))
        out_shape.append(jax.ShapeDtypeStruct(a.shape, BF16))
        blocks.append(_nbytes(blk, F32) + _nbytes(blk, BF16))
    return pl.pallas_call(
        functools.partial(_mm_body, out_scale=out_scale, has_residual=residual is not None,
                          has_tail=tail is not None, n_side=len(side_casts)),
        grid=(n_j, n_main + (tail is not None)),
        in_specs=in_specs, out_specs=out_specs, out_shape=out_shape,
        scratch_shapes=[pltpu.VMEM((K, tn), BF16)],
        compiler_params=pltpu.CompilerParams(
            dimension_semantics=("arbitrary", "arbitrary"),
            vmem_limit_bytes=_vmem_limit(*blocks, scratch=_nbytes((K, tn), BF16))),
        name="proj",
    )(*args)


def _tri(bk):
    j = lax.broadcasted_iota(jnp.int32, (bk, bk), 0)
    s = lax.broadcasted_iota(jnp.int32, (bk, bk), 1)
    return (j >= s).astype(BF16)


LOG2E = 1.4426950408889634
Q_SCALE = HEAD_DIM ** -0.5 * LOG2E


def _softplus2(z2):
    return jnp.maximum(z2, 0.0) + jnp.log(1.0 + jnp.exp2(-jnp.abs(z2))) * LOG2E


def _cumsum_mxu(sp, tri):
    hi = sp.astype(BF16)
    lo = (sp - hi.astype(F32)).astype(BF16)
    if sp.shape[1] % 128 == 0:
        return _dot(jnp.concatenate([hi, lo], axis=1), jnp.concatenate([tri, tri], axis=0))
    return _dot(hi, tri) + _dot(lo, tri)


def _sb_weights(z2, tri, run, masked):
    sp = _softplus2(z2)
    if masked is not None:
        sp = masked(sp)
    tot = _cumsum_mxu(sp, tri)
    if run is not None:
        tot = tot + jnp.concatenate([run] * (z2.shape[1] // run.shape[1]), axis=1)
    a = jnp.exp2(z2 - tot)
    if masked is not None:
        a = masked(a)
    return a.astype(BF16), jnp.sum(sp, axis=1, keepdims=True)


def _attn_prompt_body(q_ref, k_ref, v_ref, tri_ref, o_ref, run_scr, acc_scr):
    blk = ATT_BLK
    seq = q_ref.shape[0]
    nb = seq // blk
    tri = tri_ref[...]
    heads = [slice(h * HEAD_DIM, (h + 1) * HEAD_DIM) for h in range(ATT_HEADS_PER_STEP)]
    block = lambda b: slice(b * blk, (b + 1) * blk)

    def key_block(ref, b, cols):
        return ref[block(b), cols].astype(BF16)

    def stacked_scores(cols, shift):
        return jnp.concatenate([_dot_nt(q_ref[block(b), cols], key_block(k_ref, b - shift, cols))
                                for b in range(shift, nb)], axis=0)

    def stacked_values(a, cols, shift):
        return jnp.concatenate([_dot(a[block(b - shift), :], key_block(v_ref, b - shift, cols))
                                for b in range(shift, nb)], axis=0)

    t = lax.broadcasted_iota(jnp.int32, (seq, blk), 0) % blk
    s = lax.broadcasted_iota(jnp.int32, (seq, blk), 1)
    causal = s < t
    run, acc = [], []
    for cols in heads:
        a, tot = _sb_weights(stacked_scores(cols, 0), tri, None, lambda x: jnp.where(causal, x, 0.0))
        run.append(jnp.broadcast_to(tot, (seq, HEAD_DIM)))
        acc.append(stacked_values(a, cols, 0))
    if nb > 1:
        for i, cols in enumerate(heads):
            a, tot = _sb_weights(stacked_scores(cols, 1), tri, run[i][blk:], None)
            run[i] = jnp.concatenate([run[i][:blk], run[i][blk:] + tot], axis=0)
            acc[i] = jnp.concatenate([acc[i][:blk], acc[i][blk:] + stacked_values(a, cols, 1)], axis=0)
    for i, cols in enumerate(heads):
        o_ref[:, cols] = acc[i].astype(o_ref.dtype)
    if nb <= 2:
        return

    far = 2 * blk
    alive = None
    for i in range(len(heads)):
        run_scr[i] = run[i][far:]
        acc_scr[i] = acc[i][far:]
        lowest = jnp.min(run[i][far:])
        alive = lowest if alive is None else jnp.minimum(alive, lowest)

    @pl.when(alive <= ATT_DEAD_LOG2)
    def _():
        run = [run_scr[i] for i in range(len(heads))]
        acc = [acc_scr[i] for i in range(len(heads))]
        for c in range(nb - 3, -1, -1):
            top = c * blk
            for i, cols in enumerate(heads):
                z2 = _dot_nt(q_ref[far + top:, cols], key_block(k_ref, c, cols))
                a, tot = _sb_weights(z2, tri, run[i][top:], None)
                pv = _dot(a, key_block(v_ref, c, cols))
                run[i] = jnp.concatenate([run[i][:top], run[i][top:] + tot], axis=0) if top else run[i] + tot
                acc[i] = jnp.concatenate([acc[i][:top], acc[i][top:] + pv], axis=0) if top else acc[i] + pv
        for i, cols in enumerate(heads):
            o_ref[far:, cols] = acc[i].astype(o_ref.dtype)


def _attn_prompt(q, k, v, batch, seq):
    blk = ATT_BLK
    assert seq % blk == 0 and N_HEADS % ATT_HEADS_PER_STEP == 0
    hd = HEAD_DIM
    width = ATT_HEADS_PER_STEP * hd
    spec = pl.BlockSpec((seq, width), lambda b, h: (b, h))
    return pl.pallas_call(
        _attn_prompt_body, grid=(batch, N_HEADS // ATT_HEADS_PER_STEP),
        in_specs=[spec, spec, spec, pl.BlockSpec((blk, blk), lambda b, h: (0, 0))],
        out_specs=spec,
        out_shape=jax.ShapeDtypeStruct(q.shape, BF16),
        scratch_shapes=[pltpu.VMEM((ATT_HEADS_PER_STEP, max(seq - 2 * blk, 8), hd), F32)] * 2,
        compiler_params=pltpu.CompilerParams(
            dimension_semantics=("parallel", "parallel"),
            vmem_limit_bytes=_vmem_limit(2 * _nbytes((seq, width), F32), 2 * _nbytes((seq, width), BF16),
                                         scratch=8 * ATT_HEADS_PER_STEP * _nbytes((seq, blk), F32))),
        name="attn_prompt",
    )(q, k, v, _tri(blk))


def _head_cols(h):
    return slice(h * HEAD_DIM, (h + 1) * HEAD_DIM)


def _sample_scores(q_ref, load_k):
    return jnp.concatenate([_dot_nt(q_ref[:, _head_cols(h)], load_k(h)) for h in range(N_HEADS)], axis=0)


def _sample_values(a, load_v, n_new):
    return jnp.concatenate([_dot(a[h * n_new:(h + 1) * n_new, :], load_v(h)) for h in range(N_HEADS)], axis=0)


def _sample_cache_sweep(q_ref, kc_ref, vc_ref, tri, run, acc, n_new):
    bk = ATT_BLK
    for sb in range(kc_ref.shape[0] // (bk * N_HEADS) - 1, -1, -1):
        def head_rows_of(ref, h):
            return ref[pl.ds(sb * bk * N_HEADS + h, bk, stride=N_HEADS), :].astype(BF16)
        z = _sample_scores(q_ref, lambda h: head_rows_of(kc_ref, h))
        a, tot = _sb_weights(z, tri, run, None)
        acc = acc + _sample_values(a, lambda h: head_rows_of(vc_ref, h), n_new)
        run = run + tot
    return run, acc


def _sample_store(o_ref, acc, n_new):
    for h in range(N_HEADS):
        o_ref[:, _head_cols(h)] = acc[h * n_new:(h + 1) * n_new, :].astype(o_ref.dtype)


def _attn_sample_near_body(q_ref, kn_ref, vn_ref, kc_ref, vc_ref, tri_ref, trin_ref,
                           o_ref, run_ref, acc_ref, alive_ref, *, n_new):
    rows = N_HEADS * n_new
    z = _sample_scores(q_ref, lambda h: kn_ref[:, _head_cols(h)].astype(BF16))
    t = lax.broadcasted_iota(jnp.int32, (rows, n_new), 0) % n_new
    s = lax.broadcasted_iota(jnp.int32, (rows, n_new), 1)
    a, tot = _sb_weights(z, trin_ref[...], None, lambda x: jnp.where(s < t, x, 0.0))
    acc = _sample_values(a, lambda h: vn_ref[:, _head_cols(h)].astype(BF16), n_new)
    run = jnp.broadcast_to(tot, (rows, HEAD_DIM))
    run, acc = _sample_cache_sweep(q_ref, kc_ref, vc_ref, tri_ref[...], run, acc, n_new)
    _sample_store(o_ref, acc, n_new)
    run_ref[...] = run
    acc_ref[...] = acc
    alive_ref[...] = jnp.broadcast_to((jnp.min(run) <= ATT_DEAD_LOG2).astype(jnp.int32), alive_ref.shape)


def _attn_sample_far_body(alive_ref, src_ref, q_ref, kc_ref, vc_ref, tri_ref, run_ref, acc_ref, near_ref,
                          o_ref, *, n_new):
    del src_ref
    b = pl.program_id(0)

    @pl.when(alive_ref[b] == 0)
    def _():
        o_ref[...] = near_ref[...]

    @pl.when(alive_ref[b] != 0)
    def _():
        _, acc = _sample_cache_sweep(q_ref, kc_ref, vc_ref, tri_ref[...], run_ref[...], acc_ref[...], n_new)
        _sample_store(o_ref, acc, n_new)


def _attn_sample(q, k_new, v_new, cache_k, cache_v, n_streams, n_new, past_len):
    bk = ATT_BLK
    hd = HEAD_DIM
    assert past_len % bk == 0
    d_all = N_HEADS * hd
    rows = N_HEADS * n_new
    near_rows = bk * N_HEADS
    far_rows = (past_len - bk) * N_HEADS
    last_near = past_len // bk - 1

    new_spec = pl.BlockSpec((n_new, d_all), lambda b: (b, 0))
    near_spec = pl.BlockSpec((None, near_rows, hd), lambda b: (b, last_near, 0))
    state_spec = pl.BlockSpec((rows, hd), lambda b: (b, 0))
    alive_spec = pl.BlockSpec((None, 8, 128), lambda b: (b, 0, 0))
    tri = _tri(bk)
    o_near, run, acc, alive = pl.pallas_call(
        functools.partial(_attn_sample_near_body, n_new=n_new),
        grid=(n_streams,),
        in_specs=[new_spec, new_spec, new_spec, near_spec, near_spec,
                  pl.BlockSpec((bk, bk), lambda b: (0, 0)),
                  pl.BlockSpec((n_new, n_new), lambda b: (0, 0))],
        out_specs=[new_spec, state_spec, state_spec, alive_spec],
        out_shape=[jax.ShapeDtypeStruct(q.shape, BF16),
                   jax.ShapeDtypeStruct((n_streams * rows, hd), F32),
                   jax.ShapeDtypeStruct((n_streams * rows, hd), F32),
                   jax.ShapeDtypeStruct((n_streams, 8, 128), jnp.int32)],
        compiler_params=pltpu.CompilerParams(
            dimension_semantics=("arbitrary",),
            vmem_limit_bytes=_vmem_limit(2 * _nbytes((near_rows, hd), F32))),
        name="attn_sample_near",
    )(q, k_new, v_new, cache_k, cache_v, tri, _tri(n_new))
    if far_rows == 0:
        return o_near

    alive = alive[:, 0, 0]
    src = lax.cummax(jnp.where(alive > 0, jnp.arange(n_streams, dtype=jnp.int32), 0))
    row_map = lambda b, alive, src: (b, 0)
    far_spec = pl.BlockSpec((None, far_rows, hd), lambda b, alive, src: (src[b], 0, 0))
    return pl.pallas_call(
        functools.partial(_attn_sample_far_body, n_new=n_new),
        grid_spec=pltpu.PrefetchScalarGridSpec(
            num_scalar_prefetch=2, grid=(n_streams,),
            in_specs=[pl.BlockSpec((n_new, d_all), row_map), far_spec, far_spec,
                      pl.BlockSpec((bk, bk), lambda b, alive, src: (0, 0)),
                      pl.BlockSpec((rows, hd), row_map), pl.BlockSpec((rows, hd), row_map),
                      pl.BlockSpec((n_new, d_all), row_map)],
            out_specs=pl.BlockSpec((n_new, d_all), row_map)),
        out_shape=jax.ShapeDtypeStruct(q.shape, BF16),
        compiler_params=pltpu.CompilerParams(
            dimension_semantics=("arbitrary",),
            vmem_limit_bytes=_vmem_limit(2 * _nbytes((far_rows, hd), F32))),
        name="attn_sample_far",
    )(alive, src, q, cache_k, cache_v, tri, run, acc, o_near)


def _merge(cb, conv, o, ga, gb, wco_ref, wao_ref, m_ref):
    y_a = _dot((cb * conv).astype(BF16), wco_ref[...])
    y_b = _dot(o, wao_ref[...])
    m_ref[...] = (_sigmoid(ga) * y_a + _sigmoid(gb) * y_b).astype(m_ref.dtype)


def _mix_prompt_body(cb_ref, cc_ref, cx_ref, o_ref, ga_ref, gb_ref, cw_ref, wco_ref, wao_ref,
                     m_ref, hist_ref, carry_scr, *, tiles_per_seq):
    i = pl.program_id(0)
    tm = cc_ref.shape[0]

    @pl.when(i % tiles_per_seq == 0)
    def _():
        carry_scr[...] = jnp.zeros_like(carry_scr)

    u = cc_ref[...] * cx_ref[...]
    row = lax.broadcasted_iota(jnp.int32, u.shape, 0)
    p1 = carry_scr[7:8, :]
    p2 = carry_scr[6:7, :]
    u1 = jnp.where(row == 0, p1, pltpu.roll(u, 1, axis=0))
    u2 = jnp.where(row == 0, p2, jnp.where(row == 1, p1, pltpu.roll(u, 2, axis=0)))
    conv = u2 * cw_ref[0:1, :] + u1 * cw_ref[1:2, :] + u * cw_ref[2:3, :]
    carry_scr[...] = u[tm - 8:, :]
    hist_ref[...] = u[tm - (CONV_W - 1):, :]
    _merge(cb_ref[...], conv, o_ref[...], ga_ref[...], gb_ref[...], wco_ref, wao_ref, m_ref)


def _mix_sample_body(cb_ref, cc_ref, cx_ref, o_ref, ga_ref, gb_ref, cw_ref, wco_ref, wao_ref,
                     p1_ref, p2_ref, m_ref, u_ref, *, n_new):
    u = cc_ref[...] * cx_ref[...]
    t = lax.broadcasted_iota(jnp.int32, u.shape, 0) % n_new
    u1 = jnp.where(t == 0, p1_ref[...], pltpu.roll(u, 1, axis=0))
    u2 = jnp.where(t < 2, p2_ref[...], pltpu.roll(u, 2, axis=0))
    conv = u2 * cw_ref[0:1, :] + u1 * cw_ref[1:2, :] + u * cw_ref[2:3, :]
    u_ref[...] = u
    _merge(cb_ref[...], conv, o_ref[...], ga_ref[...], gb_ref[...], wco_ref, wao_ref, m_ref)


def _mix_prompt(pconv, o, gates, conv_w, w_co, w_ao, batch, seq):
    T = pconv.shape[0]
    C = pconv.shape[1] // 3
    D = gates.shape[1] // 2
    Da = o.shape[1]
    tm = MIX_TM
    assert seq % tm == 0
    tps = seq // tm
    row = lambda c: (lambda i: (i, c))
    const = lambda i: (0, 0)
    m, hist = pl.pallas_call(
        functools.partial(_mix_prompt_body, tiles_per_seq=tps),
        grid=(T // tm,),
        in_specs=[
            pl.BlockSpec((tm, C), row(0)), pl.BlockSpec((tm, C), row(1)), pl.BlockSpec((tm, C), row(2)),
            pl.BlockSpec((tm, Da), row(0)),
            pl.BlockSpec((tm, D), row(0)), pl.BlockSpec((tm, D), row(1)),
            pl.BlockSpec((CONV_W, C), const),
            pl.BlockSpec((C, D), const, pipeline_mode=pl.Buffered(1)),
            pl.BlockSpec((Da, D), const, pipeline_mode=pl.Buffered(1)),
        ],
        out_specs=[pl.BlockSpec((tm, D), row(0)),
                   pl.BlockSpec((None, CONV_W - 1, C), lambda i: (i // tps, 0, 0))],
        out_shape=[jax.ShapeDtypeStruct((T, D), BF16),
                   jax.ShapeDtypeStruct((batch, CONV_W - 1, C), F32)],
        scratch_shapes=[pltpu.VMEM((8, C), F32)],
        compiler_params=pltpu.CompilerParams(
            dimension_semantics=("arbitrary",),
            vmem_limit_bytes=_vmem_limit(3 * _nbytes((tm, C), F32), _nbytes((tm, Da), BF16),
                                         2 * _nbytes((tm, D), gates.dtype), _nbytes((C, D), BF16),
                                         _nbytes((Da, D), BF16), _nbytes((tm, D), BF16))),
        name="mix_prompt",
    )(pconv, pconv, pconv, o, gates, gates, conv_w, w_co, w_ao)
    return m, hist


def _mix_sample(pconv, o, gates, conv_w, w_co, w_ao, state, n_streams, n_new):
    T = pconv.shape[0]
    C = pconv.shape[1] // 3
    D = gates.shape[1] // 2
    Da = o.shape[1]
    assert T == n_streams * n_new and n_new >= CONV_W - 1
    zeros = jnp.zeros((n_streams, n_new, C), F32)
    p1 = zeros.at[:, 0].set(state[:, 1]).reshape(T, C)
    p2 = zeros.at[:, 0].set(state[:, 0]).at[:, 1].set(state[:, 1]).reshape(T, C)
    col = lambda c: (lambda i: (0, c))
    const = lambda i: (0, 0)
    m, u = pl.pallas_call(
        functools.partial(_mix_sample_body, n_new=n_new),
        grid=(1,),
        in_specs=[
            pl.BlockSpec((T, C), col(0)), pl.BlockSpec((T, C), col(1)), pl.BlockSpec((T, C), col(2)),
            pl.BlockSpec((T, Da), const),
            pl.BlockSpec((T, D), col(0)), pl.BlockSpec((T, D), col(1)),
            pl.BlockSpec((CONV_W, C), const),
            pl.BlockSpec((C, D), const), pl.BlockSpec((Da, D), const),
            pl.BlockSpec((T, C), const), pl.BlockSpec((T, C), const),
        ],
        out_specs=[pl.BlockSpec((T, D), const), pl.BlockSpec((T, C), const)],
        out_shape=[jax.ShapeDtypeStruct((T, D), BF16), jax.ShapeDtypeStruct((T, C), F32)],
        compiler_params=pltpu.CompilerParams(
            dimension_semantics=("arbitrary",),
            vmem_limit_bytes=_vmem_limit(6 * _nbytes((T, C), F32), _nbytes((T, Da), BF16),
                                         2 * _nbytes((T, D), gates.dtype), _nbytes((C, D), BF16),
                                         _nbytes((Da, D), BF16), _nbytes((T, D), BF16))),
        name="mix_sample",
    )(pconv, pconv, pconv, o, gates, gates, conv_w, w_co, w_ao, p1, p2)
    new_hist = u.reshape(n_streams, n_new, C)[:, n_new - (CONV_W - 1):]
    return m, new_hist


def _project_both(hp, hs, w, col0, n, out_dtype, residuals=(None, None), **kw):
    if hs.shape[0] <= min(MM_TM, hp.shape[0]):
        out_p, out_s, *side = _matmul(hp, w, col0, n, out_dtype, residual=residuals[0],
                                      tail=(hs, residuals[1]), **kw)
    else:
        out_p, *side = _matmul(hp, w, col0, n, out_dtype, residual=residuals[0], **kw)
        out_s, = _matmul(hs, w, col0, n, out_dtype, residual=residuals[1],
                         out_scale=kw.get("out_scale"))
    return out_p, out_s, side


def _mixers(xp1, hp, xs1, hs, w, *, prompt_shape, sample_shape, sample_state):
    C = w["conv_w"].shape[1]
    Da = N_HEADS * HEAD_DIM
    D = xp1.shape[1]
    w_in = w["w_in"]
    pconv_p, pconv_s, _ = _project_both(hp, hs, w_in, 0, 3 * C, F32)
    q_p, q_s, _ = _project_both(hp, hs, w_in, 3 * C, Da, BF16, out_scale=Q_SCALE)
    k_p, k_s, _ = _project_both(hp, hs, w_in, 3 * C + Da, Da, F32)
    v_p, v_s, _ = _project_both(hp, hs, w_in, 3 * C + 2 * Da, Da, F32)
    merge_f32 = (w["w_co"], w["w_ao"])
    fuse = all(_grid_slab(a.shape, *_mm_grid(hp.shape[0], 2 * D)) for a in merge_f32)
    gates_p, gates_s, merge_w = _project_both(hp, hs, w_in, 3 * C + 3 * Da, 2 * D, F32,
                                              side_casts=merge_f32 if fuse else ())
    w_co, w_ao = merge_w if fuse else [a.astype(BF16) for a in merge_f32]

    B, T = prompt_shape
    o_p = _attn_prompt(q_p, k_p, v_p, B, T)
    m_p, hist_p = _mix_prompt(pconv_p, o_p, gates_p, w["conv_w"], w_co, w_ao, B, T)

    S, n_new = sample_shape
    conv_state, cache_k, cache_v, past_len = sample_state
    o_s = _attn_sample(q_s, k_s, v_s, cache_k.reshape(S, past_len * N_HEADS, HEAD_DIM),
                       cache_v.reshape(S, past_len * N_HEADS, HEAD_DIM), S, n_new, past_len)
    m_s, hist_s = _mix_sample(pconv_s, o_s, gates_s, w["conv_w"], w_co, w_ao, conv_state, S, n_new)

    x2_p, x2_s, _ = _project_both(m_p, m_s, w["w_o"], 0, D, F32, residuals=(xp1, xs1))
    return (x2_p, hist_p, k_p, v_p), (x2_s, hist_s, k_s, v_s)


def kernel(x_prompt, x_sample, cache_k, cache_v, state_conv, norm_ffn1, ffn1_w_gate_up, ffn1_w_down,
           norm_mix, w_in, conv_w, w_conv_out, w_attn_out, w_o, norm_ffn2, ffn2_w_gate_up,
           ffn2_w_down, norm_final):
    depth = w_in.shape[0]
    B, T, D = x_prompt.shape
    S, n_new, _ = x_sample.shape
    past_len = cache_k.shape[2]
    xp = x_prompt.reshape(B * T, D)
    xs = x_sample.reshape(S * n_new, D)
    outs = [[] for _ in range(6)]
    for l in range(depth):
        w = {"w_in": w_in[l], "conv_w": conv_w[l], "w_co": w_conv_out[l], "w_ao": w_attn_out[l],
             "w_o": w_o[l]}
        ffn1 = dict(emit_x=True, h_dtype=BF16)
        d_ff = ffn1_w_down.shape[1]
        if _ffn_grid(S * n_new, d_ff)[0] == 1:
            xs1, hs, w_gate1, w_up1, w_dn1 = _ffn(xs, norm_ffn1[l], ffn1_w_gate_up[l], ffn1_w_down[l],
                                                  norm_mix[l], keep_weights=True, **ffn1)
            w_gu1 = (w_gate1, w_up1)
        else:
            w_gu1, w_dn1 = ffn1_w_gate_up[l].astype(BF16), ffn1_w_down[l].astype(BF16)
            xs1, hs = _ffn(xs, norm_ffn1[l], w_gu1, w_dn1, norm_mix[l], **ffn1)
        ffn2_f32 = (ffn2_w_gate_up[l], ffn2_w_down[l])
        fuse = all(_grid_slab(a.shape, *_ffn_grid(B * T, d_ff)) for a in ffn2_f32)
        xp1, hp, *ffn2_bf = _ffn(xp, norm_ffn1[l], w_gu1, w_dn1, norm_mix[l],
                                 side_casts=ffn2_f32 if fuse else (), **ffn1)
        if not fuse:
            ffn2_bf = [a.astype(BF16) for a in ffn2_f32]
        (xp2, c_p, k_p, v_p), (xs2, c_s, k_s, v_s) = _mixers(
            xp1, hp, xs1, hs, w, prompt_shape=(B, T), sample_shape=(S, n_new),
            sample_state=(state_conv[l], cache_k[l], cache_v[l], past_len))
        w2 = (norm_ffn2[l], *ffn2_bf, norm_final)
        if l == depth - 1:
            xp = _ffn(xp2, *w2, emit_x=False, h_dtype=F32)[0]
            xs = _ffn(xs2, *w2, emit_x=False, h_dtype=F32)[0]
        else:
            xp = _ffn(xp2, *w2, emit_x=True, h_dtype=BF16)[0]
            xs = _ffn(xs2, *w2, emit_x=True, h_dtype=BF16)[0]
        for lst, val in zip(outs, (k_p.reshape(B, T, N_HEADS, HEAD_DIM), v_p.reshape(B, T, N_HEADS, HEAD_DIM),
                                   c_p, k_s.reshape(S, n_new, N_HEADS, HEAD_DIM),
                                   v_s.reshape(S, n_new, N_HEADS, HEAD_DIM), c_s)):
            lst.append(val)
    y_prompt = xp.reshape(B, T, D)
    y_sample = xs.reshape(S, n_new, D)
    return (y_prompt, y_sample) + tuple(jnp.stack(o, axis=0) for o in outs)
```

```python
import functools

import jax
import jax.numpy as jnp
from jax import lax
from jax.experimental import pallas as pl
from jax.experimental.pallas import tpu as pltpu

N_HEADS = 8
HEAD_DIM = 128
CONV_W = 3
EPS = 1e-6
FFN_SCALE = 0.5

F32 = jnp.float32
BF16 = jnp.bfloat16

V7X_VMEM_LIMIT_CAP = 60 * 1024 * 1024

FFN_TM = 512
FFN_TF = 512
FFN_TF_F32 = 256
MM_TM = 1024
MM_TN = 1024
MIX_TM = 512
ATT_BLK = 256
ATT_HEADS_PER_STEP = 2
ATT_DEAD_LOG2 = 160.0


def _vmem_limit(*block_bytes, scratch=0):
    est = 2 * sum(block_bytes) + scratch
    return int(min(V7X_VMEM_LIMIT_CAP, max(2 * est, 16 * 1024 * 1024)))


def _nbytes(shape, dtype):
    n = 1
    for s in shape:
        n *= s
    return n * jnp.dtype(dtype).itemsize


def _rmsnorm(x, g):
    ms = jnp.mean(x * x, axis=-1, keepdims=True)
    return x * lax.rsqrt(ms + EPS) * g


def _dot(a, b):
    return jnp.dot(a, b, preferred_element_type=F32)


def _dot_nt(a, b):
    return lax.dot_general(a, b, (((1,), (1,)), ((), ())), preferred_element_type=F32)


def _ffn_body(x_ref, gin_ref, wg_ref, wu_ref, wd_ref, gout_ref, *rest, emit_x, n_side, keep_weights):
    side_in, rest = rest[:n_side], rest[n_side:]
    if emit_x:
        xo_ref, ho_ref = rest[:2]
        rest = rest[2:]
    else:
        ho_ref = rest[0]
        rest = rest[1:]
    side_out, rest = rest[:n_side], rest[n_side:]
    if keep_weights:
        weight_out, rest = rest[:3], rest[3:]
    h_scr, acc_scr = rest
    j = pl.program_id(1)

    @pl.when(j == 0)
    def _():
        h_scr[...] = _rmsnorm(x_ref[...], gin_ref[...]).astype(BF16)
        acc_scr[...] = jnp.zeros_like(acc_scr)

    wg, wu, wd = wg_ref[...], wu_ref[...], wd_ref[...]
    if keep_weights:
        wg, wu, wd = wg.astype(BF16), wu.astype(BF16), wd.astype(BF16)
        for dst, tile in zip(weight_out, (wg, wu, wd)):
            dst[...] = tile

    h = h_scr[...]
    g = _dot(h, wg)
    u = _dot(h, wu)
    act = (g * (1.0 / (1.0 + jnp.exp(-g))) * u).astype(BF16)
    acc_scr[...] += _dot(act, wd)

    for src, dst in zip(side_in, side_out):
        dst[...] = src[...].astype(dst.dtype)

    @pl.when(j == pl.num_programs(1) - 1)
    def _():
        xn = x_ref[...] + FFN_SCALE * acc_scr[...]
        if emit_x:
            xo_ref[...] = xn
        ho_ref[...] = _rmsnorm(xn, gout_ref[...]).astype(ho_ref.dtype)


def _grid_slab(shape, n_i, n_j):
    rows, cols = shape
    for nr, nc, imap in ((n_i, n_j, lambda i, j: (i, j)), (n_j, n_i, lambda i, j: (j, i))):
        if rows % nr == 0 and cols % nc == 0 and (rows // nr) % 16 == 0 and (cols // nc) % 128 == 0:
            return (rows // nr, cols // nc), imap
    return None


def _ffn_grid(n_tokens, d_ff, keep_weights=False):
    tm = min(FFN_TM, n_tokens)
    tf = FFN_TF_F32 if keep_weights else FFN_TF
    assert n_tokens % tm == 0 and d_ff % tf == 0
    return n_tokens // tm, d_ff // tf


def _ffn(x, g_in, w_gu, w_dn, g_out, *, emit_x, h_dtype, side_casts=(), keep_weights=False):
    T, D = x.shape
    d_ff = w_dn.shape[0]
    grid = _ffn_grid(T, d_ff, keep_weights)
    tm, tf = T // grid[0], d_ff // grid[1]
    n_f = grid[1]
    w_gate, w_up = w_gu if isinstance(w_gu, tuple) else (w_gu, w_gu)
    up0 = 0 if isinstance(w_gu, tuple) else n_f
    w_dtype = w_dn.dtype
    assert (w_dtype == F32) == keep_weights and (grid[0] == 1 or not keep_weights)
    hidden = lambda i, j: jnp.where(i % 2 == 0, j, n_f - 1 - j)
    in_specs = [
        pl.BlockSpec((tm, D), lambda i, j: (i, 0)),
        pl.BlockSpec((1, D), lambda i, j: (0, 0)),
        pl.BlockSpec((D, tf), lambda i, j: (0, hidden(i, j))),
        pl.BlockSpec((D, tf), lambda i, j: (0, hidden(i, j) + up0)),
        pl.BlockSpec((tf, D), lambda i, j: (hidden(i, j), 0)),
        pl.BlockSpec((1, D), lambda i, j: (0, 0)),
    ]
    out_shape = [jax.ShapeDtypeStruct((T, D), h_dtype)]
    out_specs = [pl.BlockSpec((tm, D), lambda i, j: (i, 0))]
    if emit_x:
        out_shape = [jax.ShapeDtypeStruct((T, D), F32)] + out_shape
        out_specs = [pl.BlockSpec((tm, D), lambda i, j: (i, 0))] + out_specs
    side_bytes = 0
    for a in side_casts:
        blk, imap = _grid_slab(a.shape, *grid)
        in_specs.append(pl.BlockSpec(blk, imap))
        out_specs.append(pl.BlockSpec(blk, imap))
        out_shape.append(jax.ShapeDtypeStruct(a.shape, BF16))
        side_bytes += _nbytes(blk, F32) + _nbytes(blk, BF16)
    if keep_weights:
        out_specs += [pl.BlockSpec((D, tf), lambda i, j: (0, j)),
                      pl.BlockSpec((D, tf), lambda i, j: (0, j)),
                      pl.BlockSpec((tf, D), lambda i, j: (j, 0))]
        out_shape += [jax.ShapeDtypeStruct((D, d_ff), BF16), jax.ShapeDtypeStruct((D, d_ff), BF16),
                      jax.ShapeDtypeStruct((d_ff, D), BF16)]
        side_bytes += 3 * _nbytes((D, tf), BF16)
    limit = _vmem_limit(
        _nbytes((tm, D), F32), 3 * _nbytes((D, tf), w_dtype),
        _nbytes((tm, D), h_dtype), _nbytes((tm, D), F32) if emit_x else 0, side_bytes,
        scratch=_nbytes((tm, D), BF16) + _nbytes((tm, D), F32))
    return pl.pallas_call(
        functools.partial(_ffn_body, emit_x=emit_x, n_side=len(side_casts), keep_weights=keep_weights),
        grid=grid, in_specs=in_specs, out_specs=out_specs, out_shape=out_shape,
        scratch_shapes=[pltpu.VMEM((tm, D), BF16), pltpu.VMEM((tm, D), F32)],
        compiler_params=pltpu.CompilerParams(
            dimension_semantics=("arbitrary", "arbitrary"), vmem_limit_bytes=limit),
        name="ffn_mid" if emit_x else "ffn_final",
    )(x, g_in.reshape(1, D), w_gate, w_up, w_dn, g_out.reshape(1, D), *side_casts)


def _sigmoid(x):
    return 1.0 / (1.0 + jnp.exp(-x))


def _mm_body(x_ref, w_ref, *rest, out_scale, has_residual, has_tail, n_side):
    rest = list(rest)
    xt_ref = rest.pop(0) if has_tail else None
    r_ref = rest.pop(0) if has_residual else None
    rt_ref = rest.pop(0) if has_residual and has_tail else None
    side_in, rest = rest[:n_side], rest[n_side:]
    o_ref = rest.pop(0)
    ot_ref = rest.pop(0) if has_tail else None
    side_out, wb_scr = rest[:n_side], rest[n_side]
    i = pl.program_id(1)

    @pl.when(i == 0)
    def _():
        wb_scr[...] = w_ref[...].astype(BF16)

    def project(x, r, o):
        y = _dot(x[...], wb_scr[...])
        if out_scale is not None:
            y = y * out_scale
        if has_residual:
            y = r[...] + y
        o[...] = y.astype(o.dtype)

    if has_tail:
        pl.when(i == 0)(lambda: project(xt_ref, rt_ref, ot_ref))
        pl.when(i > 0)(lambda: project(x_ref, r_ref, o_ref))
    else:
        project(x_ref, r_ref, o_ref)
    for src, dst in zip(side_in, side_out):
        dst[...] = src[...].astype(dst.dtype)


def _mm_grid(n_tokens, n):
    tm, tn = min(MM_TM, n_tokens), min(MM_TN, n)
    assert n_tokens % tm == 0 and n % tn == 0
    return n // tn, n_tokens // tm


def _matmul(x, w, col0, n, out_dtype, residual=None, out_scale=None, side_casts=(), tail=None):
    T, K = x.shape
    n_j, n_main = _mm_grid(T, n)
    tm, tn = T // n_main, n // n_j
    assert col0 % tn == 0
    c0 = col0 // tn
    first = 0 if tail is None else 1
    main = lambda i: jnp.maximum(i - first, 0)
    in_specs = [
        pl.BlockSpec((tm, K), lambda j, i: (main(i), 0)),
        pl.BlockSpec((K, tn), lambda j, i: (0, j + c0)),
    ]
    args = [x, w]
    blocks = [_nbytes((tm, K), x.dtype), _nbytes((K, tn), w.dtype), _nbytes((tm, tn), out_dtype)]
    out_specs = [pl.BlockSpec((tm, tn), lambda j, i: (main(i), j))]
    out_shape = [jax.ShapeDtypeStruct((T, n), out_dtype)]
    if tail is not None:
        x_tail, r_tail = tail
        t_tail = x_tail.shape[0]
        assert t_tail <= tm and (r_tail is None) == (residual is None)
        in_specs.append(pl.BlockSpec((t_tail, K), lambda j, i: (0, 0)))
        args.append(x_tail)
        out_specs.append(pl.BlockSpec((t_tail, tn), lambda j, i: (0, j)))
        out_shape.append(jax.ShapeDtypeStruct((t_tail, n), out_dtype))
        blocks += [_nbytes((t_tail, K), x.dtype), _nbytes((t_tail, tn), out_dtype)]
    if residual is not None:
        in_specs.append(pl.BlockSpec((tm, tn), lambda j, i: (main(i), j)))
        args.append(residual)
        blocks.append(_nbytes((tm, tn), F32))
        if tail is not None:
            in_specs.append(pl.BlockSpec((t_tail, tn), lambda j, i: (0, j)))
            args.append(r_tail)
            blocks.append(_nbytes((t_tail, tn), F32))
    for a in side_casts:
        blk, imap = _grid_slab(a.shape, n_j, n_main)
        slab_map = lambda j, i, imap=imap: imap(j, main(i))
        in_specs.append(pl.BlockSpec(blk, slab_map))
        args.append(a)
        out_specs.append(pl.BlockSpec(blk, slab_map))
        out_shape.append(jax.ShapeDtypeStruct(a.shape, BF16))
        blocks.append(_nbytes(blk, F32) + _nbytes(blk, BF16))
    return pl.pallas_call(
        functools.partial(_mm_body, out_scale=out_scale, has_residual=residual is not None,
                          has_tail=tail is not None, n_side=len(side_casts)),
        grid=(n_j, n_main + (tail is not None)),
        in_specs=in_specs, out_specs=out_specs, out_shape=out_shape,
        scratch_shapes=[pltpu.VMEM((K, tn), BF16)],
        compiler_params=pltpu.CompilerParams(
            dimension_semantics=("arbitrary", "arbitrary"),
            vmem_limit_bytes=_vmem_limit(*blocks, scratch=_nbytes((K, tn), BF16))),
        name="proj",
    )(*args)


def _tri(bk):
    j = lax.broadcasted_iota(jnp.int32, (bk, bk), 0)
    s = lax.broadcasted_iota(jnp.int32, (bk, bk), 1)
    return (j >= s).astype(BF16)


LOG2E = 1.4426950408889634
Q_SCALE = HEAD_DIM ** -0.5 * LOG2E


def _softplus2(z2):
    return jnp.maximum(z2, 0.0) + jnp.log(1.0 + jnp.exp2(-jnp.abs(z2))) * LOG2E


def _cumsum_mxu(sp, tri):
    hi = sp.astype(BF16)
    lo = (sp - hi.astype(F32)).astype(BF16)
    if sp.shape[1] % 128 == 0:
        return _dot(jnp.concatenate([hi, lo], axis=1), jnp.concatenate([tri, tri], axis=0))
    return _dot(hi, tri) + _dot(lo, tri)


def _sb_weights(z2, tri, run, masked):
    sp = _softplus2(z2)
    if masked is not None:
        sp = masked(sp)
    tot = _cumsum_mxu(sp, tri)
    if run is not None:
        tot = tot + jnp.concatenate([run] * (z2.shape[1] // run.shape[1]), axis=1)
    a = jnp.exp2(z2 - tot)
    if masked is not None:
        a = masked(a)
    return a.astype(BF16), jnp.sum(sp, axis=1, keepdims=True)


def _attn_prompt_body(q_ref, k_ref, v_ref, tri_ref, o_ref, run_scr, acc_scr):
    blk = ATT_BLK
    seq = q_ref.shape[0]
    nb = seq // blk
    tri = tri_ref[...]
    heads = [slice(h * HEAD_DIM, (h + 1) * HEAD_DIM) for h in range(ATT_HEADS_PER_STEP)]
    block = lambda b: slice(b * blk, (b + 1) * blk)

    def key_block(ref, b, cols):
        return ref[block(b), cols].astype(BF16)

    def stacked_scores(cols, shift):
        return jnp.concatenate([_dot_nt(q_ref[block(b), cols], key_block(k_ref, b - shift, cols))
                                for b in range(shift, nb)], axis=0)

    def stacked_values(a, cols, shift):
        return jnp.concatenate([_dot(a[block(b - shift), :], key_block(v_ref, b - shift, cols))
                                for b in range(shift, nb)], axis=0)

    t = lax.broadcasted_iota(jnp.int32, (seq, blk), 0) % blk
    s = lax.broadcasted_iota(jnp.int32, (seq, blk), 1)
    causal = s < t
    run, acc = [], []
    for cols in heads:
        a, tot = _sb_weights(stacked_scores(cols, 0), tri, None, lambda x: jnp.where(causal, x, 0.0))
        run.append(jnp.broadcast_to(tot, (seq, HEAD_DIM)))
        acc.append(stacked_values(a, cols, 0))
    if nb > 1:
        for i, cols in enumerate(heads):
            a, tot = _sb_weights(stacked_scores(cols, 1), tri, run[i][blk:], None)
            run[i] = jnp.concatenate([run[i][:blk], run[i][blk:] + tot], axis=0)
            acc[i] = jnp.concatenate([acc[i][:blk], acc[i][blk:] + stacked_values(a, cols, 1)], axis=0)
    for i, cols in enumerate(heads):
        o_ref[:, cols] = acc[i].astype(o_ref.dtype)
    if nb <= 2:
        return

    far = 2 * blk
    alive = None
    for i in range(len(heads)):
        run_scr[i] = run[i][far:]
        acc_scr[i] = acc[i][far:]
        lowest = jnp.min(run[i][far:])
        alive = lowest if alive is None else jnp.minimum(alive, lowest)

    @pl.when(alive <= ATT_DEAD_LOG2)
    def _():
        run = [run_scr[i] for i in range(len(heads))]
        acc = [acc_scr[i] for i in range(len(heads))]
        for c in range(nb - 3, -1, -1):
            top = c * blk
            for i, cols in enumerate(heads):
                z2 = _dot_nt(q_ref[far + top:, cols], key_block(k_ref, c, cols))
                a, tot = _sb_weights(z2, tri, run[i][top:], None)
                pv = _dot(a, key_block(v_ref, c, cols))
                run[i] = jnp.concatenate([run[i][:top], run[i][top:] + tot], axis=0) if top else run[i] + tot
                acc[i] = jnp.concatenate([acc[i][:top], acc[i][top:] + pv], axis=0) if top else acc[i] + pv
        for i, cols in enumerate(heads):
            o_ref[far:, cols] = acc[i].astype(o_ref.dtype)


def _attn_prompt(q, k, v, batch, seq):
    blk = ATT_BLK
    assert seq % blk == 0 and N_HEADS % ATT_HEADS_PER_STEP == 0
    hd = HEAD_DIM
    width = ATT_HEADS_PER_STEP * hd
    spec = pl.BlockSpec((seq, width), lambda b, h: (b, h))
    return pl.pallas_call(
        _attn_prompt_body, grid=(batch, N_HEADS // ATT_HEADS_PER_STEP),
        in_specs=[spec, spec, spec, pl.BlockSpec((blk, blk), lambda b, h: (0, 0))],
        out_specs=spec,
        out_shape=jax.ShapeDtypeStruct(q.shape, BF16),
        scratch_shapes=[pltpu.VMEM((ATT_HEADS_PER_STEP, max(seq - 2 * blk, 8), hd), F32)] * 2,
        compiler_params=pltpu.CompilerParams(
            dimension_semantics=("parallel", "parallel"),
            vmem_limit_bytes=_vmem_limit(2 * _nbytes((seq, width), F32), 2 * _nbytes((seq, width), BF16),
                                         scratch=8 * ATT_HEADS_PER_STEP * _nbytes((seq, blk), F32))),
        name="attn_prompt",
    )(q, k, v, _tri(blk))


def _head_cols(h):
    return slice(h * HEAD_DIM, (h + 1) * HEAD_DIM)


def _sample_scores(q_ref, load_k):
    return jnp.concatenate([_dot_nt(q_ref[:, _head_cols(h)], load_k(h)) for h in range(N_HEADS)], axis=0)


def _sample_values(a, load_v, n_new):
    return jnp.concatenate([_dot(a[h * n_new:(h + 1) * n_new, :], load_v(h)) for h in range(N_HEADS)], axis=0)


def _sample_cache_sweep(q_ref, kc_ref, vc_ref, tri, run, acc, n_new):
    bk = ATT_BLK
    for sb in range(kc_ref.shape[0] // (bk * N_HEADS) - 1, -1, -1):
        def head_rows_of(ref, h):
            return ref[pl.ds(sb * bk * N_HEADS + h, bk, stride=N_HEADS), :].astype(BF16)
        z = _sample_scores(q_ref, lambda h: head_rows_of(kc_ref, h))
        a, tot = _sb_weights(z, tri, run, None)
        acc = acc + _sample_values(a, lambda h: head_rows_of(vc_ref, h), n_new)
        run = run + tot
    return run, acc


def _sample_store(o_ref, acc, n_new):
    for h in range(N_HEADS):
        o_ref[:, _head_cols(h)] = acc[h * n_new:(h + 1) * n_new, :].astype(o_ref.dtype)


def _attn_sample_near_body(q_ref, kn_ref, vn_ref, kc_ref, vc_ref, tri_ref, trin_ref,
                           o_ref, run_ref, acc_ref, alive_ref, *, n_new):
    rows = N_HEADS * n_new
    z = _sample_scores(q_ref, lambda h: kn_ref[:, _head_cols(h)].astype(BF16))
    t = lax.broadcasted_iota(jnp.int32, (rows, n_new), 0) % n_new
    s = lax.broadcasted_iota(jnp.int32, (rows, n_new), 1)
    a, tot = _sb_weights(z, trin_ref[...], None, lambda x: jnp.where(s < t, x, 0.0))
    acc = _sample_values(a, lambda h: vn_ref[:, _head_cols(h)].astype(BF16), n_new)
    run = jnp.broadcast_to(tot, (rows, HEAD_DIM))
    run, acc = _sample_cache_sweep(q_ref, kc_ref, vc_ref, tri_ref[...], run, acc, n_new)
    _sample_store(o_ref, acc, n_new)
    run_ref[...] = run
    acc_ref[...] = acc
    alive_ref[...] = jnp.broadcast_to((jnp.min(run) <= ATT_DEAD_LOG2).astype(jnp.int32), alive_ref.shape)


def _attn_sample_far_body(alive_ref, src_ref, q_ref, kc_ref, vc_ref, tri_ref, run_ref, acc_ref, near_ref,
                          o_ref, *, n_new):
    del src_ref
    b = pl.program_id(0)

    @pl.when(alive_ref[b] == 0)
    def _():
        o_ref[...] = near_ref[...]

    @pl.when(alive_ref[b] != 0)
    def _():
        _, acc = _sample_cache_sweep(q_ref, kc_ref, vc_ref, tri_ref[...], run_ref[...], acc_ref[...], n_new)
        _sample_store(o_ref, acc, n_new)


def _attn_sample(q, k_new, v_new, cache_k, cache_v, n_streams, n_new, past_len):
    bk = ATT_BLK
    hd = HEAD_DIM
    assert past_len % bk == 0
    d_all = N_HEADS * hd
    rows = N_HEADS * n_new
    near_rows = bk * N_HEADS
    far_rows = (past_len - bk) * N_HEADS
    last_near = past_len // bk - 1

    new_spec = pl.BlockSpec((n_new, d_all), lambda b: (b, 0))
    near_spec = pl.BlockSpec((None, near_rows, hd), lambda b: (b, last_near, 0))
    state_spec = pl.BlockSpec((rows, hd), lambda b: (b, 0))
    alive_spec = pl.BlockSpec((None, 8, 128), lambda b: (b, 0, 0))
    tri = _tri(bk)
    o_near, run, acc, alive = pl.pallas_call(
        functools.partial(_attn_sample_near_body, n_new=n_new),
        grid=(n_streams,),
        in_specs=[new_spec, new_spec, new_spec, near_spec, near_spec,
                  pl.BlockSpec((bk, bk), lambda b: (0, 0)),
                  pl.BlockSpec((n_new, n_new), lambda b: (0, 0))],
        out_specs=[new_spec, state_spec, state_spec, alive_spec],
        out_shape=[jax.ShapeDtypeStruct(q.shape, BF16),
                   jax.ShapeDtypeStruct((n_streams * rows, hd), F32),
                   jax.ShapeDtypeStruct((n_streams * rows, hd), F32),
                   jax.ShapeDtypeStruct((n_streams, 8, 128), jnp.int32)],
        compiler_params=pltpu.CompilerParams(
            dimension_semantics=("arbitrary",),
            vmem_limit_bytes=_vmem_limit(2 * _nbytes((near_rows, hd), F32))),
        name="attn_sample_near",
    )(q, k_new, v_new, cache_k, cache_v, tri, _tri(n_new))
    if far_rows == 0:
        return o_near

    alive = alive[:, 0, 0]
    src = lax.cummax(jnp.where(alive > 0, jnp.arange(n_streams, dtype=jnp.int32), 0))
    row_map = lambda b, alive, src: (b, 0)
    far_spec = pl.BlockSpec((None, far_rows, hd), lambda b, alive, src: (src[b], 0, 0))
    return pl.pallas_call(
        functools.partial(_attn_sample_far_body, n_new=n_new),
        grid_spec=pltpu.PrefetchScalarGridSpec(
            num_scalar_prefetch=2, grid=(n_streams,),
            in_specs=[pl.BlockSpec((n_new, d_all), row_map), far_spec, far_spec,
                      pl.BlockSpec((bk, bk), lambda b, alive, src: (0, 0)),
                      pl.BlockSpec((rows, hd), row_map), pl.BlockSpec((rows, hd), row_map),
                      pl.BlockSpec((n_new, d_all), row_map)],
            out_specs=pl.BlockSpec((n_new, d_all), row_map)),
        out_shape=jax.ShapeDtypeStruct(q.shape, BF16),
        compiler_params=pltpu.CompilerParams(
            dimension_semantics=("arbitrary",),
            vmem_limit_bytes=_vmem_limit(2 * _nbytes((far_rows, hd), F32))),
        name="attn_sample_far",
    )(alive, src, q, cache_k, cache_v, tri, run, acc, o_near)


def _merge(cb, conv, o, ga, gb, wco_ref, wao_ref, m_ref):
    y_a = _dot((cb * conv).astype(BF16), wco_ref[...])
    y_b = _dot(o, wao_ref[...])
    m_ref[...] = (_sigmoid(ga) * y_a + _sigmoid(gb) * y_b).astype(m_ref.dtype)


def _mix_prompt_body(cb_ref, cc_ref, cx_ref, o_ref, ga_ref, gb_ref, cw_ref, wco_ref, wao_ref,
                     m_ref, hist_ref, carry_scr, *, tiles_per_seq):
    i = pl.program_id(0)
    tm = cc_ref.shape[0]

    @pl.when(i % tiles_per_seq == 0)
    def _():
        carry_scr[...] = jnp.zeros_like(carry_scr)

    u = cc_ref[...] * cx_ref[...]
    row = lax.broadcasted_iota(jnp.int32, u.shape, 0)
    p1 = carry_scr[7:8, :]
    p2 = carry_scr[6:7, :]
    u1 = jnp.where(row == 0, p1, pltpu.roll(u, 1, axis=0))
    u2 = jnp.where(row == 0, p2, jnp.where(row == 1, p1, pltpu.roll(u, 2, axis=0)))
    conv = u2 * cw_ref[0:1, :] + u1 * cw_ref[1:2, :] + u * cw_ref[2:3, :]
    carry_scr[...] = u[tm - 8:, :]
    hist_ref[...] = u[tm - (CONV_W - 1):, :]
    _merge(cb_ref[...], conv, o_ref[...], ga_ref[...], gb_ref[...], wco_ref, wao_ref, m_ref)


def _mix_sample_body(cb_ref, cc_ref, cx_ref, o_ref, ga_ref, gb_ref, cw_ref, wco_ref, wao_ref,
                     p1_ref, p2_ref, m_ref, u_ref, *, n_new):
    u = cc_ref[...] * cx_ref[...]
    t = lax.broadcasted_iota(jnp.int32, u.shape, 0) % n_new
    u1 = jnp.where(t == 0, p1_ref[...], pltpu.roll(u, 1, axis=0))
    u2 = jnp.where(t < 2, p2_ref[...], pltpu.roll(u, 2, axis=0))
    conv = u2 * cw_ref[0:1, :] + u1 * cw_ref[1:2, :] + u * cw_ref[2:3, :]
    u_ref[...] = u
    _merge(cb_ref[...], conv, o_ref[...], ga_ref[...], gb_ref[...], wco_ref, wao_ref, m_ref)


def _mix_prompt(pconv, o, gates, conv_w, w_co, w_ao, batch, seq):
    T = pconv.shape[0]
    C = pconv.shape[1] // 3
    D = gates.shape[1] // 2
    Da = o.shape[1]
    tm = MIX_TM
    assert seq % tm == 0
    tps = seq // tm
    row = lambda c: (lambda i: (i, c))
    const = lambda i: (0, 0)
    m, hist = pl.pallas_call(
        functools.partial(_mix_prompt_body, tiles_per_seq=tps),
        grid=(T // tm,),
        in_specs=[
            pl.BlockSpec((tm, C), row(0)), pl.BlockSpec((tm, C), row(1)), pl.BlockSpec((tm, C), row(2)),
            pl.BlockSpec((tm, Da), row(0)),
            pl.BlockSpec((tm, D), row(0)), pl.BlockSpec((tm, D), row(1)),
            pl.BlockSpec((CONV_W, C), const),
            pl.BlockSpec((C, D), const, pipeline_mode=pl.Buffered(1)),
            pl.BlockSpec((Da, D), const, pipeline_mode=pl.Buffered(1)),
        ],
        out_specs=[pl.BlockSpec((tm, D), row(0)),
                   pl.BlockSpec((None, CONV_W - 1, C), lambda i: (i // tps, 0, 0))],
        out_shape=[jax.ShapeDtypeStruct((T, D), BF16),
                   jax.ShapeDtypeStruct((batch, CONV_W - 1, C), F32)],
        scratch_shapes=[pltpu.VMEM((8, C), F32)],
        compiler_params=pltpu.CompilerParams(
            dimension_semantics=("arbitrary",),
            vmem_limit_bytes=_vmem_limit(3 * _nbytes((tm, C), F32), _nbytes((tm, Da), BF16),
                                         2 * _nbytes((tm, D), gates.dtype), _nbytes((C, D), BF16),
                                         _nbytes((Da, D), BF16), _nbytes((tm, D), BF16))),
        name="mix_prompt",
    )(pconv, pconv, pconv, o, gates, gates, conv_w, w_co, w_ao)
    return m, hist


def _mix_sample(pconv, o, gates, conv_w, w_co, w_ao, state, n_streams, n_new):
    T = pconv.shape[0]
    C = pconv.shape[1] // 3
    D = gates.shape[1] // 2
    Da = o.shape[1]
    assert T == n_streams * n_new and n_new >= CONV_W - 1
    zeros = jnp.zeros((n_streams, n_new, C), F32)
    p1 = zeros.at[:, 0].set(state[:, 1]).reshape(T, C)
    p2 = zeros.at[:, 0].set(state[:, 0]).at[:, 1].set(state[:, 1]).reshape(T, C)
    col = lambda c: (lambda i: (0, c))
    const = lambda i: (0, 0)
    m, u = pl.pallas_call(
        functools.partial(_mix_sample_body, n_new=n_new),
        grid=(1,),
        in_specs=[
            pl.BlockSpec((T, C), col(0)), pl.BlockSpec((T, C), col(1)), pl.BlockSpec((T, C), col(2)),
            pl.BlockSpec((T, Da), const),
            pl.BlockSpec((T, D), col(0)), pl.BlockSpec((T, D), col(1)),
            pl.BlockSpec((CONV_W, C), const),
            pl.BlockSpec((C, D), const), pl.BlockSpec((Da, D), const),
            pl.BlockSpec((T, C), const), pl.BlockSpec((T, C), const),
        ],
        out_specs=[pl.BlockSpec((T, D), const), pl.BlockSpec((T, C), const)],
        out_shape=[jax.ShapeDtypeStruct((T, D), BF16), jax.ShapeDtypeStruct((T, C), F32)],
        compiler_params=pltpu.CompilerParams(
            dimension_semantics=("arbitrary",),
            vmem_limit_bytes=_vmem_limit(6 * _nbytes((T, C), F32), _nbytes((T, Da), BF16),
                                         2 * _nbytes((T, D), gates.dtype), _nbytes((C, D), BF16),
                                         _nbytes((Da, D), BF16), _nbytes((T, D), BF16))),
        name="mix_sample",
    )(pconv, pconv, pconv, o, gates, gates, conv_w, w_co, w_ao, p1, p2)
    new_hist = u.reshape(n_streams, n_new, C)[:, n_new - (CONV_W - 1):]
    return m, new_hist


def _project_both(hp, hs, w, col0, n, out_dtype, residuals=(None, None), **kw):
    if hs.shape[0] <= min(MM_TM, hp.shape[0]):
        out_p, out_s, *side = _matmul(hp, w, col0, n, out_dtype, residual=residuals[0],
                                      tail=(hs, residuals[1]), **kw)
    else:
        out_p, *side = _matmul(hp, w, col0, n, out_dtype, residual=residuals[0], **kw)
        out_s, = _matmul(hs, w, col0, n, out_dtype, residual=residuals[1],
                         out_scale=kw.get("out_scale"))
    return out_p, out_s, side


def _mixers(xp1, hp, xs1, hs, w, *, prompt_shape, sample_shape, sample_state):
    C = w["conv_w"].shape[1]
    Da = N_HEADS * HEAD_DIM
    D = xp1.shape[1]
    w_in = w["w_in"]
    pconv_p, pconv_s, _ = _project_both(hp, hs, w_in, 0, 3 * C, F32)
    q_p, q_s, _ = _project_both(hp, hs, w_in, 3 * C, Da, BF16, out_scale=Q_SCALE)
    k_p, k_s, _ = _project_both(hp, hs, w_in, 3 * C + Da, Da, F32)
    v_p, v_s, _ = _project_both(hp, hs, w_in, 3 * C + 2 * Da, Da, F32)
    merge_f32 = (w["w_co"], w["w_ao"])
    fuse = all(_grid_slab(a.shape, *_mm_grid(hp.shape[0], 2 * D)) for a in merge_f32)
    gates_p, gates_s, merge_w = _project_both(hp, hs, w_in, 3 * C + 3 * Da, 2 * D, F32,
                                              side_casts=merge_f32 if fuse else ())
    w_co, w_ao = merge_w if fuse else [a.astype(BF16) for a in merge_f32]

    B, T = prompt_shape
    o_p = _attn_prompt(q_p, k_p, v_p, B, T)
    m_p, hist_p = _mix_prompt(pconv_p, o_p, gates_p, w["conv_w"], w_co, w_ao, B, T)

    S, n_new = sample_shape
    conv_state, cache_k, cache_v, past_len = sample_state
    o_s = _attn_sample(q_s, k_s, v_s, cache_k.reshape(S, past_len * N_HEADS, HEAD_DIM),
                       cache_v.reshape(S, past_len * N_HEADS, HEAD_DIM), S, n_new, past_len)
    m_s, hist_s = _mix_sample(pconv_s, o_s, gates_s, w["conv_w"], w_co, w_ao, conv_state, S, n_new)

    x2_p, x2_s, _ = _project_both(m_p, m_s, w["w_o"], 0, D, F32, residuals=(xp1, xs1))
    return (x2_p, hist_p, k_p, v_p), (x2_s, hist_s, k_s, v_s)


def kernel(x_prompt, x_sample, cache_k, cache_v, state_conv, norm_ffn1, ffn1_w_gate_up, ffn1_w_down,
           norm_mix, w_in, conv_w, w_conv_out, w_attn_out, w_o, norm_ffn2, ffn2_w_gate_up,
           ffn2_w_down, norm_final):
    depth = w_in.shape[0]
    B, T, D = x_prompt.shape
    S, n_new, _ = x_sample.shape
    past_len = cache_k.shape[2]
    xp = x_prompt.reshape(B * T, D)
    xs = x_sample.reshape(S * n_new, D)
    outs = [[] for _ in range(6)]
    for l in range(depth):
        w = {"w_in": w_in[l], "conv_w": conv_w[l], "w_co": w_conv_out[l], "w_ao": w_attn_out[l],
             "w_o": w_o[l]}
        ffn1 = dict(emit_x=True, h_dtype=BF16)
        d_ff = ffn1_w_down.shape[1]
        if _ffn_grid(S * n_new, d_ff)[0] == 1:
            xs1, hs, w_gate1, w_up1, w_dn1 = _ffn(xs, norm_ffn1[l], ffn1_w_gate_up[l], ffn1_w_down[l],
                                                  norm_mix[l], keep_weights=True, **ffn1)
            w_gu1 = (w_gate1, w_up1)
        else:
            w_gu1, w_dn1 = ffn1_w_gate_up[l].astype(BF16), ffn1_w_down[l].astype(BF16)
            xs1, hs = _ffn(xs, norm_ffn1[l], w_gu1, w_dn1, norm_mix[l], **ffn1)
        ffn2_f32 = (ffn2_w_gate_up[l], ffn2_w_down[l])
        fuse = all(_grid_slab(a.shape, *_ffn_grid(B * T, d_ff)) for a in ffn2_f32)
        xp1, hp, *ffn2_bf = _ffn(xp, norm_ffn1[l], w_gu1, w_dn1, norm_mix[l],
                                 side_casts=ffn2_f32 if fuse else (), **ffn1)
        if not fuse:
            ffn2_bf = [a.astype(BF16) for a in ffn2_f32]
        (xp2, c_p, k_p, v_p), (xs2, c_s, k_s, v_s) = _mixers(
            xp1, hp, xs1, hs, w, prompt_shape=(B, T), sample_shape=(S, n_new),
            sample_state=(state_conv[l], cache_k[l], cache_v[l], past_len))
        w2 = (norm_ffn2[l], *ffn2_bf, norm_final)
        if l == depth - 1:
            xp = _ffn(xp2, *w2, emit_x=False, h_dtype=F32)[0]
            xs = _ffn(xs2, *w2, emit_x=False, h_dtype=F32)[0]
        else:
            xp = _ffn(xp2, *w2, emit_x=True, h_dtype=BF16)[0]
            xs = _ffn(xs2, *w2, emit_x=True, h_dtype=BF16)[0]
        for lst, val in zip(outs, (k_p.reshape(B, T, N_HEADS, HEAD_DIM), v_p.reshape(B, T, N_HEADS, HEAD_DIM),
                                   c_p, k_s.reshape(S, n_new, N_HEADS, HEAD_DIM),
                                   v_s.reshape(S, n_new, N_HEADS, HEAD_DIM), c_s)):
            lst.append(val)
    y_prompt = xp.reshape(B, T, D)
    y_sample = xs.reshape(S, n_new, D)
    return (y_prompt, y_sample) + tuple(jnp.stack(o, axis=0) for o in outs)
```

```python
import functools

import jax
import jax.numpy as jnp
from jax import lax
from jax.experimental import pallas as pl
from jax.experimental.pallas import tpu as pltpu

N_HEADS = 8
HEAD_DIM = 128
CONV_W = 3
EPS = 1e-6
FFN_SCALE = 0.5

F32 = jnp.float32
BF16 = jnp.bfloat16

V7X_VMEM_LIMIT_CAP = 60 * 1024 * 1024
LANES = 128
F32_TILE_ROWS = 8
BF16_TILE_ROWS = 16

FFN_TM = 512
FFN_TF = 512
FFN_TF_F32 = 256
MM_TM = 1024
MM_TN = 1024
MIX_TM = 512
ATT_BLK = 256
ATT_HEADS_PER_STEP = 2
ATT_DEAD_LOG2 = 160.0


def _vmem_limit(*block_bytes, scratch=0):
    est = 2 * sum(block_bytes) + scratch
    return int(min(V7X_VMEM_LIMIT_CAP, max(2 * est, 16 * 1024 * 1024)))


def _nbytes(shape, dtype):
    n = 1
    for s in shape:
        n *= s
    return n * jnp.dtype(dtype).itemsize


def _rmsnorm(x, g):
    ms = jnp.mean(x * x, axis=-1, keepdims=True)
    return x * lax.rsqrt(ms + EPS) * g


def _dot(a, b):
    return jnp.dot(a, b, preferred_element_type=F32)


def _dot_nt(a, b):
    return lax.dot_general(a, b, (((1,), (1,)), ((), ())), preferred_element_type=F32)


def _ffn_body(x_ref, gin_ref, wg_ref, wu_ref, wd_ref, gout_ref, *rest, emit_x, n_side, keep_weights):
    side_in, rest = rest[:n_side], rest[n_side:]
    if emit_x:
        xo_ref, ho_ref = rest[:2]
        rest = rest[2:]
    else:
        ho_ref = rest[0]
        rest = rest[1:]
    side_out, rest = rest[:n_side], rest[n_side:]
    if keep_weights:
        weight_out, rest = rest[:3], rest[3:]
    h_scr, acc_scr = rest
    j = pl.program_id(1)

    @pl.when(j == 0)
    def _():
        h_scr[...] = _rmsnorm(x_ref[...], gin_ref[...]).astype(BF16)
        acc_scr[...] = jnp.zeros_like(acc_scr)

    wg, wu, wd = wg_ref[...], wu_ref[...], wd_ref[...]
    if keep_weights:
        wg, wu, wd = wg.astype(BF16), wu.astype(BF16), wd.astype(BF16)
        for dst, tile in zip(weight_out, (wg, wu, wd)):
            dst[...] = tile

    h = h_scr[...]
    g = _dot(h, wg)
    u = _dot(h, wu)
    act = (g * (1.0 / (1.0 + jnp.exp(-g))) * u).astype(BF16)
    acc_scr[...] += _dot(act, wd)

    for src, dst in zip(side_in, side_out):
        dst[...] = src[...].astype(dst.dtype)

    @pl.when(j == pl.num_programs(1) - 1)
    def _():
        xn = x_ref[...] + FFN_SCALE * acc_scr[...]
        if emit_x:
            xo_ref[...] = xn
        ho_ref[...] = _rmsnorm(xn, gout_ref[...]).astype(ho_ref.dtype)


def _grid_slab(shape, n_i, n_j):
    rows, cols = shape
    for nr, nc, imap in ((n_i, n_j, lambda i, j: (i, j)), (n_j, n_i, lambda i, j: (j, i))):
        if (rows % nr == 0 and cols % nc == 0 and (rows // nr) % BF16_TILE_ROWS == 0
                and (cols // nc) % LANES == 0):
            return (rows // nr, cols // nc), imap
    return None


def _ffn_grid(n_tokens, d_ff, keep_weights=False):
    tm = min(FFN_TM, n_tokens)
    tf = FFN_TF_F32 if keep_weights else FFN_TF
    assert n_tokens % tm == 0 and d_ff % tf == 0
    return n_tokens // tm, d_ff // tf


def _ffn(x, g_in, w_gu, w_dn, g_out, *, emit_x, h_dtype, side_casts=(), keep_weights=False):
    T, D = x.shape
    d_ff = w_dn.shape[0]
    grid = _ffn_grid(T, d_ff, keep_weights)
    tm, tf = T // grid[0], d_ff // grid[1]
    n_f = grid[1]
    w_gate, w_up = w_gu if isinstance(w_gu, tuple) else (w_gu, w_gu)
    up0 = 0 if isinstance(w_gu, tuple) else n_f
    w_dtype = w_dn.dtype
    assert (w_dtype == F32) == keep_weights and (grid[0] == 1 or not keep_weights)
    in_specs = [
        pl.BlockSpec((tm, D), lambda i, j: (i, 0)),
        pl.BlockSpec((1, D), lambda i, j: (0, 0)),
        pl.BlockSpec((D, tf), lambda i, j: (0, j)),
        pl.BlockSpec((D, tf), lambda i, j: (0, j + up0)),
        pl.BlockSpec((tf, D), lambda i, j: (j, 0)),
        pl.BlockSpec((1, D), lambda i, j: (0, 0)),
    ]
    out_shape = [jax.ShapeDtypeStruct((T, D), h_dtype)]
    out_specs = [pl.BlockSpec((tm, D), lambda i, j: (i, 0))]
    if emit_x:
        out_shape = [jax.ShapeDtypeStruct((T, D), F32)] + out_shape
        out_specs = [pl.BlockSpec((tm, D), lambda i, j: (i, 0))] + out_specs
    side_bytes = 0
    for a in side_casts:
        blk, imap = _grid_slab(a.shape, *grid)
        in_specs.append(pl.BlockSpec(blk, imap))
        out_specs.append(pl.BlockSpec(blk, imap))
        out_shape.append(jax.ShapeDtypeStruct(a.shape, BF16))
        side_bytes += _nbytes(blk, F32) + _nbytes(blk, BF16)
    if keep_weights:
        out_specs += [pl.BlockSpec((D, tf), lambda i, j: (0, j)),
                      pl.BlockSpec((D, tf), lambda i, j: (0, j)),
                      pl.BlockSpec((tf, D), lambda i, j: (j, 0))]
        out_shape += [jax.ShapeDtypeStruct((D, d_ff), BF16), jax.ShapeDtypeStruct((D, d_ff), BF16),
                      jax.ShapeDtypeStruct((d_ff, D), BF16)]
        side_bytes += 3 * _nbytes((D, tf), BF16)
    limit = _vmem_limit(
        _nbytes((tm, D), F32), 3 * _nbytes((D, tf), w_dtype),
        _nbytes((tm, D), h_dtype), _nbytes((tm, D), F32) if emit_x else 0, side_bytes,
        scratch=_nbytes((tm, D), BF16) + _nbytes((tm, D), F32))
    return pl.pallas_call(
        functools.partial(_ffn_body, emit_x=emit_x, n_side=len(side_casts), keep_weights=keep_weights),
        grid=grid, in_specs=in_specs, out_specs=out_specs, out_shape=out_shape,
        scratch_shapes=[pltpu.VMEM((tm, D), BF16), pltpu.VMEM((tm, D), F32)],
        compiler_params=pltpu.CompilerParams(
            dimension_semantics=("arbitrary", "arbitrary"), vmem_limit_bytes=limit),
        name="ffn_mid" if emit_x else "ffn_final",
    )(x, g_in.reshape(1, D), w_gate, w_up, w_dn, g_out.reshape(1, D), *side_casts)


def _sigmoid(x):
    return 1.0 / (1.0 + jnp.exp(-x))


def _mm_body(x_ref, w_ref, *rest, out_scale, has_residual, has_tail, n_side):
    rest = list(rest)
    xt_ref = rest.pop(0) if has_tail else None
    r_ref = rest.pop(0) if has_residual else None
    rt_ref = rest.pop(0) if has_residual and has_tail else None
    side_in, rest = rest[:n_side], rest[n_side:]
    o_ref = rest.pop(0)
    ot_ref = rest.pop(0) if has_tail else None
    side_out, wb_scr = rest[:n_side], rest[n_side]
    i = pl.program_id(1)

    @pl.when(i == 0)
    def _():
        wb_scr[...] = w_ref[...].astype(BF16)

    def project(x, r, o):
        y = _dot(x[...], wb_scr[...])
        if out_scale is not None:
            y = y * out_scale
        if has_residual:
            y = r[...] + y
        o[...] = y.astype(o.dtype)

    if has_tail:
        pl.when(i == 0)(lambda: project(xt_ref, rt_ref, ot_ref))
        pl.when(i > 0)(lambda: project(x_ref, r_ref, o_ref))
    else:
        project(x_ref, r_ref, o_ref)
    for src, dst in zip(side_in, side_out):
        dst[...] = src[...].astype(dst.dtype)


def _mm_grid(n_tokens, n):
    tm, tn = min(MM_TM, n_tokens), min(MM_TN, n)
    assert n_tokens % tm == 0 and n % tn == 0
    return n // tn, n_tokens // tm


def _matmul(x, w, col0, n, out_dtype, residual=None, out_scale=None, side_casts=(), tail=None):
    T, K = x.shape
    n_j, n_main = _mm_grid(T, n)
    tm, tn = T // n_main, n // n_j
    assert col0 % tn == 0
    c0 = col0 // tn
    first = 0 if tail is None else 1
    main = lambda i: jnp.maximum(i - first, 0)
    in_specs = [
        pl.BlockSpec((tm, K), lambda j, i: (main(i), 0)),
        pl.BlockSpec((K, tn), lambda j, i: (0, j + c0)),
    ]
    args = [x, w]
    blocks = [_nbytes((tm, K), x.dtype), _nbytes((K, tn), w.dtype), _nbytes((tm, tn), out_dtype)]
    out_specs = [pl.BlockSpec((tm, tn), lambda j, i: (main(i), j))]
    out_shape = [jax.ShapeDtypeStruct((T, n), out_dtype)]
    if tail is not None:
        x_tail, r_tail = tail
        t_tail = x_tail.shape[0]
        assert t_tail <= tm and (r_tail is None) == (residual is None)
        in_specs.append(pl.BlockSpec((t_tail, K), lambda j, i: (0, 0)))
        args.append(x_tail)
        out_specs.append(pl.BlockSpec((t_tail, tn), lambda j, i: (0, j)))
        out_shape.append(jax.ShapeDtypeStruct((t_tail, n), out_dtype))
        blocks += [_nbytes((t_tail, K), x.dtype), _nbytes((t_tail, tn), out_dtype)]
    if residual is not None:
        in_specs.append(pl.BlockSpec((tm, tn), lambda j, i: (main(i), j)))
        args.append(residual)
        blocks.append(_nbytes((tm, tn), F32))
        if tail is not None:
            in_specs.append(pl.BlockSpec((t_tail, tn), lambda j, i: (0, j)))
            args.append(r_tail)
            blocks.append(_nbytes((t_tail, tn), F32))
    for a in side_casts:
        blk, imap = _grid_slab(a.shape, n_j, n_main)
        slab_map = lambda j, i, imap=imap: imap(j, main(i))
        in_specs.append(pl.BlockSpec(blk, slab_map))
        args.append(a)
        out_specs.append(pl.BlockSpec(blk, slab_map))
        out_shape.append(jax.ShapeDtypeStruct(a.shape, BF16))
        blocks.append(_nbytes(blk, F32) + _nbytes(blk, BF16))
    return pl.pallas_call(
        functools.partial(_mm_body, out_scale=out_scale, has_residual=residual is not None,
                          has_tail=tail is not None, n_side=len(side_casts)),
        grid=(n_j, n_main + (tail is not None)),
        in_specs=in_specs, out_specs=out_specs, out_shape=out_shape,
        scratch_shapes=[pltpu.VMEM((K, tn), BF16)],
        compiler_params=pltpu.CompilerParams(
            dimension_semantics=("arbitrary", "arbitrary"),
            vmem_limit_bytes=_vmem_limit(*blocks, scratch=_nbytes((K, tn), BF16))),
        name="proj",
    )(*args)


def _tri(bk):
    j = lax.broadcasted_iota(jnp.int32, (bk, bk), 0)
    s = lax.broadcasted_iota(jnp.int32, (bk, bk), 1)
    return (j >= s).astype(BF16)


LOG2E = 1.4426950408889634
Q_SCALE = HEAD_DIM ** -0.5 * LOG2E


def _softplus2(z2):
    return jnp.maximum(z2, 0.0) + jnp.log(1.0 + jnp.exp2(-jnp.abs(z2))) * LOG2E


def _cumsum_mxu(sp, tri):
    hi = sp.astype(BF16)
    lo = (sp - hi.astype(F32)).astype(BF16)
    if sp.shape[1] % 128 == 0:
        return _dot(jnp.concatenate([hi, lo], axis=1), jnp.concatenate([tri, tri], axis=0))
    return _dot(hi, tri) + _dot(lo, tri)


def _sb_weights(z2, tri, run, masked):
    sp = _softplus2(z2)
    if masked is not None:
        sp = masked(sp)
    tot = _cumsum_mxu(sp, tri)
    if run is not None:
        tot = tot + jnp.concatenate([run] * (z2.shape[1] // run.shape[1]), axis=1)
    a = jnp.exp2(z2 - tot)
    if masked is not None:
        a = masked(a)
    return a.astype(BF16), jnp.sum(sp, axis=1, keepdims=True)


def _attn_prompt_body(q_ref, k_ref, v_ref, tri_ref, o_ref, run_scr, acc_scr):
    blk = ATT_BLK
    seq = q_ref.shape[0]
    nb = seq // blk
    tri = tri_ref[...]
    heads = [slice(h * HEAD_DIM, (h + 1) * HEAD_DIM) for h in range(ATT_HEADS_PER_STEP)]
    block = lambda b: slice(b * blk, (b + 1) * blk)

    def key_block(ref, b, cols):
        return ref[block(b), cols].astype(BF16)

    def stacked_scores(cols, shift):
        return jnp.concatenate([_dot_nt(q_ref[block(b), cols], key_block(k_ref, b - shift, cols))
                                for b in range(shift, nb)], axis=0)

    def stacked_values(a, cols, shift):
        return jnp.concatenate([_dot(a[block(b - shift), :], key_block(v_ref, b - shift, cols))
                                for b in range(shift, nb)], axis=0)

    t = lax.broadcasted_iota(jnp.int32, (seq, blk), 0) % blk
    s = lax.broadcasted_iota(jnp.int32, (seq, blk), 1)
    causal = s < t
    run, acc = [], []
    for cols in heads:
        a, tot = _sb_weights(stacked_scores(cols, 0), tri, None, lambda x: jnp.where(causal, x, 0.0))
        run.append(jnp.broadcast_to(tot, (seq, HEAD_DIM)))
        acc.append(stacked_values(a, cols, 0))
    if nb > 1:
        for i, cols in enumerate(heads):
            a, tot = _sb_weights(stacked_scores(cols, 1), tri, run[i][blk:], None)
            run[i] = jnp.concatenate([run[i][:blk], run[i][blk:] + tot], axis=0)
            acc[i] = jnp.concatenate([acc[i][:blk], acc[i][blk:] + stacked_values(a, cols, 1)], axis=0)
    for i, cols in enumerate(heads):
        o_ref[:, cols] = acc[i].astype(o_ref.dtype)
    if nb <= 2:
        return

    far = 2 * blk
    alive = None
    for i in range(len(heads)):
        run_scr[i] = run[i][far:]
        acc_scr[i] = acc[i][far:]
        lowest = jnp.min(run[i][far:])
        alive = lowest if alive is None else jnp.minimum(alive, lowest)

    @pl.when(alive <= ATT_DEAD_LOG2)
    def _():
        run = [run_scr[i] for i in range(len(heads))]
        acc = [acc_scr[i] for i in range(len(heads))]
        for c in range(nb - 3, -1, -1):
            top = c * blk
            for i, cols in enumerate(heads):
                z2 = _dot_nt(q_ref[far + top:, cols], key_block(k_ref, c, cols))
                a, tot = _sb_weights(z2, tri, run[i][top:], None)
                pv = _dot(a, key_block(v_ref, c, cols))
                run[i] = jnp.concatenate([run[i][:top], run[i][top:] + tot], axis=0) if top else run[i] + tot
                acc[i] = jnp.concatenate([acc[i][:top], acc[i][top:] + pv], axis=0) if top else acc[i] + pv
        for i, cols in enumerate(heads):
            o_ref[far:, cols] = acc[i].astype(o_ref.dtype)


def _attn_prompt(q, k, v, batch, seq):
    blk = ATT_BLK
    assert seq % blk == 0 and N_HEADS % ATT_HEADS_PER_STEP == 0
    hd = HEAD_DIM
    width = ATT_HEADS_PER_STEP * hd
    spec = pl.BlockSpec((seq, width), lambda b, h: (b, h))
    return pl.pallas_call(
        _attn_prompt_body, grid=(batch, N_HEADS // ATT_HEADS_PER_STEP),
        in_specs=[spec, spec, spec, pl.BlockSpec((blk, blk), lambda b, h: (0, 0))],
        out_specs=spec,
        out_shape=jax.ShapeDtypeStruct(q.shape, BF16),
        scratch_shapes=[pltpu.VMEM((ATT_HEADS_PER_STEP, max(seq - 2 * blk, F32_TILE_ROWS), hd), F32)] * 2,
        compiler_params=pltpu.CompilerParams(
            dimension_semantics=("parallel", "parallel"),
            vmem_limit_bytes=_vmem_limit(2 * _nbytes((seq, width), F32), 2 * _nbytes((seq, width), BF16),
                                         scratch=8 * ATT_HEADS_PER_STEP * _nbytes((seq, blk), F32))),
        name="attn_prompt",
    )(q, k, v, _tri(blk))


def _head_cols(h):
    return slice(h * HEAD_DIM, (h + 1) * HEAD_DIM)


def _sample_scores(q_ref, load_k):
    return jnp.concatenate([_dot_nt(q_ref[:, _head_cols(h)], load_k(h)) for h in range(N_HEADS)], axis=0)


def _sample_values(a, load_v, n_new):
    return jnp.concatenate([_dot(a[h * n_new:(h + 1) * n_new, :], load_v(h)) for h in range(N_HEADS)], axis=0)


def _sample_cache_sweep(q_ref, kc_ref, vc_ref, tri, run, acc, n_new):
    bk = ATT_BLK
    for sb in range(kc_ref.shape[0] // (bk * N_HEADS) - 1, -1, -1):
        def head_rows_of(ref, h):
            return ref[pl.ds(sb * bk * N_HEADS + h, bk, stride=N_HEADS), :].astype(BF16)
        z = _sample_scores(q_ref, lambda h: head_rows_of(kc_ref, h))
        a, tot = _sb_weights(z, tri, run, None)
        acc = acc + _sample_values(a, lambda h: head_rows_of(vc_ref, h), n_new)
        run = run + tot
    return run, acc


def _sample_store(o_ref, acc, n_new):
    for h in range(N_HEADS):
        o_ref[:, _head_cols(h)] = acc[h * n_new:(h + 1) * n_new, :].astype(o_ref.dtype)


def _attn_sample_near_body(q_ref, kn_ref, vn_ref, kc_ref, vc_ref, tri_ref, trin_ref,
                           o_ref, run_ref, acc_ref, alive_ref, *, n_new):
    rows = N_HEADS * n_new
    z = _sample_scores(q_ref, lambda h: kn_ref[:, _head_cols(h)].astype(BF16))
    t = lax.broadcasted_iota(jnp.int32, (rows, n_new), 0) % n_new
    s = lax.broadcasted_iota(jnp.int32, (rows, n_new), 1)
    a, tot = _sb_weights(z, trin_ref[...], None, lambda x: jnp.where(s < t, x, 0.0))
    acc = _sample_values(a, lambda h: vn_ref[:, _head_cols(h)].astype(BF16), n_new)
    run = jnp.broadcast_to(tot, (rows, HEAD_DIM))
    run, acc = _sample_cache_sweep(q_ref, kc_ref, vc_ref, tri_ref[...], run, acc, n_new)
    _sample_store(o_ref, acc, n_new)
    run_ref[...] = run
    acc_ref[...] = acc
    alive_ref[...] = jnp.broadcast_to((jnp.min(run) <= ATT_DEAD_LOG2).astype(jnp.int32), alive_ref.shape)


def _attn_sample_far_body(alive_ref, src_ref, q_ref, kc_ref, vc_ref, tri_ref, run_ref, acc_ref, near_ref,
                          o_ref, *, n_new):
    del src_ref
    b = pl.program_id(0)

    @pl.when(alive_ref[b] == 0)
    def _():
        o_ref[...] = near_ref[...]

    @pl.when(alive_ref[b] != 0)
    def _():
        _, acc = _sample_cache_sweep(q_ref, kc_ref, vc_ref, tri_ref[...], run_ref[...], acc_ref[...], n_new)
        _sample_store(o_ref, acc, n_new)


def _attn_sample(q, k_new, v_new, cache_k, cache_v, n_streams, n_new, past_len):
    bk = ATT_BLK
    hd = HEAD_DIM
    assert past_len % bk == 0
    d_all = N_HEADS * hd
    rows = N_HEADS * n_new
    near_rows = bk * N_HEADS
    far_rows = (past_len - bk) * N_HEADS
    last_near = past_len // bk - 1

    new_spec = pl.BlockSpec((n_new, d_all), lambda b: (b, 0))
    near_spec = pl.BlockSpec((None, near_rows, hd), lambda b: (b, last_near, 0))
    state_spec = pl.BlockSpec((rows, hd), lambda b: (b, 0))
    alive_spec = pl.BlockSpec((None, F32_TILE_ROWS, LANES), lambda b: (b, 0, 0))
    tri = _tri(bk)
    o_near, run, acc, alive = pl.pallas_call(
        functools.partial(_attn_sample_near_body, n_new=n_new),
        grid=(n_streams,),
        in_specs=[new_spec, new_spec, new_spec, near_spec, near_spec,
                  pl.BlockSpec((bk, bk), lambda b: (0, 0)),
                  pl.BlockSpec((n_new, n_new), lambda b: (0, 0))],
        out_specs=[new_spec, state_spec, state_spec, alive_spec],
        out_shape=[jax.ShapeDtypeStruct(q.shape, BF16),
                   jax.ShapeDtypeStruct((n_streams * rows, hd), F32),
                   jax.ShapeDtypeStruct((n_streams * rows, hd), F32),
                   jax.ShapeDtypeStruct((n_streams, F32_TILE_ROWS, LANES), jnp.int32)],
        compiler_params=pltpu.CompilerParams(
            dimension_semantics=("arbitrary",),
            vmem_limit_bytes=_vmem_limit(2 * _nbytes((near_rows, hd), F32))),
        name="attn_sample_near",
    )(q, k_new, v_new, cache_k, cache_v, tri, _tri(n_new))
    if far_rows == 0:
        return o_near

    alive = alive[:, 0, 0]
    src = lax.cummax(jnp.where(alive > 0, jnp.arange(n_streams, dtype=jnp.int32), 0))
    row_map = lambda b, alive, src: (b, 0)
    far_spec = pl.BlockSpec((None, far_rows, hd), lambda b, alive, src: (src[b], 0, 0))
    return pl.pallas_call(
        functools.partial(_attn_sample_far_body, n_new=n_new),
        grid_spec=pltpu.PrefetchScalarGridSpec(
            num_scalar_prefetch=2, grid=(n_streams,),
            in_specs=[pl.BlockSpec((n_new, d_all), row_map), far_spec, far_spec,
                      pl.BlockSpec((bk, bk), lambda b, alive, src: (0, 0)),
                      pl.BlockSpec((rows, hd), row_map), pl.BlockSpec((rows, hd), row_map),
                      pl.BlockSpec((n_new, d_all), row_map)],
            out_specs=pl.BlockSpec((n_new, d_all), row_map)),
        out_shape=jax.ShapeDtypeStruct(q.shape, BF16),
        compiler_params=pltpu.CompilerParams(
            dimension_semantics=("arbitrary",),
            vmem_limit_bytes=_vmem_limit(2 * _nbytes((far_rows, hd), F32))),
        name="attn_sample_far",
    )(alive, src, q, cache_k, cache_v, tri, run, acc, o_near)


def _merge(cb, conv, o, ga, gb, wco_ref, wao_ref, m_ref):
    y_a = _dot((cb * conv).astype(BF16), wco_ref[...])
    y_b = _dot(o, wao_ref[...])
    m_ref[...] = (_sigmoid(ga) * y_a + _sigmoid(gb) * y_b).astype(m_ref.dtype)


def _mix_prompt_body(cb_ref, cc_ref, cx_ref, o_ref, ga_ref, gb_ref, cw_ref, wco_ref, wao_ref,
                     m_ref, hist_ref, carry_scr, *, tiles_per_seq):
    i = pl.program_id(0)
    tm = cc_ref.shape[0]

    @pl.when(i % tiles_per_seq == 0)
    def _():
        carry_scr[...] = jnp.zeros_like(carry_scr)

    u = cc_ref[...] * cx_ref[...]
    row = lax.broadcasted_iota(jnp.int32, u.shape, 0)
    p1 = carry_scr[F32_TILE_ROWS - 1:F32_TILE_ROWS, :]
    p2 = carry_scr[F32_TILE_ROWS - 2:F32_TILE_ROWS - 1, :]
    u1 = jnp.where(row == 0, p1, pltpu.roll(u, 1, axis=0))
    u2 = jnp.where(row == 0, p2, jnp.where(row == 1, p1, pltpu.roll(u, 2, axis=0)))
    conv = u2 * cw_ref[0:1, :] + u1 * cw_ref[1:2, :] + u * cw_ref[2:3, :]
    carry_scr[...] = u[tm - F32_TILE_ROWS:, :]
    hist_ref[...] = u[tm - (CONV_W - 1):, :]
    _merge(cb_ref[...], conv, o_ref[...], ga_ref[...], gb_ref[...], wco_ref, wao_ref, m_ref)


def _mix_sample_body(cb_ref, cc_ref, cx_ref, o_ref, ga_ref, gb_ref, cw_ref, wco_ref, wao_ref,
                     p1_ref, p2_ref, m_ref, u_ref, *, n_new):
    u = cc_ref[...] * cx_ref[...]
    t = lax.broadcasted_iota(jnp.int32, u.shape, 0) % n_new
    u1 = jnp.where(t == 0, p1_ref[...], pltpu.roll(u, 1, axis=0))
    u2 = jnp.where(t < 2, p2_ref[...], pltpu.roll(u, 2, axis=0))
    conv = u2 * cw_ref[0:1, :] + u1 * cw_ref[1:2, :] + u * cw_ref[2:3, :]
    u_ref[...] = u
    _merge(cb_ref[...], conv, o_ref[...], ga_ref[...], gb_ref[...], wco_ref, wao_ref, m_ref)


def _mix_prompt(pconv, o, gates, conv_w, w_co, w_ao, batch, seq):
    T = pconv.shape[0]
    C = pconv.shape[1] // 3
    D = gates.shape[1] // 2
    Da = o.shape[1]
    tm = MIX_TM
    assert seq % tm == 0
    tps = seq // tm
    row = lambda c: (lambda i: (i, c))
    const = lambda i: (0, 0)
    m, hist = pl.pallas_call(
        functools.partial(_mix_prompt_body, tiles_per_seq=tps),
        grid=(T // tm,),
        in_specs=[
            pl.BlockSpec((tm, C), row(0)), pl.BlockSpec((tm, C), row(1)), pl.BlockSpec((tm, C), row(2)),
            pl.BlockSpec((tm, Da), row(0)),
            pl.BlockSpec((tm, D), row(0)), pl.BlockSpec((tm, D), row(1)),
            pl.BlockSpec((CONV_W, C), const),
            pl.BlockSpec((C, D), const, pipeline_mode=pl.Buffered(1)),
            pl.BlockSpec((Da, D), const, pipeline_mode=pl.Buffered(1)),
        ],
        out_specs=[pl.BlockSpec((tm, D), row(0)),
                   pl.BlockSpec((None, CONV_W - 1, C), lambda i: (i // tps, 0, 0))],
        out_shape=[jax.ShapeDtypeStruct((T, D), BF16),
                   jax.ShapeDtypeStruct((batch, CONV_W - 1, C), F32)],
        scratch_shapes=[pltpu.VMEM((F32_TILE_ROWS, C), F32)],
        compiler_params=pltpu.CompilerParams(
            dimension_semantics=("arbitrary",),
            vmem_limit_bytes=_vmem_limit(3 * _nbytes((tm, C), F32), _nbytes((tm, Da), BF16),
                                         2 * _nbytes((tm, D), gates.dtype), _nbytes((C, D), BF16),
                                         _nbytes((Da, D), BF16), _nbytes((tm, D), BF16))),
        name="mix_prompt",
    )(pconv, pconv, pconv, o, gates, gates, conv_w, w_co, w_ao)
    return m, hist


def _mix_sample(pconv, o, gates, conv_w, w_co, w_ao, state, n_streams, n_new):
    T = pconv.shape[0]
    C = pconv.shape[1] // 3
    D = gates.shape[1] // 2
    Da = o.shape[1]
    assert T == n_streams * n_new and n_new >= CONV_W - 1
    zeros = jnp.zeros((n_streams, n_new, C), F32)
    p1 = zeros.at[:, 0].set(state[:, 1]).reshape(T, C)
    p2 = zeros.at[:, 0].set(state[:, 0]).at[:, 1].set(state[:, 1]).reshape(T, C)
    col = lambda c: (lambda i: (0, c))
    const = lambda i: (0, 0)
    m, u = pl.pallas_call(
        functools.partial(_mix_sample_body, n_new=n_new),
        grid=(1,),
        in_specs=[
            pl.BlockSpec((T, C), col(0)), pl.BlockSpec((T, C), col(1)), pl.BlockSpec((T, C), col(2)),
            pl.BlockSpec((T, Da), const),
            pl.BlockSpec((T, D), col(0)), pl.BlockSpec((T, D), col(1)),
            pl.BlockSpec((CONV_W, C), const),
            pl.BlockSpec((C, D), const), pl.BlockSpec((Da, D), const),
            pl.BlockSpec((T, C), const), pl.BlockSpec((T, C), const),
        ],
        out_specs=[pl.BlockSpec((T, D), const), pl.BlockSpec((T, C), const)],
        out_shape=[jax.ShapeDtypeStruct((T, D), BF16), jax.ShapeDtypeStruct((T, C), F32)],
        compiler_params=pltpu.CompilerParams(
            dimension_semantics=("arbitrary",),
            vmem_limit_bytes=_vmem_limit(6 * _nbytes((T, C), F32), _nbytes((T, Da), BF16),
                                         2 * _nbytes((T, D), gates.dtype), _nbytes((C, D), BF16),
                                         _nbytes((Da, D), BF16), _nbytes((T, D), BF16))),
        name="mix_sample",
    )(pconv, pconv, pconv, o, gates, gates, conv_w, w_co, w_ao, p1, p2)
    new_hist = u.reshape(n_streams, n_new, C)[:, n_new - (CONV_W - 1):]
    return m, new_hist


def _project_both(hp, hs, w, col0, n, out_dtype, residuals=(None, None), **kw):
    if hs.shape[0] <= min(MM_TM, hp.shape[0]):
        out_p, out_s, *side = _matmul(hp, w, col0, n, out_dtype, residual=residuals[0],
                                      tail=(hs, residuals[1]), **kw)
    else:
        out_p, *side = _matmul(hp, w, col0, n, out_dtype, residual=residuals[0], **kw)
        out_s, = _matmul(hs, w, col0, n, out_dtype, residual=residuals[1],
                         out_scale=kw.get("out_scale"))
    return out_p, out_s, side


def _mixers(xp1, hp, xs1, hs, w, *, prompt_shape, sample_shape, sample_state):
    C = w["conv_w"].shape[1]
    Da = N_HEADS * HEAD_DIM
    D = xp1.shape[1]
    w_in = w["w_in"]
    pconv_p, pconv_s, _ = _project_both(hp, hs, w_in, 0, 3 * C, F32)
    q_p, q_s, _ = _project_both(hp, hs, w_in, 3 * C, Da, BF16, out_scale=Q_SCALE)
    k_p, k_s, _ = _project_both(hp, hs, w_in, 3 * C + Da, Da, F32)
    v_p, v_s, _ = _project_both(hp, hs, w_in, 3 * C + 2 * Da, Da, F32)
    merge_f32 = (w["w_co"], w["w_ao"])
    fuse = all(_grid_slab(a.shape, *_mm_grid(hp.shape[0], 2 * D)) for a in merge_f32)
    gates_p, gates_s, merge_w = _project_both(hp, hs, w_in, 3 * C + 3 * Da, 2 * D, F32,
                                              side_casts=merge_f32 if fuse else ())
    w_co, w_ao = merge_w if fuse else [a.astype(BF16) for a in merge_f32]

    B, T = prompt_shape
    o_p = _attn_prompt(q_p, k_p, v_p, B, T)
    m_p, hist_p = _mix_prompt(pconv_p, o_p, gates_p, w["conv_w"], w_co, w_ao, B, T)

    S, n_new = sample_shape
    conv_state, cache_k, cache_v, past_len = sample_state
    o_s = _attn_sample(q_s, k_s, v_s, cache_k.reshape(S, past_len * N_HEADS, HEAD_DIM),
                       cache_v.reshape(S, past_len * N_HEADS, HEAD_DIM), S, n_new, past_len)
    m_s, hist_s = _mix_sample(pconv_s, o_s, gates_s, w["conv_w"], w_co, w_ao, conv_state, S, n_new)

    x2_p, x2_s, _ = _project_both(m_p, m_s, w["w_o"], 0, D, F32, residuals=(xp1, xs1))
    return (x2_p, hist_p, k_p, v_p), (x2_s, hist_s, k_s, v_s)


def kernel(x_prompt, x_sample, cache_k, cache_v, state_conv, norm_ffn1, ffn1_w_gate_up, ffn1_w_down,
           norm_mix, w_in, conv_w, w_conv_out, w_attn_out, w_o, norm_ffn2, ffn2_w_gate_up,
           ffn2_w_down, norm_final):
    depth = w_in.shape[0]
    B, T, D = x_prompt.shape
    S, n_new, _ = x_sample.shape
    past_len = cache_k.shape[2]
    xp = x_prompt.reshape(B * T, D)
    xs = x_sample.reshape(S * n_new, D)
    outs = [[] for _ in range(6)]
    for l in range(depth):
        w = {"w_in": w_in[l], "conv_w": conv_w[l], "w_co": w_conv_out[l], "w_ao": w_attn_out[l],
             "w_o": w_o[l]}
        ffn1 = dict(emit_x=True, h_dtype=BF16)
        d_ff = ffn1_w_down.shape[1]
        if _ffn_grid(S * n_new, d_ff)[0] == 1:
            xs1, hs, w_gate1, w_up1, w_dn1 = _ffn(xs, norm_ffn1[l], ffn1_w_gate_up[l], ffn1_w_down[l],
                                                  norm_mix[l], keep_weights=True, **ffn1)
            w_gu1 = (w_gate1, w_up1)
        else:
            w_gu1, w_dn1 = ffn1_w_gate_up[l].astype(BF16), ffn1_w_down[l].astype(BF16)
            xs1, hs = _ffn(xs, norm_ffn1[l], w_gu1, w_dn1, norm_mix[l], **ffn1)
        ffn2_f32 = (ffn2_w_gate_up[l], ffn2_w_down[l])
        fuse = all(_grid_slab(a.shape, *_ffn_grid(B * T, d_ff)) for a in ffn2_f32)
        xp1, hp, *ffn2_bf = _ffn(xp, norm_ffn1[l], w_gu1, w_dn1, norm_mix[l],
                                 side_casts=ffn2_f32 if fuse else (), **ffn1)
        if not fuse:
            ffn2_bf = [a.astype(BF16) for a in ffn2_f32]
        (xp2, c_p, k_p, v_p), (xs2, c_s, k_s, v_s) = _mixers(
            xp1, hp, xs1, hs, w, prompt_shape=(B, T), sample_shape=(S, n_new),
            sample_state=(state_conv[l], cache_k[l], cache_v[l], past_len))
        w2 = (norm_ffn2[l], *ffn2_bf, norm_final)
        if l == depth - 1:
            xp = _ffn(xp2, *w2, emit_x=False, h_dtype=F32)[0]
            xs = _ffn(xs2, *w2, emit_x=False, h_dtype=F32)[0]
        else:
            xp = _ffn(xp2, *w2, emit_x=True, h_dtype=BF16)[0]
            xs = _ffn(xs2, *w2, emit_x=True, h_dtype=BF16)[0]
        for lst, val in zip(outs, (k_p.reshape(B, T, N_HEADS, HEAD_DIM), v_p.reshape(B, T, N_HEADS, HEAD_DIM),
                                   c_p, k_s.reshape(S, n_new, N_HEADS, HEAD_DIM),
                                   v_s.reshape(S, n_new, N_HEADS, HEAD_DIM), c_s)):
            lst.append(val)
    y_prompt = xp.reshape(B, T, D)
    y_sample = xs.reshape(S, n_new, D)
    return (y_prompt, y_sample) + tuple(jnp.stack(o, axis=0) for o in outs)
```

```python
import functools

import jax
import jax.numpy as jnp
from jax import lax
from jax.experimental import pallas as pl
from jax.experimental.pallas import tpu as pltpu

N_HEADS = 8
HEAD_DIM = 128
CONV_W = 3
EPS = 1e-6
FFN_SCALE = 0.5

F32 = jnp.float32
BF16 = jnp.bfloat16

V7X_VMEM_LIMIT_CAP = 60 * 1024 * 1024
LANES = 128
F32_TILE_ROWS = 8
BF16_TILE_ROWS = 16

FFN_TM = 512
FFN_TF = 512
FFN_TF_F32 = 512
MM_TM = 1024
MM_TN = 1024
MIX_TM = 512
ATT_BLK = 256
ATT_HEADS_PER_STEP = 2
ATT_DEAD_LOG2 = 160.0


def _vmem_limit(*block_bytes, scratch=0):
    est = 2 * sum(block_bytes) + scratch
    return int(min(V7X_VMEM_LIMIT_CAP, max(2 * est, 16 * 1024 * 1024)))


def _nbytes(shape, dtype):
    n = 1
    for s in shape:
        n *= s
    return n * jnp.dtype(dtype).itemsize


def _rmsnorm(x, g):
    ms = jnp.mean(x * x, axis=-1, keepdims=True)
    return x * lax.rsqrt(ms + EPS) * g


def _dot(a, b):
    return jnp.dot(a, b, preferred_element_type=F32)


def _dot_nt(a, b):
    return lax.dot_general(a, b, (((1,), (1,)), ((), ())), preferred_element_type=F32)


def _ffn_body(x_ref, gin_ref, wg_ref, wu_ref, wd_ref, gout_ref, *rest, emit_x, n_side, keep_weights):
    side_in, rest = rest[:n_side], rest[n_side:]
    if emit_x:
        xo_ref, ho_ref = rest[:2]
        rest = rest[2:]
    else:
        ho_ref = rest[0]
        rest = rest[1:]
    side_out, rest = rest[:n_side], rest[n_side:]
    if keep_weights:
        weight_out, rest = rest[:3], rest[3:]
    h_scr, acc_scr = rest
    j = pl.program_id(1)

    @pl.when(j == 0)
    def _():
        h_scr[...] = _rmsnorm(x_ref[...], gin_ref[...]).astype(BF16)
        acc_scr[...] = jnp.zeros_like(acc_scr)

    wg, wu, wd = wg_ref[...], wu_ref[...], wd_ref[...]
    if keep_weights:
        wg, wu, wd = wg.astype(BF16), wu.astype(BF16), wd.astype(BF16)
        for dst, tile in zip(weight_out, (wg, wu, wd)):
            dst[...] = tile

    h = h_scr[...]
    g = _dot(h, wg)
    u = _dot(h, wu)
    act = (g * (1.0 / (1.0 + jnp.exp(-g))) * u).astype(BF16)
    acc_scr[...] += _dot(act, wd)

    for src, dst in zip(side_in, side_out):
        dst[...] = src[...].astype(dst.dtype)

    @pl.when(j == pl.num_programs(1) - 1)
    def _():
        xn = x_ref[...] + FFN_SCALE * acc_scr[...]
        if emit_x:
            xo_ref[...] = xn
        ho_ref[...] = _rmsnorm(xn, gout_ref[...]).astype(ho_ref.dtype)


def _grid_slab(shape, n_i, n_j):
    rows, cols = shape
    for nr, nc, imap in ((n_i, n_j, lambda i, j: (i, j)), (n_j, n_i, lambda i, j: (j, i))):
        if (rows % nr == 0 and cols % nc == 0 and (rows // nr) % BF16_TILE_ROWS == 0
                and (cols // nc) % LANES == 0):
            return (rows // nr, cols // nc), imap
    return None


def _ffn_grid(n_tokens, d_ff, keep_weights=False):
    tm = min(FFN_TM, n_tokens)
    tf = FFN_TF_F32 if keep_weights else FFN_TF
    assert n_tokens % tm == 0 and d_ff % tf == 0
    return n_tokens // tm, d_ff // tf


def _ffn(x, g_in, w_gu, w_dn, g_out, *, emit_x, h_dtype, side_casts=(), keep_weights=False):
    T, D = x.shape
    d_ff = w_dn.shape[0]
    grid = _ffn_grid(T, d_ff, keep_weights)
    tm, tf = T // grid[0], d_ff // grid[1]
    n_f = grid[1]
    w_gate, w_up = w_gu if isinstance(w_gu, tuple) else (w_gu, w_gu)
    up0 = 0 if isinstance(w_gu, tuple) else n_f
    w_dtype = w_dn.dtype
    assert (w_dtype == F32) == keep_weights and (grid[0] == 1 or not keep_weights)
    in_specs = [
        pl.BlockSpec((tm, D), lambda i, j: (i, 0)),
        pl.BlockSpec((1, D), lambda i, j: (0, 0)),
        pl.BlockSpec((D, tf), lambda i, j: (0, j)),
        pl.BlockSpec((D, tf), lambda i, j: (0, j + up0)),
        pl.BlockSpec((tf, D), lambda i, j: (j, 0)),
        pl.BlockSpec((1, D), lambda i, j: (0, 0)),
    ]
    out_shape = [jax.ShapeDtypeStruct((T, D), h_dtype)]
    out_specs = [pl.BlockSpec((tm, D), lambda i, j: (i, 0))]
    if emit_x:
        out_shape = [jax.ShapeDtypeStruct((T, D), F32)] + out_shape
        out_specs = [pl.BlockSpec((tm, D), lambda i, j: (i, 0))] + out_specs
    side_bytes = 0
    for a in side_casts:
        blk, imap = _grid_slab(a.shape, *grid)
        in_specs.append(pl.BlockSpec(blk, imap))
        out_specs.append(pl.BlockSpec(blk, imap))
        out_shape.append(jax.ShapeDtypeStruct(a.shape, BF16))
        side_bytes += _nbytes(blk, F32) + _nbytes(blk, BF16)
    if keep_weights:
        out_specs += [pl.BlockSpec((D, tf), lambda i, j: (0, j)),
                      pl.BlockSpec((D, tf), lambda i, j: (0, j)),
                      pl.BlockSpec((tf, D), lambda i, j: (j, 0))]
        out_shape += [jax.ShapeDtypeStruct((D, d_ff), BF16), jax.ShapeDtypeStruct((D, d_ff), BF16),
                      jax.ShapeDtypeStruct((d_ff, D), BF16)]
        side_bytes += 3 * _nbytes((D, tf), BF16)
    limit = _vmem_limit(
        _nbytes((tm, D), F32), 3 * _nbytes((D, tf), w_dtype),
        _nbytes((tm, D), h_dtype), _nbytes((tm, D), F32) if emit_x else 0, side_bytes,
        scratch=_nbytes((tm, D), BF16) + _nbytes((tm, D), F32))
    return pl.pallas_call(
        functools.partial(_ffn_body, emit_x=emit_x, n_side=len(side_casts), keep_weights=keep_weights),
        grid=grid, in_specs=in_specs, out_specs=out_specs, out_shape=out_shape,
        scratch_shapes=[pltpu.VMEM((tm, D), BF16), pltpu.VMEM((tm, D), F32)],
        compiler_params=pltpu.CompilerParams(
            dimension_semantics=("arbitrary", "arbitrary"), vmem_limit_bytes=limit),
        name="ffn_mid" if emit_x else "ffn_final",
    )(x, g_in.reshape(1, D), w_gate, w_up, w_dn, g_out.reshape(1, D), *side_casts)


def _sigmoid(x):
    return 1.0 / (1.0 + jnp.exp(-x))


def _mm_body(x_ref, w_ref, *rest, out_scale, has_residual, has_tail, n_side):
    rest = list(rest)
    xt_ref = rest.pop(0) if has_tail else None
    r_ref = rest.pop(0) if has_residual else None
    rt_ref = rest.pop(0) if has_residual and has_tail else None
    side_in, rest = rest[:n_side], rest[n_side:]
    o_ref = rest.pop(0)
    ot_ref = rest.pop(0) if has_tail else None
    side_out, wb_scr = rest[:n_side], rest[n_side]
    i = pl.program_id(1)

    @pl.when(i == 0)
    def _():
        wb_scr[...] = w_ref[...].astype(BF16)

    def project(x, r, o):
        y = _dot(x[...], wb_scr[...])
        if out_scale is not None:
            y = y * out_scale
        if has_residual:
            y = r[...] + y
        o[...] = y.astype(o.dtype)

    if has_tail:
        pl.when(i == 0)(lambda: project(xt_ref, rt_ref, ot_ref))
        pl.when(i > 0)(lambda: project(x_ref, r_ref, o_ref))
    else:
        project(x_ref, r_ref, o_ref)
    for src, dst in zip(side_in, side_out):
        dst[...] = src[...].astype(dst.dtype)


def _mm_grid(n_tokens, n):
    tm, tn = min(MM_TM, n_tokens), min(MM_TN, n)
    assert n_tokens % tm == 0 and n % tn == 0
    return n // tn, n_tokens // tm


def _matmul(x, w, col0, n, out_dtype, residual=None, out_scale=None, side_casts=(), tail=None):
    T, K = x.shape
    n_j, n_main = _mm_grid(T, n)
    tm, tn = T // n_main, n // n_j
    assert col0 % tn == 0
    c0 = col0 // tn
    first = 0 if tail is None else 1
    main = lambda i: jnp.maximum(i - first, 0)
    in_specs = [
        pl.BlockSpec((tm, K), lambda j, i: (main(i), 0)),
        pl.BlockSpec((K, tn), lambda j, i: (0, j + c0)),
    ]
    args = [x, w]
    blocks = [_nbytes((tm, K), x.dtype), _nbytes((K, tn), w.dtype), _nbytes((tm, tn), out_dtype)]
    out_specs = [pl.BlockSpec((tm, tn), lambda j, i: (main(i), j))]
    out_shape = [jax.ShapeDtypeStruct((T, n), out_dtype)]
    if tail is not None:
        x_tail, r_tail = tail
        t_tail = x_tail.shape[0]
        assert t_tail <= tm and (r_tail is None) == (residual is None)
        in_specs.append(pl.BlockSpec((t_tail, K), lambda j, i: (0, 0)))
        args.append(x_tail)
        out_specs.append(pl.BlockSpec((t_tail, tn), lambda j, i: (0, j)))
        out_shape.append(jax.ShapeDtypeStruct((t_tail, n), out_dtype))
        blocks += [_nbytes((t_tail, K), x.dtype), _nbytes((t_tail, tn), out_dtype)]
    if residual is not None:
        in_specs.append(pl.BlockSpec((tm, tn), lambda j, i: (main(i), j)))
        args.append(residual)
        blocks.append(_nbytes((tm, tn), F32))
        if tail is not None:
            in_specs.append(pl.BlockSpec((t_tail, tn), lambda j, i: (0, j)))
            args.append(r_tail)
            blocks.append(_nbytes((t_tail, tn), F32))
    for a in side_casts:
        blk, imap = _grid_slab(a.shape, n_j, n_main)
        slab_map = lambda j, i, imap=imap: imap(j, main(i))
        in_specs.append(pl.BlockSpec(blk, slab_map))
        args.append(a)
        out_specs.append(pl.BlockSpec(blk, slab_map))
        out_shape.append(jax.ShapeDtypeStruct(a.shape, BF16))
        blocks.append(_nbytes(blk, F32) + _nbytes(blk, BF16))
    return pl.pallas_call(
        functools.partial(_mm_body, out_scale=out_scale, has_residual=residual is not None,
                          has_tail=tail is not None, n_side=len(side_casts)),
        grid=(n_j, n_main + (tail is not None)),
        in_specs=in_specs, out_specs=out_specs, out_shape=out_shape,
        scratch_shapes=[pltpu.VMEM((K, tn), BF16)],
        compiler_params=pltpu.CompilerParams(
            dimension_semantics=("arbitrary", "arbitrary"),
            vmem_limit_bytes=_vmem_limit(*blocks, scratch=_nbytes((K, tn), BF16))),
        name="proj",
    )(*args)


def _tri(bk):
    j = lax.broadcasted_iota(jnp.int32, (bk, bk), 0)
    s = lax.broadcasted_iota(jnp.int32, (bk, bk), 1)
    return (j >= s).astype(BF16)


LOG2E = 1.4426950408889634
Q_SCALE = HEAD_DIM ** -0.5 * LOG2E


def _softplus2(z2):
    return jnp.maximum(z2, 0.0) + jnp.log(1.0 + jnp.exp2(-jnp.abs(z2))) * LOG2E


def _cumsum_mxu(sp, tri):
    hi = sp.astype(BF16)
    lo = (sp - hi.astype(F32)).astype(BF16)
    if sp.shape[1] % 128 == 0:
        return _dot(jnp.concatenate([hi, lo], axis=1), jnp.concatenate([tri, tri], axis=0))
    return _dot(hi, tri) + _dot(lo, tri)


def _sb_weights(z2, tri, run, masked):
    sp = _softplus2(z2)
    if masked is not None:
        sp = masked(sp)
    tot = _cumsum_mxu(sp, tri)
    if run is not None:
        tot = tot + jnp.concatenate([run] * (z2.shape[1] // run.shape[1]), axis=1)
    a = jnp.exp2(z2 - tot)
    if masked is not None:
        a = masked(a)
    return a.astype(BF16), jnp.sum(sp, axis=1, keepdims=True)


def _attn_prompt_body(q_ref, k_ref, v_ref, tri_ref, o_ref, run_scr, acc_scr):
    blk = ATT_BLK
    seq = q_ref.shape[0]
    nb = seq // blk
    tri = tri_ref[...]
    heads = [slice(h * HEAD_DIM, (h + 1) * HEAD_DIM) for h in range(ATT_HEADS_PER_STEP)]
    block = lambda b: slice(b * blk, (b + 1) * blk)

    def key_block(ref, b, cols):
        return ref[block(b), cols].astype(BF16)

    def stacked_scores(cols, shift):
        return jnp.concatenate([_dot_nt(q_ref[block(b), cols], key_block(k_ref, b - shift, cols))
                                for b in range(shift, nb)], axis=0)

    def stacked_values(a, cols, shift):
        return jnp.concatenate([_dot(a[block(b - shift), :], key_block(v_ref, b - shift, cols))
                                for b in range(shift, nb)], axis=0)

    t = lax.broadcasted_iota(jnp.int32, (seq, blk), 0) % blk
    s = lax.broadcasted_iota(jnp.int32, (seq, blk), 1)
    causal = s < t
    run, acc = [], []
    for cols in heads:
        a, tot = _sb_weights(stacked_scores(cols, 0), tri, None, lambda x: jnp.where(causal, x, 0.0))
        run.append(jnp.broadcast_to(tot, (seq, HEAD_DIM)))
        acc.append(stacked_values(a, cols, 0))
    if nb > 1:
        for i, cols in enumerate(heads):
            a, tot = _sb_weights(stacked_scores(cols, 1), tri, run[i][blk:], None)
            run[i] = jnp.concatenate([run[i][:blk], run[i][blk:] + tot], axis=0)
            acc[i] = jnp.concatenate([acc[i][:blk], acc[i][blk:] + stacked_values(a, cols, 1)], axis=0)
    for i, cols in enumerate(heads):
        o_ref[:, cols] = acc[i].astype(o_ref.dtype)
    if nb <= 2:
        return

    far = 2 * blk
    alive = None
    for i in range(len(heads)):
        run_scr[i] = run[i][far:]
        acc_scr[i] = acc[i][far:]
        lowest = jnp.min(run[i][far:])
        alive = lowest if alive is None else jnp.minimum(alive, lowest)

    @pl.when(alive <= ATT_DEAD_LOG2)
    def _():
        run = [run_scr[i] for i in range(len(heads))]
        acc = [acc_scr[i] for i in range(len(heads))]
        for c in range(nb - 3, -1, -1):
            top = c * blk
            for i, cols in enumerate(heads):
                z2 = _dot_nt(q_ref[far + top:, cols], key_block(k_ref, c, cols))
                a, tot = _sb_weights(z2, tri, run[i][top:], None)
                pv = _dot(a, key_block(v_ref, c, cols))
                run[i] = jnp.concatenate([run[i][:top], run[i][top:] + tot], axis=0) if top else run[i] + tot
                acc[i] = jnp.concatenate([acc[i][:top], acc[i][top:] + pv], axis=0) if top else acc[i] + pv
        for i, cols in enumerate(heads):
            o_ref[far:, cols] = acc[i].astype(o_ref.dtype)


def _attn_prompt(q, k, v, batch, seq):
    blk = ATT_BLK
    assert seq % blk == 0 and N_HEADS % ATT_HEADS_PER_STEP == 0
    hd = HEAD_DIM
    width = ATT_HEADS_PER_STEP * hd
    spec = pl.BlockSpec((seq, width), lambda b, h: (b, h))
    return pl.pallas_call(
        _attn_prompt_body, grid=(batch, N_HEADS // ATT_HEADS_PER_STEP),
        in_specs=[spec, spec, spec, pl.BlockSpec((blk, blk), lambda b, h: (0, 0))],
        out_specs=spec,
        out_shape=jax.ShapeDtypeStruct(q.shape, BF16),
        scratch_shapes=[pltpu.VMEM((ATT_HEADS_PER_STEP, max(seq - 2 * blk, F32_TILE_ROWS), hd), F32)] * 2,
        compiler_params=pltpu.CompilerParams(
            dimension_semantics=("parallel", "parallel"),
            vmem_limit_bytes=_vmem_limit(2 * _nbytes((seq, width), F32), 2 * _nbytes((seq, width), BF16),
                                         scratch=8 * ATT_HEADS_PER_STEP * _nbytes((seq, blk), F32))),
        name="attn_prompt",
    )(q, k, v, _tri(blk))


def _head_cols(h):
    return slice(h * HEAD_DIM, (h + 1) * HEAD_DIM)


def _sample_scores(q_ref, load_k):
    return jnp.concatenate([_dot_nt(q_ref[:, _head_cols(h)], load_k(h)) for h in range(N_HEADS)], axis=0)


def _sample_values(a, load_v, n_new):
    return jnp.concatenate([_dot(a[h * n_new:(h + 1) * n_new, :], load_v(h)) for h in range(N_HEADS)], axis=0)


def _sample_cache_sweep(q_ref, kc_ref, vc_ref, tri, run, acc, n_new):
    bk = ATT_BLK
    for sb in range(kc_ref.shape[0] // (bk * N_HEADS) - 1, -1, -1):
        def head_rows_of(ref, h):
            return ref[pl.ds(sb * bk * N_HEADS + h, bk, stride=N_HEADS), :].astype(BF16)
        z = _sample_scores(q_ref, lambda h: head_rows_of(kc_ref, h))
        a, tot = _sb_weights(z, tri, run, None)
        acc = acc + _sample_values(a, lambda h: head_rows_of(vc_ref, h), n_new)
        run = run + tot
    return run, acc


def _sample_store(o_ref, acc, n_new):
    for h in range(N_HEADS):
        o_ref[:, _head_cols(h)] = acc[h * n_new:(h + 1) * n_new, :].astype(o_ref.dtype)


def _attn_sample_near_body(q_ref, kn_ref, vn_ref, kc_ref, vc_ref, tri_ref, trin_ref,
                           o_ref, run_ref, acc_ref, alive_ref, *, n_new):
    rows = N_HEADS * n_new
    z = _sample_scores(q_ref, lambda h: kn_ref[:, _head_cols(h)].astype(BF16))
    t = lax.broadcasted_iota(jnp.int32, (rows, n_new), 0) % n_new
    s = lax.broadcasted_iota(jnp.int32, (rows, n_new), 1)
    a, tot = _sb_weights(z, trin_ref[...], None, lambda x: jnp.where(s < t, x, 0.0))
    acc = _sample_values(a, lambda h: vn_ref[:, _head_cols(h)].astype(BF16), n_new)
    run = jnp.broadcast_to(tot, (rows, HEAD_DIM))
    run, acc = _sample_cache_sweep(q_ref, kc_ref, vc_ref, tri_ref[...], run, acc, n_new)
    _sample_store(o_ref, acc, n_new)
    run_ref[...] = run
    acc_ref[...] = acc
    alive_ref[...] = jnp.broadcast_to((jnp.min(run) <= ATT_DEAD_LOG2).astype(jnp.int32), alive_ref.shape)


def _attn_sample_far_body(alive_ref, src_ref, q_ref, kc_ref, vc_ref, tri_ref, run_ref, acc_ref, near_ref,
                          o_ref, *, n_new):
    del src_ref
    b = pl.program_id(0)

    @pl.when(alive_ref[b] == 0)
    def _():
        o_ref[...] = near_ref[...]

    @pl.when(alive_ref[b] != 0)
    def _():
        _, acc = _sample_cache_sweep(q_ref, kc_ref, vc_ref, tri_ref[...], run_ref[...], acc_ref[...], n_new)
        _sample_store(o_ref, acc, n_new)


def _attn_sample(q, k_new, v_new, cache_k, cache_v, n_streams, n_new, past_len):
    bk = ATT_BLK
    hd = HEAD_DIM
    assert past_len % bk == 0
    d_all = N_HEADS * hd
    rows = N_HEADS * n_new
    near_rows = bk * N_HEADS
    far_rows = (past_len - bk) * N_HEADS
    last_near = past_len // bk - 1

    new_spec = pl.BlockSpec((n_new, d_all), lambda b: (b, 0))
    near_spec = pl.BlockSpec((None, near_rows, hd), lambda b: (b, last_near, 0))
    state_spec = pl.BlockSpec((rows, hd), lambda b: (b, 0))
    alive_spec = pl.BlockSpec((None, F32_TILE_ROWS, LANES), lambda b: (b, 0, 0))
    tri = _tri(bk)
    o_near, run, acc, alive = pl.pallas_call(
        functools.partial(_attn_sample_near_body, n_new=n_new),
        grid=(n_streams,),
        in_specs=[new_spec, new_spec, new_spec, near_spec, near_spec,
                  pl.BlockSpec((bk, bk), lambda b: (0, 0)),
                  pl.BlockSpec((n_new, n_new), lambda b: (0, 0))],
        out_specs=[new_spec, state_spec, state_spec, alive_spec],
        out_shape=[jax.ShapeDtypeStruct(q.shape, BF16),
                   jax.ShapeDtypeStruct((n_streams * rows, hd), F32),
                   jax.ShapeDtypeStruct((n_streams * rows, hd), F32),
                   jax.ShapeDtypeStruct((n_streams, F32_TILE_ROWS, LANES), jnp.int32)],
        compiler_params=pltpu.CompilerParams(
            dimension_semantics=("arbitrary",),
            vmem_limit_bytes=_vmem_limit(2 * _nbytes((near_rows, hd), F32))),
        name="attn_sample_near",
    )(q, k_new, v_new, cache_k, cache_v, tri, _tri(n_new))
    if far_rows == 0:
        return o_near

    alive = alive[:, 0, 0]
    src = lax.cummax(jnp.where(alive > 0, jnp.arange(n_streams, dtype=jnp.int32), 0))
    row_map = lambda b, alive, src: (b, 0)
    far_spec = pl.BlockSpec((None, far_rows, hd), lambda b, alive, src: (src[b], 0, 0))
    return pl.pallas_call(
        functools.partial(_attn_sample_far_body, n_new=n_new),
        grid_spec=pltpu.PrefetchScalarGridSpec(
            num_scalar_prefetch=2, grid=(n_streams,),
            in_specs=[pl.BlockSpec((n_new, d_all), row_map), far_spec, far_spec,
                      pl.BlockSpec((bk, bk), lambda b, alive, src: (0, 0)),
                      pl.BlockSpec((rows, hd), row_map), pl.BlockSpec((rows, hd), row_map),
                      pl.BlockSpec((n_new, d_all), row_map)],
            out_specs=pl.BlockSpec((n_new, d_all), row_map)),
        out_shape=jax.ShapeDtypeStruct(q.shape, BF16),
        compiler_params=pltpu.CompilerParams(
            dimension_semantics=("arbitrary",),
            vmem_limit_bytes=_vmem_limit(2 * _nbytes((far_rows, hd), F32))),
        name="attn_sample_far",
    )(alive, src, q, cache_k, cache_v, tri, run, acc, o_near)


def _merge(cb, conv, o, ga, gb, wco_ref, wao_ref, m_ref):
    y_a = _dot((cb * conv).astype(BF16), wco_ref[...])
    y_b = _dot(o, wao_ref[...])
    m_ref[...] = (_sigmoid(ga) * y_a + _sigmoid(gb) * y_b).astype(m_ref.dtype)


def _mix_prompt_body(cb_ref, cc_ref, cx_ref, o_ref, ga_ref, gb_ref, cw_ref, wco_ref, wao_ref,
                     m_ref, hist_ref, carry_scr, *, tiles_per_seq):
    i = pl.program_id(0)
    tm = cc_ref.shape[0]

    @pl.when(i % tiles_per_seq == 0)
    def _():
        carry_scr[...] = jnp.zeros_like(carry_scr)

    u = cc_ref[...] * cx_ref[...]
    row = lax.broadcasted_iota(jnp.int32, u.shape, 0)
    p1 = carry_scr[F32_TILE_ROWS - 1:F32_TILE_ROWS, :]
    p2 = carry_scr[F32_TILE_ROWS - 2:F32_TILE_ROWS - 1, :]
    u1 = jnp.where(row == 0, p1, pltpu.roll(u, 1, axis=0))
    u2 = jnp.where(row == 0, p2, jnp.where(row == 1, p1, pltpu.roll(u, 2, axis=0)))
    conv = u2 * cw_ref[0:1, :] + u1 * cw_ref[1:2, :] + u * cw_ref[2:3, :]
    carry_scr[...] = u[tm - F32_TILE_ROWS:, :]
    hist_ref[...] = u[tm - (CONV_W - 1):, :]
    _merge(cb_ref[...], conv, o_ref[...], ga_ref[...], gb_ref[...], wco_ref, wao_ref, m_ref)


def _mix_sample_body(cb_ref, cc_ref, cx_ref, o_ref, ga_ref, gb_ref, cw_ref, wco_ref, wao_ref,
                     p1_ref, p2_ref, m_ref, u_ref, *, n_new):
    u = cc_ref[...] * cx_ref[...]
    t = lax.broadcasted_iota(jnp.int32, u.shape, 0) % n_new
    u1 = jnp.where(t == 0, p1_ref[...], pltpu.roll(u, 1, axis=0))
    u2 = jnp.where(t < 2, p2_ref[...], pltpu.roll(u, 2, axis=0))
    conv = u2 * cw_ref[0:1, :] + u1 * cw_ref[1:2, :] + u * cw_ref[2:3, :]
    u_ref[...] = u
    _merge(cb_ref[...], conv, o_ref[...], ga_ref[...], gb_ref[...], wco_ref, wao_ref, m_ref)


def _mix_prompt(pconv, o, gates, conv_w, w_co, w_ao, batch, seq):
    T = pconv.shape[0]
    C = pconv.shape[1] // 3
    D = gates.shape[1] // 2
    Da = o.shape[1]
    tm = MIX_TM
    assert seq % tm == 0
    tps = seq // tm
    row = lambda c: (lambda i: (i, c))
    const = lambda i: (0, 0)
    m, hist = pl.pallas_call(
        functools.partial(_mix_prompt_body, tiles_per_seq=tps),
        grid=(T // tm,),
        in_specs=[
            pl.BlockSpec((tm, C), row(0)), pl.BlockSpec((tm, C), row(1)), pl.BlockSpec((tm, C), row(2)),
            pl.BlockSpec((tm, Da), row(0)),
            pl.BlockSpec((tm, D), row(0)), pl.BlockSpec((tm, D), row(1)),
            pl.BlockSpec((CONV_W, C), const),
            pl.BlockSpec((C, D), const, pipeline_mode=pl.Buffered(1)),
            pl.BlockSpec((Da, D), const, pipeline_mode=pl.Buffered(1)),
        ],
        out_specs=[pl.BlockSpec((tm, D), row(0)),
                   pl.BlockSpec((None, CONV_W - 1, C), lambda i: (i // tps, 0, 0))],
        out_shape=[jax.ShapeDtypeStruct((T, D), BF16),
                   jax.ShapeDtypeStruct((batch, CONV_W - 1, C), F32)],
        scratch_shapes=[pltpu.VMEM((F32_TILE_ROWS, C), F32)],
        compiler_params=pltpu.CompilerParams(
            dimension_semantics=("arbitrary",),
            vmem_limit_bytes=_vmem_limit(3 * _nbytes((tm, C), F32), _nbytes((tm, Da), BF16),
                                         2 * _nbytes((tm, D), gates.dtype), _nbytes((C, D), BF16),
                                         _nbytes((Da, D), BF16), _nbytes((tm, D), BF16))),
        name="mix_prompt",
    )(pconv, pconv, pconv, o, gates, gates, conv_w, w_co, w_ao)
    return m, hist


def _mix_sample(pconv, o, gates, conv_w, w_co, w_ao, state, n_streams, n_new):
    T = pconv.shape[0]
    C = pconv.shape[1] // 3
    D = gates.shape[1] // 2
    Da = o.shape[1]
    assert T == n_streams * n_new and n_new >= CONV_W - 1
    zeros = jnp.zeros((n_streams, n_new, C), F32)
    p1 = zeros.at[:, 0].set(state[:, 1]).reshape(T, C)
    p2 = zeros.at[:, 0].set(state[:, 0]).at[:, 1].set(state[:, 1]).reshape(T, C)
    col = lambda c: (lambda i: (0, c))
    const = lambda i: (0, 0)
    m, u = pl.pallas_call(
        functools.partial(_mix_sample_body, n_new=n_new),
        grid=(1,),
        in_specs=[
            pl.BlockSpec((T, C), col(0)), pl.BlockSpec((T, C), col(1)), pl.BlockSpec((T, C), col(2)),
            pl.BlockSpec((T, Da), const),
            pl.BlockSpec((T, D), col(0)), pl.BlockSpec((T, D), col(1)),
            pl.BlockSpec((CONV_W, C), const),
            pl.BlockSpec((C, D), const), pl.BlockSpec((Da, D), const),
            pl.BlockSpec((T, C), const), pl.BlockSpec((T, C), const),
        ],
        out_specs=[pl.BlockSpec((T, D), const), pl.BlockSpec((T, C), const)],
        out_shape=[jax.ShapeDtypeStruct((T, D), BF16), jax.ShapeDtypeStruct((T, C), F32)],
        compiler_params=pltpu.CompilerParams(
            dimension_semantics=("arbitrary",),
            vmem_limit_bytes=_vmem_limit(6 * _nbytes((T, C), F32), _nbytes((T, Da), BF16),
                                         2 * _nbytes((T, D), gates.dtype), _nbytes((C, D), BF16),
                                         _nbytes((Da, D), BF16), _nbytes((T, D), BF16))),
        name="mix_sample",
    )(pconv, pconv, pconv, o, gates, gates, conv_w, w_co, w_ao, p1, p2)
    new_hist = u.reshape(n_streams, n_new, C)[:, n_new - (CONV_W - 1):]
    return m, new_hist


def _project_both(hp, hs, w, col0, n, out_dtype, residuals=(None, None), **kw):
    if hs.shape[0] <= min(MM_TM, hp.shape[0]):
        out_p, out_s, *side = _matmul(hp, w, col0, n, out_dtype, residual=residuals[0],
                                      tail=(hs, residuals[1]), **kw)
    else:
        out_p, *side = _matmul(hp, w, col0, n, out_dtype, residual=residuals[0], **kw)
        out_s, = _matmul(hs, w, col0, n, out_dtype, residual=residuals[1],
                         out_scale=kw.get("out_scale"))
    return out_p, out_s, side


def _mixers(xp1, hp, xs1, hs, w, *, prompt_shape, sample_shape, sample_state):
    C = w["conv_w"].shape[1]
    Da = N_HEADS * HEAD_DIM
    D = xp1.shape[1]
    w_in = w["w_in"]
    pconv_p, pconv_s, _ = _project_both(hp, hs, w_in, 0, 3 * C, F32)
    q_p, q_s, _ = _project_both(hp, hs, w_in, 3 * C, Da, BF16, out_scale=Q_SCALE)
    k_p, k_s, _ = _project_both(hp, hs, w_in, 3 * C + Da, Da, F32)
    v_p, v_s, _ = _project_both(hp, hs, w_in, 3 * C + 2 * Da, Da, F32)
    merge_f32 = (w["w_co"], w["w_ao"])
    fuse = all(_grid_slab(a.shape, *_mm_grid(hp.shape[0], 2 * D)) for a in merge_f32)
    gates_p, gates_s, merge_w = _project_both(hp, hs, w_in, 3 * C + 3 * Da, 2 * D, F32,
                                              side_casts=merge_f32 if fuse else ())
    w_co, w_ao = merge_w if fuse else [a.astype(BF16) for a in merge_f32]

    B, T = prompt_shape
    o_p = _attn_prompt(q_p, k_p, v_p, B, T)
    m_p, hist_p = _mix_prompt(pconv_p, o_p, gates_p, w["conv_w"], w_co, w_ao, B, T)

    S, n_new = sample_shape
    conv_state, cache_k, cache_v, past_len = sample_state
    o_s = _attn_sample(q_s, k_s, v_s, cache_k.reshape(S, past_len * N_HEADS, HEAD_DIM),
                       cache_v.reshape(S, past_len * N_HEADS, HEAD_DIM), S, n_new, past_len)
    m_s, hist_s = _mix_sample(pconv_s, o_s, gates_s, w["conv_w"], w_co, w_ao, conv_state, S, n_new)

    x2_p, x2_s, _ = _project_both(m_p, m_s, w["w_o"], 0, D, F32, residuals=(xp1, xs1))
    return (x2_p, hist_p, k_p, v_p), (x2_s, hist_s, k_s, v_s)


def kernel(x_prompt, x_sample, cache_k, cache_v, state_conv, norm_ffn1, ffn1_w_gate_up, ffn1_w_down,
           norm_mix, w_in, conv_w, w_conv_out, w_attn_out, w_o, norm_ffn2, ffn2_w_gate_up,
           ffn2_w_down, norm_final):
    depth = w_in.shape[0]
    B, T, D = x_prompt.shape
    S, n_new, _ = x_sample.shape
    past_len = cache_k.shape[2]
    xp = x_prompt.reshape(B * T, D)
    xs = x_sample.reshape(S * n_new, D)
    outs = [[] for _ in range(6)]
    for l in range(depth):
        w = {"w_in": w_in[l], "conv_w": conv_w[l], "w_co": w_conv_out[l], "w_ao": w_attn_out[l],
             "w_o": w_o[l]}
        ffn1 = dict(emit_x=True, h_dtype=BF16)
        d_ff = ffn1_w_down.shape[1]
        if _ffn_grid(S * n_new, d_ff)[0] == 1:
            xs1, hs, w_gate1, w_up1, w_dn1 = _ffn(xs, norm_ffn1[l], ffn1_w_gate_up[l], ffn1_w_down[l],
                                                  norm_mix[l], keep_weights=True, **ffn1)
            w_gu1 = (w_gate1, w_up1)
        else:
            w_gu1, w_dn1 = ffn1_w_gate_up[l].astype(BF16), ffn1_w_down[l].astype(BF16)
            xs1, hs = _ffn(xs, norm_ffn1[l], w_gu1, w_dn1, norm_mix[l], **ffn1)
        ffn2_f32 = (ffn2_w_gate_up[l], ffn2_w_down[l])
        fuse = all(_grid_slab(a.shape, *_ffn_grid(B * T, d_ff)) for a in ffn2_f32)
        xp1, hp, *ffn2_bf = _ffn(xp, norm_ffn1[l], w_gu1, w_dn1, norm_mix[l],
                                 side_casts=ffn2_f32 if fuse else (), **ffn1)
        if not fuse:
            ffn2_bf = [a.astype(BF16) for a in ffn2_f32]
        (xp2, c_p, k_p, v_p), (xs2, c_s, k_s, v_s) = _mixers(
            xp1, hp, xs1, hs, w, prompt_shape=(B, T), sample_shape=(S, n_new),
            sample_state=(state_conv[l], cache_k[l], cache_v[l], past_len))
        w2 = (norm_ffn2[l], *ffn2_bf, norm_final)
        if l == depth - 1:
            xp = _ffn(xp2, *w2, emit_x=False, h_dtype=F32)[0]
            xs = _ffn(xs2, *w2, emit_x=False, h_dtype=F32)[0]
        else:
            xp = _ffn(xp2, *w2, emit_x=True, h_dtype=BF16)[0]
            xs = _ffn(xs2, *w2, emit_x=True, h_dtype=BF16)[0]
        for lst, val in zip(outs, (k_p.reshape(B, T, N_HEADS, HEAD_DIM), v_p.reshape(B, T, N_HEADS, HEAD_DIM),
                                   c_p, k_s.reshape(S, n_new, N_HEADS, HEAD_DIM),
                                   v_s.reshape(S, n_new, N_HEADS, HEAD_DIM), c_s)):
            lst.append(val)
    y_prompt = xp.reshape(B, T, D)
    y_sample = xs.reshape(S, n_new, D)
    return (y_prompt, y_sample) + tuple(jnp.stack(o, axis=0) for o in outs)
```
